```python
import jax, jax.numpy as jnp
from jax import lax
import numpy as np

D_MODEL = 1024
BATCH = 4
SEQ = 4096
DEPTH = 1

MLA_HEADS = 8
QK_NOPE_DIM = 64
QK_ROPE_DIM = 32
QK_HEAD_DIM = QK_NOPE_DIM + QK_ROPE_DIM
V_HEAD_DIM = 64
Q_LORA_RANK = 384
KV_LORA_RANK = 256
ROPE_BASE = 10000.0
Q_BLOCK = 128
HG_HEADS = 4
HG_KEY_DIM = 128
HG_VAL_DIM = 128
HG_WIDTH_K = HG_HEADS * HG_KEY_DIM
HG_WIDTH_V = HG_HEADS * HG_VAL_DIM
HG_CHUNK = 64
N_BRANCH = 2
BRANCH_WIDTH = MLA_HEADS * V_HEAD_DIM
FFN_HIDDEN = ((8 * D_MODEL // 3 + 255) // 256) * 256
PLE_DIM = 256
EPS = 1e-6

COL_SIZES = (Q_LORA_RANK, KV_LORA_RANK, QK_ROPE_DIM,
             HG_WIDTH_K, HG_WIDTH_K, HG_WIDTH_V, HG_WIDTH_V,
             N_BRANCH * D_MODEL)
IN_COLS = sum(COL_SIZES)

kernel_name = "hybrid_mla_hgrn2_gated_block"


def rms_norm(x, gain):
    xf = x.astype(jnp.float32)
    y = xf * lax.rsqrt(jnp.mean(xf * xf, axis=-1, keepdims=True) + EPS)
    return (y * gain.astype(jnp.float32)).astype(x.dtype)


def apply_rope(x, positions):
    r = x.shape[-1]
    half = r // 2
    inv_freq = jnp.exp(-jnp.log(ROPE_BASE) * jnp.arange(half, dtype=jnp.float32) * 2.0 / r)
    ang = positions.astype(jnp.float32)[..., None] * inv_freq
    cos = jnp.cos(ang)[:, :, None, :]
    sin = jnp.sin(ang)[:, :, None, :]
    xf = x.astype(jnp.float32)
    x1, x2 = xf[..., :half], xf[..., half:]
    out = jnp.concatenate([x1 * cos - x2 * sin, x2 * cos + x1 * sin], axis=-1)
    return out.astype(x.dtype)


def causal_block_attention(q, k, v):
    b, s, h, d = q.shape
    nb = s // Q_BLOCK
    qb = q.reshape(b, nb, Q_BLOCK, h, d).transpose(1, 0, 2, 3, 4)
    kpos = jnp.arange(s)
    scale = d ** -0.5

    def one_block(args):
        qi, bi = args
        sc = jnp.einsum('bqhd,bkhd->bhqk', qi, k, preferred_element_type=jnp.float32) * scale
        qpos = bi * Q_BLOCK + jnp.arange(Q_BLOCK)
        mask = kpos[None, :] <= qpos[:, None]
        sc = jnp.where(mask, sc, -jnp.inf)
        pr = jax.nn.softmax(sc, axis=-1).astype(v.dtype)
        return jnp.einsum('bhqk,bkhd->bqhd', pr, v)

    out = lax.map(one_block, (qb, jnp.arange(nb)))
    return out.transpose(1, 0, 2, 3, 4).reshape(b, s, h, v.shape[-1])


def hgrn2_chunked(q, k, v, log_f):
    b, s, h, kd = q.shape
    vd = v.shape[-1]
    nc = s // HG_CHUNK

    def to_chunks(t):
        return t.reshape(b, nc, HG_CHUNK, h, t.shape[-1]).transpose(1, 0, 3, 2, 4)

    qc, kc, vc, gc = to_chunks(q), to_chunks(k), to_chunks(v), to_chunks(log_f)
    causal = jnp.tril(jnp.ones((HG_CHUNK, HG_CHUNK), dtype=bool))[:, :, None]

    def step(state, inp):
        qi, ki, vi, gi = inp
        cum = jnp.cumsum(gi, axis=2)
        o_inter = jnp.einsum('bhck,bhkv->bhcv', qi * jnp.exp(cum), state)
        diff = cum[:, :, :, None, :] - cum[:, :, None, :, :]
        decay = jnp.exp(jnp.where(causal, diff, -jnp.inf))
        att = jnp.einsum('bhtk,bhsk,bhtsk->bhts', qi, ki, decay)
        o_intra = jnp.einsum('bhts,bhsv->bhtv', att, vi)
        last = cum[:, :, -1:, :]
        new_state = (jnp.exp(last[:, :, 0, :])[..., None] * state
                     + jnp.einsum('bhck,bhcv->bhkv', ki * jnp.exp(last - cum), vi))
        return new_state, o_inter + o_intra

    s0 = jnp.zeros((b, h, kd, vd), jnp.float32)
    _, o = lax.scan(step, s0, (qc, kc, vc, gc))
    return o.transpose(1, 0, 3, 2, 4).reshape(b, s, h, vd)


def setup_inputs(seed: int = 0) -> dict:
    key = jax.random.key(seed)
    ks = jax.random.split(key, 24)

    def w(k, shape, fan_in):
        return jax.random.normal(k, shape, jnp.float32) * (fan_in ** -0.5)

    def gain(k, shape):
        return 1.0 + 0.02 * jax.random.normal(k, shape, jnp.float32)

    x = jax.random.normal(ks[0], (BATCH, SEQ, D_MODEL), jnp.float32)
    p = jax.random.normal(ks[1], (DEPTH, BATCH, SEQ, PLE_DIM), jnp.float32)
    offsets = jax.random.randint(ks[2], (BATCH, 1), 0, 1024, dtype=jnp.int32)
    positions = offsets + jnp.arange(SEQ, dtype=jnp.int32)[None, :]
    return {
        "x": x,
        "p": p,
        "positions": positions,
        "mix_norm_g": gain(ks[3], (DEPTH, D_MODEL)),
        "w_in": w(ks[4], (DEPTH, D_MODEL, IN_COLS), D_MODEL),
        "q_a_norm_g": gain(ks[5], (DEPTH, Q_LORA_RANK)),
        "w_uq": w(ks[6], (DEPTH, Q_LORA_RANK, MLA_HEADS * QK_HEAD_DIM), Q_LORA_RANK),
        "kv_a_norm_g": gain(ks[7], (DEPTH, KV_LORA_RANK)),
        "w_ukv": w(ks[8], (DEPTH, KV_LORA_RANK, MLA_HEADS * (QK_NOPE_DIM + V_HEAD_DIM)), KV_LORA_RANK),
        "q_norm_g": gain(ks[9], (DEPTH, QK_HEAD_DIM)),
        "k_norm_g": gain(ks[10], (DEPTH, QK_HEAD_DIM)),
        "hg_lb_logits": 0.5 * jax.random.normal(ks[11], (DEPTH + 1, HG_WIDTH_K), jnp.float32),
        "hg_out_norm_g": gain(ks[12], (DEPTH, HG_VAL_DIM)),
        "w_branch": w(ks[13], (DEPTH, N_BRANCH, BRANCH_WIDTH, D_MODEL), BRANCH_WIDTH),
        "w_out": w(ks[14], (DEPTH, D_MODEL, D_MODEL), D_MODEL),
        "ffn_norm_g": gain(ks[15], (DEPTH, D_MODEL)),
        "w_ffn_gate": w(ks[16], (DEPTH, D_MODEL, FFN_HIDDEN), D_MODEL),
        "w_ffn_up": w(ks[17], (DEPTH, D_MODEL, FFN_HIDDEN), D_MODEL),
        "w_ffn_down": w(ks[18], (DEPTH, FFN_HIDDEN, D_MODEL), FFN_HIDDEN),
        "ple_gate_norm_g": gain(ks[19], (DEPTH, D_MODEL)),
        "w_ple_gate": w(ks[20], (DEPTH, D_MODEL, D_MODEL), D_MODEL),
        "w_ple_proj": w(ks[21], (DEPTH, PLE_DIM, D_MODEL), PLE_DIM),
        "ple_post_norm_g": gain(ks[22], (DEPTH, D_MODEL)),
    }


def reference(x, p, positions, mix_norm_g, w_in, q_a_norm_g, w_uq, kv_a_norm_g, w_ukv,
              q_norm_g, k_norm_g, hg_lb_logits, hg_out_norm_g, w_branch, w_out,
              ffn_norm_g, w_ffn_gate, w_ffn_up, w_ffn_down,
              ple_gate_norm_g, w_ple_gate, w_ple_proj, ple_post_norm_g):
    b, s, _ = x.shape
    split_points = np.cumsum(COL_SIZES)[:-1].tolist()
    lower_bounds = jnp.cumsum(jax.nn.softmax(hg_lb_logits.astype(jnp.float32), axis=0), axis=0)

    for layer in range(DEPTH):
        h = rms_norm(x, mix_norm_g[layer])
        proj = h @ w_in[layer]
        c_q, c_kv, k_rope_raw, hq, hf, hi, hg, br_gates = jnp.split(proj, split_points, axis=-1)

        q = (rms_norm(c_q, q_a_norm_g[layer]) @ w_uq[layer]).reshape(b, s, MLA_HEADS, QK_HEAD_DIM)
        kv = (rms_norm(c_kv, kv_a_norm_g[layer]) @ w_ukv[layer]).reshape(
            b, s, MLA_HEADS, QK_NOPE_DIM + V_HEAD_DIM)
        k_nope, v = kv[..., :QK_NOPE_DIM], kv[..., QK_NOPE_DIM:]
        k_rope = jnp.broadcast_to(k_rope_raw[:, :, None, :], (b, s, MLA_HEADS, QK_ROPE_DIM))
        k = jnp.concatenate([k_nope, k_rope], axis=-1)
        q = rms_norm(q, q_norm_g[layer])
        k = rms_norm(k, k_norm_g[layer])
        q = jnp.concatenate([q[..., :QK_NOPE_DIM], apply_rope(q[..., QK_NOPE_DIM:], positions)], axis=-1)
        k = jnp.concatenate([k[..., :QK_NOPE_DIM], apply_rope(k[..., QK_NOPE_DIM:], positions)], axis=-1)
        attn = causal_block_attention(q, k, v).reshape(b, s, BRANCH_WIDTH)

        lb = lower_bounds[layer]
        f = lb + (1.0 - lb) * jax.nn.sigmoid(hf.astype(jnp.float32))
        log_f = jnp.log(f)
        hk = 1.0 - f
        o = hgrn2_chunked(
            hq.astype(jnp.float32).reshape(b, s, HG_HEADS, HG_KEY_DIM),
            hk.reshape(b, s, HG_HEADS, HG_KEY_DIM),
            hi.astype(jnp.float32).reshape(b, s, HG_HEADS, HG_VAL_DIM),
            log_f.reshape(b, s, HG_HEADS, HG_KEY_DIM))
        o = rms_norm(o, hg_out_norm_g[layer]) * jax.nn.silu(
            hg.astype(jnp.float32).reshape(b, s, HG_HEADS, HG_VAL_DIM))
        rec = o.reshape(b, s, HG_WIDTH_V).astype(x.dtype)

        branches = jnp.stack([attn, rec], axis=2)
        y = jnp.einsum('bsgc,gcd->bsgd', branches, w_branch[layer])
        gates = jax.nn.sigmoid(br_gates.reshape(b, s, N_BRANCH, D_MODEL))
        x = x + jnp.sum(gates * y, axis=2) @ w_out[layer]

        h2 = rms_norm(x, ffn_norm_g[layer])
        x = x + (jax.nn.silu(h2 @ w_ffn_gate[layer]) * (h2 @ w_ffn_up[layer])) @ w_ffn_down[layer]

        e = rms_norm(p[layer] @ w_ple_proj[layer], ple_post_norm_g[layer])
        g = jax.nn.sigmoid(rms_norm(x, ple_gate_norm_g[layer]) @ w_ple_gate[layer])
        x = x + g * e
    return x
```

```python
import functools
import math

import jax
import jax.numpy as jnp
from jax import lax
from jax.experimental import pallas as pl
from jax.experimental.pallas import tpu as pltpu

D_MODEL = 1024
MLA_HEADS = 8
QK_NOPE_DIM = 64
QK_ROPE_DIM = 32
ROPE_HALF = QK_ROPE_DIM // 2
QK_HEAD_DIM = QK_NOPE_DIM + QK_ROPE_DIM
V_HEAD_DIM = 64
Q_LORA_RANK = 384
KV_LORA_RANK = 256
ROPE_BASE = 10000.0
HG_HEADS = 4
HG_DIM = 128
HG_WIDTH = HG_HEADS * HG_DIM
BRANCH_WIDTH = MLA_HEADS * V_HEAD_DIM
FFN_HIDDEN = 2816
PLE_DIM = 256
EPS = 1e-6

LANES = 128
HEAD_PAD = LANES
DENOM_LANE = V_HEAD_DIM

VMEM_LIMIT = 56 * 1024 * 1024

TOK_TILE = 512
ATT_TILE = 512
HG_BLOCK = 256
HG_CHUNK = 64
HG_SUB = 16
FFN_SPLITS = (1024, 1024, 768)

BF16 = jnp.bfloat16
F32 = jnp.float32


def _const_spec(shape):
    nd = len(shape)
    return pl.BlockSpec(shape, lambda *_: (0,) * nd, pipeline_mode=pl.Buffered(1))


def _rms(v, gain, width=None):
    n = v.shape[-1] if width is None else width
    ms = jnp.sum(v * v, axis=-1, keepdims=True) * (1.0 / n)
    return v * lax.rsqrt(ms + EPS) * gain


def _dot(a, b):
    return jnp.dot(a, b, preferred_element_type=F32)


def _dot_nt(a, b):
    return lax.dot_general(a, b, (((1,), (1,)), ((), ())), preferred_element_type=F32)


def _input_kernel(x_ref, pos_ref, gmix_ref, wlat_ref, gqa_ref, wuq_ref, gkva_ref, wkv_ref,
                  gq_ref, gk_ref, wh_ref, wg_ref,
                  q_ref, k_ref, v_ref, hq_ref, hf_ref, hi_ref, hg_ref, gates_ref):
    x = x_ref[...]
    h = _rms(x, gmix_ref[...]).astype(BF16)

    lat = _dot(h, wlat_ref[...])
    cq = _rms(lat[:, :Q_LORA_RANK], gqa_ref[...]).astype(BF16)
    ckv = _rms(lat[:, Q_LORA_RANK:Q_LORA_RANK + KV_LORA_RANK], gkva_ref[...]).astype(BF16)
    k_rope = lat[:, Q_LORA_RANK + KV_LORA_RANK:]

    lane = lax.broadcasted_iota(jnp.int32, (1, LANES), 1)
    in_x1 = (lane >= QK_NOPE_DIM) & (lane < QK_NOPE_DIM + ROPE_HALF)
    in_x2 = (lane >= QK_NOPE_DIM + ROPE_HALF) & (lane < QK_HEAD_DIM)
    fidx = jnp.where(in_x2, lane - (QK_NOPE_DIM + ROPE_HALF),
                     jnp.where(in_x1, lane - QK_NOPE_DIM, 0)).astype(F32)
    inv_freq = jnp.exp(fidx * (-math.log(ROPE_BASE) * 2.0 / QK_ROPE_DIM))
    ang = pos_ref[...].astype(F32) * inv_freq
    cos = jnp.cos(ang)
    sin = jnp.sin(ang)
    c_tab = jnp.where(lane < QK_NOPE_DIM, 1.0, jnp.where(in_x1 | in_x2, cos, 0.0))
    s_fwd = jnp.where(in_x2, sin, 0.0)
    s_bwd = jnp.where(in_x1, -sin, 0.0)

    def qk_norm_rope(t, gain):
        t = _rms(t, gain, width=QK_HEAD_DIM)
        return (t * c_tab + pltpu.roll(t, ROPE_HALF, 1) * s_fwd
                + pltpu.roll(t, LANES - ROPE_HALF, 1) * s_bwd)

    q_all = _dot(cq, wuq_ref[...])
    kv_all = _dot(ckv, wkv_ref[...])
    scale = QK_HEAD_DIM ** -0.5
    one_lane = (lane == DENOM_LANE).astype(F32)
    gq = gq_ref[...]
    gk = gk_ref[...]
    for hd in range(MLA_HEADS):
        sl = slice(hd * HEAD_PAD, (hd + 1) * HEAD_PAD)
        q_ref[0, hd] = (qk_norm_rope(q_all[:, sl], gq) * scale).astype(BF16)
        k_ref[0, hd] = qk_norm_rope(kv_all[:, sl] + k_rope, gk).astype(BF16)
        vsl = slice(MLA_HEADS * HEAD_PAD + hd * HEAD_PAD, MLA_HEADS * HEAD_PAD + (hd + 1) * HEAD_PAD)
        v_ref[0, hd] = (kv_all[:, vsl] + one_lane).astype(BF16)

    hh = _dot(h, wh_ref[...])
    hq_ref[...] = hh[:, :HG_WIDTH].astype(BF16)
    hf_ref[...] = hh[:, HG_WIDTH:2 * HG_WIDTH]
    hi_ref[...] = hh[:, 2 * HG_WIDTH:3 * HG_WIDTH].astype(BF16)
    hg_ref[...] = hh[:, 3 * HG_WIDTH:].astype(BF16)

    gates_ref[...] = jax.nn.sigmoid(_dot(h, wg_ref[...])).astype(BF16)


def _input_stage(x2d, pos2d, b, s, gmix, wlat, gqa, wuq, gkva, wkv, gq, gk, wh, wg):
    t = x2d.shape[0]
    tm = TOK_TILE
    nt = s // tm

    def tok(width):
        return pl.BlockSpec((tm, width), lambda i: (i, 0))

    def head_spec():
        return pl.BlockSpec((1, MLA_HEADS, tm, HEAD_PAD), lambda i: (i // nt, 0, i % nt, 0))

    head_shape = jax.ShapeDtypeStruct((b, MLA_HEADS, s, HEAD_PAD), BF16)
    consts = (gmix, wlat, gqa, wuq, gkva, wkv, gq, gk, wh, wg)
    return pl.pallas_call(
        _input_kernel,
        grid=(t // tm,),
        in_specs=[tok(D_MODEL), tok(1)] + [_const_spec(c.shape) for c in consts],
        out_specs=[head_spec(), head_spec(), head_spec(),
                   tok(HG_WIDTH), tok(HG_WIDTH), tok(HG_WIDTH), tok(HG_WIDTH), tok(2 * D_MODEL)],
        out_shape=[head_shape, head_shape, head_shape,
                   jax.ShapeDtypeStruct((t, HG_WIDTH), BF16),
                   jax.ShapeDtypeStruct((t, HG_WIDTH), F32),
                   jax.ShapeDtypeStruct((t, HG_WIDTH), BF16),
                   jax.ShapeDtypeStruct((t, HG_WIDTH), BF16),
                   jax.ShapeDtypeStruct((t, 2 * D_MODEL), BF16)],
        compiler_params=pltpu.CompilerParams(
            dimension_semantics=("arbitrary",), vmem_limit_bytes=VMEM_LIMIT),
        name="input_stage",
    )(x2d, pos2d, *consts)


def _attention_kernel(q_ref, k_ref, v_ref, o_ref, m_ref, acc_ref):
    qi = pl.program_id(2)
    tq = ATT_TILE
    row = lax.broadcasted_iota(jnp.int32, (tq, tq), 0)
    col = lax.broadcasted_iota(jnp.int32, (tq, tq), 1)
    outs = []
    for hd in range(2):
        q = q_ref[0, hd]
        m_ref[...] = jnp.full(m_ref.shape, -jnp.inf, F32)
        acc_ref[...] = jnp.zeros(acc_ref.shape, F32)

        def block(j, masked, q=q, hd=hd):
            start = pl.multiple_of(j * tq, tq)
            kj = k_ref[0, hd, pl.ds(start, tq), :]
            vj = v_ref[0, hd, pl.ds(start, tq), :]
            sc = _dot_nt(q, kj)
            if masked:
                sc = jnp.where(col <= row, sc, -jnp.inf)
            m_old = m_ref[...]
            m_new = jnp.maximum(m_old, jnp.max(sc, axis=-1, keepdims=True))
            p = jnp.exp(sc - m_new).astype(BF16)
            acc_ref[...] = jnp.exp(m_old - m_new) * acc_ref[...] + _dot(p, vj)
            m_ref[...] = m_new

        def body(j, carry):
            block(j, False)
            return carry

        lax.fori_loop(0, qi, body, 0)
        block(qi, True)
        acc = acc_ref[...]
        outs.append(acc[:, :V_HEAD_DIM] / acc[:, DENOM_LANE:DENOM_LANE + 1])
    o_ref[0] = jnp.concatenate(outs, axis=-1).astype(BF16)


def _attention_stage(q, k, v):
    b, nh, s, _ = q.shape
    tq = ATT_TILE
    return pl.pallas_call(
        _attention_kernel,
        grid=(b, nh // 2, s // tq),
        in_specs=[pl.BlockSpec((1, 2, tq, HEAD_PAD), lambda bi, hp, i: (bi, hp, i, 0)),
                  pl.BlockSpec((1, 2, s, HEAD_PAD), lambda bi, hp, i: (bi, hp, 0, 0)),
                  pl.BlockSpec((1, 2, s, HEAD_PAD), lambda bi, hp, i: (bi, hp, 0, 0))],
        out_specs=pl.BlockSpec((1, tq, 2 * V_HEAD_DIM), lambda bi, hp, i: (bi, i, hp)),
        out_shape=jax.ShapeDtypeStruct((b, s, BRANCH_WIDTH), BF16),
        scratch_shapes=[pltpu.VMEM((tq, 1), F32), pltpu.VMEM((tq, HEAD_PAD), F32)],
        compiler_params=pltpu.CompilerParams(
            dimension_semantics=("arbitrary", "arbitrary", "arbitrary"),
            vmem_limit_bytes=VMEM_LIMIT),
        name="causal_attention",
    )(q, k, v)


def _hgrn_chunk(q, hf, v_bf, lb, state_t, tril, ones_sq):
    c = HG_CHUNK
    f = lb + (1.0 - lb) * jax.nn.sigmoid(hf)
    lf = jnp.log(f)
    kk = 1.0 - f
    lf_hi = lf.astype(BF16)
    lf_lo = (lf - lf_hi.astype(F32)).astype(BF16)
    cum = _dot(tril, lf_hi) + _dot(tril, lf_lo)
    last = cum[c - 1:c, :]

    o = _dot_nt((q * jnp.exp(cum)).astype(BF16), state_t.astype(BF16))

    lane = lax.broadcasted_iota(jnp.int32, (HG_SUB, LANES), 1)
    sub_row = lax.broadcasted_iota(jnp.int32, (HG_SUB, LANES), 0)
    chunk_row = lax.broadcasted_iota(jnp.int32, (c, LANES), 0)
    att_rows = []
    for i in range(c // HG_SUB):
        lo = i * HG_SUB
        rows = slice(lo, lo + HG_SUB)
        cum_i = cum[rows]
        q_i = q[rows]
        base = cum[lo:lo + 1, :]
        if i > 0:
            qt = (q_i * jnp.exp(cum_i - base)).astype(BF16)
            kt = jnp.where(chunk_row < lo, kk * jnp.exp(jnp.minimum(base - cum, 0.0)), 0.0)
            att_i = _dot_nt(qt, kt.astype(BF16))
            att_i = jnp.concatenate([att_i, jnp.zeros((HG_SUB, LANES - c), F32)], axis=-1)
        else:
            att_i = jnp.zeros((HG_SUB, LANES), F32)
        prods = []
        for s_ in range(HG_SUB):
            r = lo + s_
            e = jnp.exp(jnp.minimum(cum_i - cum[r:r + 1, :], 0.0))
            prods.append(jnp.where(sub_row >= s_, q_i * kk[r:r + 1, :] * e, 0.0).astype(BF16))
        red = _dot(jnp.concatenate(prods, axis=0), ones_sq)
        for s_ in range(HG_SUB):
            att_i = jnp.where(lane == lo + s_, red[s_ * HG_SUB:(s_ + 1) * HG_SUB], att_i)
        att_rows.append(att_i)
    att = jnp.concatenate(att_rows, axis=0)[:, :c]
    o = o + _dot(att.astype(BF16), v_bf)

    kd = (kk * jnp.exp(last - cum)).astype(BF16)
    v_t = v_bf.astype(F32).T.astype(BF16)
    new_state_t = state_t * jnp.exp(last) + _dot(v_t, kd)
    return o, new_state_t


def _hgrn_kernel(hq_ref, hf_ref, hi_ref, hg_ref, lb_ref, gout_ref, o_ref, state_ref):
    @pl.when(pl.program_id(1) == 0)
    def _():
        state_ref[...] = jnp.zeros(state_ref.shape, F32)

    c = HG_CHUNK
    r_i = lax.broadcasted_iota(jnp.int32, (c, c), 0)
    c_i = lax.broadcasted_iota(jnp.int32, (c, c), 1)
    tril = (c_i <= r_i).astype(BF16)
    ones_sq = jnp.ones((LANES, LANES), BF16)
    gout = gout_ref[...]
    for hd in range(HG_HEADS):
        sl = slice(hd * HG_DIM, (hd + 1) * HG_DIM)
        lb = lb_ref[:, sl]
        state_t = state_ref[hd]
        for ci in range(HG_BLOCK // c):
            rows = slice(ci * c, (ci + 1) * c)
            o, state_t = _hgrn_chunk(hq_ref[0, rows, sl].astype(F32), hf_ref[0, rows, sl],
                                     hi_ref[0, rows, sl], lb, state_t, tril, ones_sq)
            gate = hg_ref[0, rows, sl].astype(F32)
            o_ref[0, rows, sl] = (_rms(o, gout) * (gate * jax.nn.sigmoid(gate))).astype(BF16)
        state_ref[hd] = state_t


def _hgrn_stage(hq, hf, hi, hg, lb, gout):
    b, s, _ = hq.shape
    tb = HG_BLOCK
    tok = pl.BlockSpec((1, tb, HG_WIDTH), lambda bi, i: (bi, i, 0))
    return pl.pallas_call(
        _hgrn_kernel,
        grid=(b, s // tb),
        in_specs=[tok, tok, tok, tok, _const_spec(lb.shape), _const_spec(gout.shape)],
        out_specs=tok,
        out_shape=jax.ShapeDtypeStruct((b, s, HG_WIDTH), BF16),
        scratch_shapes=[pltpu.VMEM((HG_HEADS, HG_DIM, HG_DIM), F32)],
        compiler_params=pltpu.CompilerParams(
            dimension_semantics=("arbitrary", "arbitrary"), vmem_limit_bytes=VMEM_LIMIT),
        name="hgrn2_recurrence",
    )(hq, hf, hi, hg, lb, gout)


def _output_kernel(x_ref, attn_ref, rec_ref, gates_ref, p_ref,
                   wb0_ref, wb1_ref, wout_ref, gffn_ref, wgate_ref, wup_ref, wdown_ref,
                   gpg_ref, wpg_ref, wpp_ref, gpost_ref, o_ref):
    y0 = _dot(attn_ref[...], wb0_ref[...])
    y1 = _dot(rec_ref[...], wb1_ref[...])
    g0 = gates_ref[:, :D_MODEL].astype(F32)
    g1 = gates_ref[:, D_MODEL:].astype(F32)
    merged = (g0 * y0 + g1 * y1).astype(BF16)
    x1 = x_ref[...] + _dot(merged, wout_ref[...])

    h2 = _rms(x1, gffn_ref[...]).astype(BF16)
    x2 = x1
    lo = 0
    for width in FFN_SPLITS:
        a = _dot(h2, wgate_ref[:, lo:lo + width])
        u = _dot(h2, wup_ref[:, lo:lo + width])
        z = (a * jax.nn.sigmoid(a) * u).astype(BF16)
        x2 = x2 + _dot(z, wdown_ref[lo:lo + width, :])
        lo += width

    e = _rms(_dot(p_ref[...].astype(BF16), wpp_ref[...]), gpost_ref[...])
    g = jax.nn.sigmoid(_dot(_rms(x2, gpg_ref[...]).astype(BF16), wpg_ref[...]))
    o_ref[...] = x2 + g * e


def _output_stage(x2d, attn, rec, gates, p2d, wb0, wb1, wout, gffn, wgate, wup, wdown,
                  gpg, wpg, wpp, gpost):
    t = x2d.shape[0]
    tm = TOK_TILE

    def tok(width):
        return pl.BlockSpec((tm, width), lambda i: (i, 0))

    consts = (wb0, wb1, wout, gffn, wgate, wup, wdown, gpg, wpg, wpp, gpost)
    return pl.pallas_call(
        _output_kernel,
        grid=(t // tm,),
        in_specs=[tok(D_MODEL), tok(BRANCH_WIDTH), tok(HG_WIDTH), tok(2 * D_MODEL), tok(PLE_DIM)]
        + [_const_spec(c.shape) for c in consts],
        out_specs=tok(D_MODEL),
        out_shape=jax.ShapeDtypeStruct((t, D_MODEL), F32),
        compiler_params=pltpu.CompilerParams(
            dimension_semantics=("arbitrary",), vmem_limit_bytes=VMEM_LIMIT),
        name="output_stage",
    )(x2d, attn, rec, gates, p2d, *consts)


def _pad_heads(w, heads, width):
    rows = w.shape[0]
    w = w.reshape(rows, heads, width)
    w = jnp.pad(w, ((0, 0), (0, 0), (0, HEAD_PAD - width)))
    return w.reshape(rows, heads * HEAD_PAD)


def _row(v):
    return v.reshape(1, -1).astype(F32)


def _pad_row(v, width):
    return jnp.pad(v.astype(F32), (0, width - v.shape[0])).reshape(1, width)


def kernel(x, p, positions, mix_norm_g, w_in, q_a_norm_g, w_uq, kv_a_norm_g, w_ukv, q_norm_g, k_norm_g, hg_lb_logits, hg_out_norm_g, w_branch, w_out, ffn_norm_g, w_ffn_gate, w_ffn_up, w_ffn_down, ple_gate_norm_g, w_ple_gate, w_ple_proj, ple_post_norm_g):
    b, s, d = x.shape
    depth = w_in.shape[0]
    lower_bounds = jnp.cumsum(jax.nn.softmax(hg_lb_logits.astype(F32), axis=0), axis=0)
    pos2d = positions.reshape(b * s, 1)
    x2d = x.reshape(b * s, d)

    for layer in range(depth):
        wi = w_in[layer]
        c0 = Q_LORA_RANK + KV_LORA_RANK
        c1 = c0 + QK_ROPE_DIM
        rope_tile = jnp.pad(wi[:, c0:c1], ((0, 0), (QK_NOPE_DIM, HEAD_PAD - QK_HEAD_DIM)))
        wlat = jnp.concatenate([wi[:, :c0], rope_tile], axis=1).astype(BF16)
        wh = wi[:, c1:c1 + 4 * HG_WIDTH].astype(BF16)
        wg = wi[:, c1 + 4 * HG_WIDTH:].astype(BF16)
        wuq = _pad_heads(w_uq[layer], MLA_HEADS, QK_HEAD_DIM).astype(BF16)
        wukv = w_ukv[layer].reshape(KV_LORA_RANK, MLA_HEADS, QK_NOPE_DIM + V_HEAD_DIM)
        wk = _pad_heads(wukv[:, :, :QK_NOPE_DIM].reshape(KV_LORA_RANK, -1), MLA_HEADS, QK_NOPE_DIM)
        wv = _pad_heads(wukv[:, :, QK_NOPE_DIM:].reshape(KV_LORA_RANK, -1), MLA_HEADS, V_HEAD_DIM)
        wkv = jnp.concatenate([wk, wv], axis=1).astype(BF16)

        q, k, v, hq, hf, hi, hg, gates = _input_stage(
            x2d, pos2d, b, s, _row(mix_norm_g[layer]), wlat, _row(q_a_norm_g[layer]), wuq,
            _row(kv_a_norm_g[layer]), wkv, _pad_row(q_norm_g[layer], HEAD_PAD),
            _pad_row(k_norm_g[layer], HEAD_PAD), wh, wg)

        attn = _attention_stage(q, k, v)
        rec = _hgrn_stage(hq.reshape(b, s, HG_WIDTH), hf.reshape(b, s, HG_WIDTH),
                          hi.reshape(b, s, HG_WIDTH), hg.reshape(b, s, HG_WIDTH),
                          _row(lower_bounds[layer]), _row(hg_out_norm_g[layer]))

        x2d = _output_stage(
            x2d, attn.reshape(b * s, BRANCH_WIDTH), rec.reshape(b * s, HG_WIDTH), gates,
            p[layer].reshape(b * s, PLE_DIM),
            w_branch[layer, 0].astype(BF16), w_branch[layer, 1].astype(BF16),
            w_out[layer].astype(BF16), _row(ffn_norm_g[layer]),
            w_ffn_gate[layer].astype(BF16), w_ffn_up[layer].astype(BF16),
            w_ffn_down[layer].astype(BF16), _row(ple_gate_norm_g[layer]),
            w_ple_gate[layer].astype(BF16), w_ple_proj[layer].astype(BF16),
            _row(ple_post_norm_g[layer]))
    return x2d.reshape(b, s, d)
```

```python
import functools
import math

import jax
import jax.numpy as jnp
from jax import lax
from jax.experimental import pallas as pl
from jax.experimental.pallas import tpu as pltpu

D_MODEL = 1024
MLA_HEADS = 8
QK_NOPE_DIM = 64
QK_ROPE_DIM = 32
ROPE_HALF = QK_ROPE_DIM // 2
QK_HEAD_DIM = QK_NOPE_DIM + QK_ROPE_DIM
V_HEAD_DIM = 64
Q_LORA_RANK = 384
KV_LORA_RANK = 256
ROPE_BASE = 10000.0
HG_HEADS = 4
HG_DIM = 128
HG_WIDTH = HG_HEADS * HG_DIM
BRANCH_WIDTH = MLA_HEADS * V_HEAD_DIM
FFN_HIDDEN = 2816
PLE_DIM = 256
EPS = 1e-6

LANES = 128
HEAD_PAD = LANES
DENOM_LANE = V_HEAD_DIM

VMEM_LIMIT = 56 * 1024 * 1024

TOK_TILE = 512
ATT_TILE = 512
HG_BLOCK = 256
HG_CHUNK = 64
HG_SUB = 16
FFN_SPLITS = (1024, 1024, 768)

BF16 = jnp.bfloat16
F32 = jnp.float32


def _const_spec(shape):
    nd = len(shape)
    return pl.BlockSpec(shape, lambda *_: (0,) * nd, pipeline_mode=pl.Buffered(1))


def _rms(v, gain, width=None):
    n = v.shape[-1] if width is None else width
    ms = jnp.sum(v * v, axis=-1, keepdims=True) * (1.0 / n)
    return v * lax.rsqrt(ms + EPS) * gain


def _dot(a, b):
    return jnp.dot(a, b, preferred_element_type=F32)


def _dot_nt(a, b):
    return lax.dot_general(a, b, (((1,), (1,)), ((), ())), preferred_element_type=F32)


def _input_kernel(x_ref, pos_ref, gmix_ref, wlat_ref, gqa_ref, wuq_ref, gkva_ref, wkv_ref,
                  gq_ref, gk_ref, wh_ref, wg_ref,
                  q_ref, k_ref, v_ref, hq_ref, hf_ref, hi_ref, hg_ref, gates_ref):
    x = x_ref[...]
    h = _rms(x, gmix_ref[...]).astype(BF16)

    lat = _dot(h, wlat_ref[...])
    cq = _rms(lat[:, :Q_LORA_RANK], gqa_ref[...]).astype(BF16)
    ckv = _rms(lat[:, Q_LORA_RANK:Q_LORA_RANK + KV_LORA_RANK], gkva_ref[...]).astype(BF16)
    k_rope = lat[:, Q_LORA_RANK + KV_LORA_RANK:]

    lane = lax.broadcasted_iota(jnp.int32, (1, LANES), 1)
    in_x1 = (lane >= QK_NOPE_DIM) & (lane < QK_NOPE_DIM + ROPE_HALF)
    in_x2 = (lane >= QK_NOPE_DIM + ROPE_HALF) & (lane < QK_HEAD_DIM)
    fidx = jnp.where(in_x2, lane - (QK_NOPE_DIM + ROPE_HALF),
                     jnp.where(in_x1, lane - QK_NOPE_DIM, 0)).astype(F32)
    inv_freq = jnp.exp(fidx * (-math.log(ROPE_BASE) * 2.0 / QK_ROPE_DIM))
    ang = pos_ref[...].astype(F32) * inv_freq
    cos = jnp.cos(ang)
    sin = jnp.sin(ang)
    c_tab = jnp.where(lane < QK_NOPE_DIM, 1.0, jnp.where(in_x1 | in_x2, cos, 0.0))
    s_fwd = jnp.where(in_x2, sin, 0.0)
    s_bwd = jnp.where(in_x1, -sin, 0.0)

    def qk_norm_rope(t, gain):
        t = _rms(t, gain, width=QK_HEAD_DIM)
        return (t * c_tab + pltpu.roll(t, ROPE_HALF, 1) * s_fwd
                + pltpu.roll(t, LANES - ROPE_HALF, 1) * s_bwd)

    q_all = _dot(cq, wuq_ref[...])
    kv_all = _dot(ckv, wkv_ref[...])
    scale = QK_HEAD_DIM ** -0.5 * math.log2(math.e)
    one_lane = (lane == DENOM_LANE).astype(F32)
    gq = gq_ref[...]
    gk = gk_ref[...]
    for hd in range(MLA_HEADS):
        sl = slice(hd * HEAD_PAD, (hd + 1) * HEAD_PAD)
        q_ref[0, hd] = (qk_norm_rope(q_all[:, sl], gq) * scale).astype(BF16)
        k_ref[0, hd] = qk_norm_rope(kv_all[:, sl] + k_rope, gk).astype(BF16)
        vsl = slice(MLA_HEADS * HEAD_PAD + hd * HEAD_PAD, MLA_HEADS * HEAD_PAD + (hd + 1) * HEAD_PAD)
        v_ref[0, hd] = (kv_all[:, vsl] + one_lane).astype(BF16)

    hh = _dot(h, wh_ref[...])
    hq_ref[...] = hh[:, :HG_WIDTH].astype(BF16)
    hf_ref[...] = hh[:, HG_WIDTH:2 * HG_WIDTH]
    hi_ref[...] = hh[:, 2 * HG_WIDTH:3 * HG_WIDTH].astype(BF16)
    hg_ref[...] = hh[:, 3 * HG_WIDTH:].astype(BF16)

    gates_ref[...] = jax.nn.sigmoid(_dot(h, wg_ref[...])).astype(BF16)


def _input_stage(x2d, pos2d, b, s, gmix, wlat, gqa, wuq, gkva, wkv, gq, gk, wh, wg):
    t = x2d.shape[0]
    tm = TOK_TILE
    nt = s // tm

    def tok(width):
        return pl.BlockSpec((tm, width), lambda i: (i, 0))

    def head_spec():
        return pl.BlockSpec((1, MLA_HEADS, tm, HEAD_PAD), lambda i: (i // nt, 0, i % nt, 0))

    head_shape = jax.ShapeDtypeStruct((b, MLA_HEADS, s, HEAD_PAD), BF16)
    consts = (gmix, wlat, gqa, wuq, gkva, wkv, gq, gk, wh, wg)
    return pl.pallas_call(
        _input_kernel,
        grid=(t // tm,),
        in_specs=[tok(D_MODEL), tok(1)] + [_const_spec(c.shape) for c in consts],
        out_specs=[head_spec(), head_spec(), head_spec(),
                   tok(HG_WIDTH), tok(HG_WIDTH), tok(HG_WIDTH), tok(HG_WIDTH), tok(2 * D_MODEL)],
        out_shape=[head_shape, head_shape, head_shape,
                   jax.ShapeDtypeStruct((t, HG_WIDTH), BF16),
                   jax.ShapeDtypeStruct((t, HG_WIDTH), F32),
                   jax.ShapeDtypeStruct((t, HG_WIDTH), BF16),
                   jax.ShapeDtypeStruct((t, HG_WIDTH), BF16),
                   jax.ShapeDtypeStruct((t, 2 * D_MODEL), BF16)],
        compiler_params=pltpu.CompilerParams(
            dimension_semantics=("arbitrary",), vmem_limit_bytes=VMEM_LIMIT),
        name="input_stage",
    )(x2d, pos2d, *consts)


def _attention_kernel(q_ref, k_ref, v_ref, o_ref, m_ref, acc_ref, sa_ref, sb_ref):
    qi = pl.program_id(2)
    tq = ATT_TILE
    row = lax.broadcasted_iota(jnp.int32, (tq, tq), 0)
    col = lax.broadcasted_iota(jnp.int32, (tq, tq), 1)
    m_ref[...] = jnp.full(m_ref.shape, -jnp.inf, F32)
    acc_ref[...] = jnp.zeros(acc_ref.shape, F32)

    def scores(j, s_ref):
        start = pl.multiple_of(j * tq, tq)
        for hd in range(2):
            s_ref[hd] = _dot_nt(q_ref[0, hd], k_ref[0, hd, pl.ds(start, tq), :])

    def accumulate(j, s_ref, masked):
        for hd in range(2):
            accumulate_head(j, hd, s_ref[hd], masked)

    def accumulate_head(j, hd, sc, masked):
        start = pl.multiple_of(j * tq, tq)
        if masked:
            sc = jnp.where(col <= row, sc, -jnp.inf)
        m_old = m_ref[hd]
        m_new = jnp.maximum(m_old, jnp.max(sc, axis=-1, keepdims=True))
        p = jnp.exp2(sc - jnp.concatenate([m_new] * (tq // LANES), axis=1)).astype(BF16)
        acc_ref[hd] = (jnp.exp2(m_old - m_new) * acc_ref[hd]
                       + _dot(p, v_ref[0, hd, pl.ds(start, tq), :]))
        m_ref[hd] = m_new

    scores(0, sa_ref)

    def body(t, carry):
        j = 2 * t
        scores(j + 1, sb_ref)
        accumulate(j, sa_ref, False)
        scores(j + 2, sa_ref)
        accumulate(j + 1, sb_ref, False)
        return carry

    lax.fori_loop(0, qi // 2, body, 0)

    @pl.when(qi % 2 == 0)
    def _():
        accumulate(qi, sa_ref, True)

    @pl.when(qi % 2 == 1)
    def _():
        scores(qi, sb_ref)
        accumulate(qi - 1, sa_ref, False)
        accumulate(qi, sb_ref, True)

    outs = []
    for hd in range(2):
        acc = acc_ref[hd]
        outs.append(acc[:, :V_HEAD_DIM] / acc[:, DENOM_LANE:DENOM_LANE + 1])
    o_ref[0] = jnp.concatenate(outs, axis=-1).astype(BF16)


def _attention_stage(q, k, v):
    b, nh, s, _ = q.shape
    tq = ATT_TILE
    return pl.pallas_call(
        _attention_kernel,
        grid=(b, nh // 2, s // tq),
        in_specs=[pl.BlockSpec((1, 2, tq, HEAD_PAD), lambda bi, hp, i: (bi, hp, i, 0)),
                  pl.BlockSpec((1, 2, s, HEAD_PAD), lambda bi, hp, i: (bi, hp, 0, 0)),
                  pl.BlockSpec((1, 2, s, HEAD_PAD), lambda bi, hp, i: (bi, hp, 0, 0))],
        out_specs=pl.BlockSpec((1, tq, 2 * V_HEAD_DIM), lambda bi, hp, i: (bi, i, hp)),
        out_shape=jax.ShapeDtypeStruct((b, s, BRANCH_WIDTH), BF16),
        scratch_shapes=[pltpu.VMEM((2, tq, LANES), F32), pltpu.VMEM((2, tq, HEAD_PAD), F32),
                        pltpu.VMEM((2, tq, tq), F32), pltpu.VMEM((2, tq, tq), F32)],
        compiler_params=pltpu.CompilerParams(
            dimension_semantics=("arbitrary", "arbitrary", "arbitrary"),
            vmem_limit_bytes=VMEM_LIMIT),
        name="causal_attention",
    )(q, k, v)


def _hgrn_chunk(q, hf, v_bf, lb, state_t, tril, ones_sq):
    c = HG_CHUNK
    f = lb + (1.0 - lb) * jax.nn.sigmoid(hf)
    lf = jnp.log(f)
    kk = 1.0 - f
    lf_hi = lf.astype(BF16)
    lf_lo = (lf - lf_hi.astype(F32)).astype(BF16)
    cum = _dot(tril, lf_hi) + _dot(tril, lf_lo)
    last = cum[c - 1:c, :]

    o = _dot_nt((q * jnp.exp(cum)).astype(BF16), state_t.astype(BF16))

    lane = lax.broadcasted_iota(jnp.int32, (HG_SUB, LANES), 1)
    sub_row = lax.broadcasted_iota(jnp.int32, (HG_SUB, LANES), 0)
    chunk_row = lax.broadcasted_iota(jnp.int32, (c, LANES), 0)
    att_rows = []
    for i in range(c // HG_SUB):
        lo = i * HG_SUB
        rows = slice(lo, lo + HG_SUB)
        cum_i = cum[rows]
        q_i = q[rows]
        base = cum[lo:lo + 1, :]
        if i > 0:
            qt = (q_i * jnp.exp(cum_i - base)).astype(BF16)
            kt = jnp.where(chunk_row < lo, kk * jnp.exp(jnp.minimum(base - cum, 0.0)), 0.0)
            att_i = _dot_nt(qt, kt.astype(BF16))
            att_i = jnp.concatenate([att_i, jnp.zeros((HG_SUB, LANES - c), F32)], axis=-1)
        else:
            att_i = jnp.zeros((HG_SUB, LANES), F32)
        prods = []
        for s_ in range(HG_SUB):
            r = lo + s_
            e = jnp.exp(jnp.minimum(cum_i - cum[r:r + 1, :], 0.0))
            prods.append(jnp.where(sub_row >= s_, q_i * kk[r:r + 1, :] * e, 0.0).astype(BF16))
        red = _dot(jnp.concatenate(prods, axis=0), ones_sq)
        for s_ in range(HG_SUB):
            att_i = jnp.where(lane == lo + s_, red[s_ * HG_SUB:(s_ + 1) * HG_SUB], att_i)
        att_rows.append(att_i)
    att = jnp.concatenate(att_rows, axis=0)[:, :c]
    o = o + _dot(att.astype(BF16), v_bf)

    kd = (kk * jnp.exp(last - cum)).astype(BF16)
    v_t = v_bf.astype(F32).T.astype(BF16)
    new_state_t = state_t * jnp.exp(last) + _dot(v_t, kd)
    return o, new_state_t


def _hgrn_kernel(hq_ref, hf_ref, hi_ref, hg_ref, lb_ref, gout_ref, o_ref, state_ref):
    @pl.when(pl.program_id(1) == 0)
    def _():
        state_ref[...] = jnp.zeros(state_ref.shape, F32)

    c = HG_CHUNK
    r_i = lax.broadcasted_iota(jnp.int32, (c, c), 0)
    c_i = lax.broadcasted_iota(jnp.int32, (c, c), 1)
    tril = (c_i <= r_i).astype(BF16)
    ones_sq = jnp.ones((LANES, LANES), BF16)
    gout = gout_ref[...]
    for hd in range(HG_HEADS):
        sl = slice(hd * HG_DIM, (hd + 1) * HG_DIM)
        lb = lb_ref[:, sl]
        state_t = state_ref[hd]
        for ci in range(HG_BLOCK // c):
            rows = slice(ci * c, (ci + 1) * c)
            o, state_t = _hgrn_chunk(hq_ref[0, rows, sl].astype(F32), hf_ref[0, rows, sl],
                                     hi_ref[0, rows, sl], lb, state_t, tril, ones_sq)
            gate = hg_ref[0, rows, sl].astype(F32)
            o_ref[0, rows, sl] = (_rms(o, gout) * (gate * jax.nn.sigmoid(gate))).astype(BF16)
        state_ref[hd] = state_t


def _hgrn_stage(hq, hf, hi, hg, lb, gout):
    b, s, _ = hq.shape
    tb = HG_BLOCK
    tok = pl.BlockSpec((1, tb, HG_WIDTH), lambda bi, i: (bi, i, 0))
    return pl.pallas_call(
        _hgrn_kernel,
        grid=(b, s // tb),
        in_specs=[tok, tok, tok, tok, _const_spec(lb.shape), _const_spec(gout.shape)],
        out_specs=tok,
        out_shape=jax.ShapeDtypeStruct((b, s, HG_WIDTH), BF16),
        scratch_shapes=[pltpu.VMEM((HG_HEADS, HG_DIM, HG_DIM), F32)],
        compiler_params=pltpu.CompilerParams(
            dimension_semantics=("arbitrary", "arbitrary"), vmem_limit_bytes=VMEM_LIMIT),
        name="hgrn2_recurrence",
    )(hq, hf, hi, hg, lb, gout)


def _output_kernel(x_ref, attn_ref, rec_ref, gates_ref, p_ref,
                   wb0_ref, wb1_ref, wout_ref, gffn_ref, wgate_ref, wup_ref, wdown_ref,
                   gpg_ref, wpg_ref, wpp_ref, gpost_ref, o_ref):
    y0 = _dot(attn_ref[...], wb0_ref[...])
    y1 = _dot(rec_ref[...], wb1_ref[...])
    g0 = gates_ref[:, :D_MODEL].astype(F32)
    g1 = gates_ref[:, D_MODEL:].astype(F32)
    merged = (g0 * y0 + g1 * y1).astype(BF16)
    x1 = x_ref[...] + _dot(merged, wout_ref[...])

    h2 = _rms(x1, gffn_ref[...]).astype(BF16)
    x2 = x1
    lo = 0
    for width in FFN_SPLITS:
        a = _dot(h2, wgate_ref[:, lo:lo + width])
        u = _dot(h2, wup_ref[:, lo:lo + width])
        z = (a * jax.nn.sigmoid(a) * u).astype(BF16)
        x2 = x2 + _dot(z, wdown_ref[lo:lo + width, :])
        lo += width

    e = _rms(_dot(p_ref[...].astype(BF16), wpp_ref[...]), gpost_ref[...])
    g = jax.nn.sigmoid(_dot(_rms(x2, gpg_ref[...]).astype(BF16), wpg_ref[...]))
    o_ref[...] = x2 + g * e


def _output_stage(x2d, attn, rec, gates, p2d, wb0, wb1, wout, gffn, wgate, wup, wdown,
                  gpg, wpg, wpp, gpost):
    t = x2d.shape[0]
    tm = TOK_TILE

    def tok(width):
        return pl.BlockSpec((tm, width), lambda i: (i, 0))

    consts = (wb0, wb1, wout, gffn, wgate, wup, wdown, gpg, wpg, wpp, gpost)
    return pl.pallas_call(
        _output_kernel,
        grid=(t // tm,),
        in_specs=[tok(D_MODEL), tok(BRANCH_WIDTH), tok(HG_WIDTH), tok(2 * D_MODEL), tok(PLE_DIM)]
        + [_const_spec(c.shape) for c in consts],
        out_specs=tok(D_MODEL),
        out_shape=jax.ShapeDtypeStruct((t, D_MODEL), F32),
        compiler_params=pltpu.CompilerParams(
            dimension_semantics=("arbitrary",), vmem_limit_bytes=VMEM_LIMIT),
        name="output_stage",
    )(x2d, attn, rec, gates, p2d, *consts)


def _pad_heads(w, heads, width):
    rows = w.shape[0]
    w = w.reshape(rows, heads, width)
    w = jnp.pad(w, ((0, 0), (0, 0), (0, HEAD_PAD - width)))
    return w.reshape(rows, heads * HEAD_PAD)


def _row(v):
    return v.reshape(1, -1).astype(F32)


def _pad_row(v, width):
    return jnp.pad(v.astype(F32), (0, width - v.shape[0])).reshape(1, width)


def kernel(x, p, positions, mix_norm_g, w_in, q_a_norm_g, w_uq, kv_a_norm_g, w_ukv, q_norm_g, k_norm_g, hg_lb_logits, hg_out_norm_g, w_branch, w_out, ffn_norm_g, w_ffn_gate, w_ffn_up, w_ffn_down, ple_gate_norm_g, w_ple_gate, w_ple_proj, ple_post_norm_g):
    b, s, d = x.shape
    depth = w_in.shape[0]
    lower_bounds = jnp.cumsum(jax.nn.softmax(hg_lb_logits.astype(F32), axis=0), axis=0)
    pos2d = positions.reshape(b * s, 1)
    x2d = x.reshape(b * s, d)

    for layer in range(depth):
        wi = w_in[layer]
        c0 = Q_LORA_RANK + KV_LORA_RANK
        c1 = c0 + QK_ROPE_DIM
        rope_tile = jnp.pad(wi[:, c0:c1], ((0, 0), (QK_NOPE_DIM, HEAD_PAD - QK_HEAD_DIM)))
        wlat = jnp.concatenate([wi[:, :c0], rope_tile], axis=1).astype(BF16)
        wh = wi[:, c1:c1 + 4 * HG_WIDTH].astype(BF16)
        wg = wi[:, c1 + 4 * HG_WIDTH:].astype(BF16)
        wuq = _pad_heads(w_uq[layer], MLA_HEADS, QK_HEAD_DIM).astype(BF16)
        wukv = w_ukv[layer].reshape(KV_LORA_RANK, MLA_HEADS, QK_NOPE_DIM + V_HEAD_DIM)
        wk = _pad_heads(wukv[:, :, :QK_NOPE_DIM].reshape(KV_LORA_RANK, -1), MLA_HEADS, QK_NOPE_DIM)
        wv = _pad_heads(wukv[:, :, QK_NOPE_DIM:].reshape(KV_LORA_RANK, -1), MLA_HEADS, V_HEAD_DIM)
        wkv = jnp.concatenate([wk, wv], axis=1).astype(BF16)

        q, k, v, hq, hf, hi, hg, gates = _input_stage(
            x2d, pos2d, b, s, _row(mix_norm_g[layer]), wlat, _row(q_a_norm_g[layer]), wuq,
            _row(kv_a_norm_g[layer]), wkv, _pad_row(q_norm_g[layer], HEAD_PAD),
            _pad_row(k_norm_g[layer], HEAD_PAD), wh, wg)

        attn = _attention_stage(q, k, v)
        rec = _hgrn_stage(hq.reshape(b, s, HG_WIDTH), hf.reshape(b, s, HG_WIDTH),
                          hi.reshape(b, s, HG_WIDTH), hg.reshape(b, s, HG_WIDTH),
                          _row(lower_bounds[layer]), _row(hg_out_norm_g[layer]))

        x2d = _output_stage(
            x2d, attn.reshape(b * s, BRANCH_WIDTH), rec.reshape(b * s, HG_WIDTH), gates,
            p[layer].reshape(b * s, PLE_DIM),
            w_branch[layer, 0].astype(BF16), w_branch[layer, 1].astype(BF16),
            w_out[layer].astype(BF16), _row(ffn_norm_g[layer]),
            w_ffn_gate[layer].astype(BF16), w_ffn_up[layer].astype(BF16),
            w_ffn_down[layer].astype(BF16), _row(ple_gate_norm_g[layer]),
            w_ple_gate[layer].astype(BF16), w_ple_proj[layer].astype(BF16),
            _row(ple_post_norm_g[layer]))
    return x2d.reshape(b, s, d)
```

```python
import functools
import math

import numpy as np

import jax
import jax.numpy as jnp
from jax import lax
from jax.experimental import pallas as pl
from jax.experimental.pallas import tpu as pltpu

D_MODEL = 1024
MLA_HEADS = 8
QK_NOPE_DIM = 64
QK_ROPE_DIM = 32
ROPE_HALF = QK_ROPE_DIM // 2
QK_HEAD_DIM = QK_NOPE_DIM + QK_ROPE_DIM
V_HEAD_DIM = 64
Q_LORA_RANK = 384
KV_LORA_RANK = 256
ROPE_BASE = 10000.0
HG_HEADS = 4
HG_DIM = 128
HG_WIDTH = HG_HEADS * HG_DIM
BRANCH_WIDTH = MLA_HEADS * V_HEAD_DIM
FFN_HIDDEN = 2816
PLE_DIM = 256
EPS = 1e-6

LANES = 128
HEAD_PAD = LANES
DENOM_LANE = V_HEAD_DIM

VMEM_LIMIT = 56 * 1024 * 1024

TOK_TILE = 512
ATT_TILE = 512
HG_BLOCK = 256
HG_CHUNK = 64
HG_LEVELS = tuple(HG_CHUNK >> (i + 1) for i in range(HG_CHUNK.bit_length() - 1))
FFN_SPLITS = (1024, 1024, 768)

BF16 = jnp.bfloat16
F32 = jnp.float32


def _const_spec(shape):
    nd = len(shape)
    return pl.BlockSpec(shape, lambda *_: (0,) * nd, pipeline_mode=pl.Buffered(1))


def _rms(v, gain, width=None):
    n = v.shape[-1] if width is None else width
    ms = jnp.sum(v * v, axis=-1, keepdims=True) * (1.0 / n)
    return v * lax.rsqrt(ms + EPS) * gain


def _dot(a, b):
    return jnp.dot(a, b, preferred_element_type=F32)


def _dot_nt(a, b):
    return lax.dot_general(a, b, (((1,), (1,)), ((), ())), preferred_element_type=F32)


def _input_kernel(x_ref, pos_ref, gmix_ref, wlat_ref, gqa_ref, wuq_ref, gkva_ref, wkv_ref,
                  gq_ref, gk_ref, wh_ref, wg_ref,
                  q_ref, k_ref, v_ref, hq_ref, hf_ref, hi_ref, hg_ref, gates_ref):
    x = x_ref[...]
    h = _rms(x, gmix_ref[...]).astype(BF16)

    lat = _dot(h, wlat_ref[...])
    cq = _rms(lat[:, :Q_LORA_RANK], gqa_ref[...]).astype(BF16)
    ckv = _rms(lat[:, Q_LORA_RANK:Q_LORA_RANK + KV_LORA_RANK], gkva_ref[...]).astype(BF16)
    k_rope = lat[:, Q_LORA_RANK + KV_LORA_RANK:]

    lane = lax.broadcasted_iota(jnp.int32, (1, LANES), 1)
    in_x1 = (lane >= QK_NOPE_DIM) & (lane < QK_NOPE_DIM + ROPE_HALF)
    in_x2 = (lane >= QK_NOPE_DIM + ROPE_HALF) & (lane < QK_HEAD_DIM)
    fidx = jnp.where(in_x2, lane - (QK_NOPE_DIM + ROPE_HALF),
                     jnp.where(in_x1, lane - QK_NOPE_DIM, 0)).astype(F32)
    inv_freq = jnp.exp(fidx * (-math.log(ROPE_BASE) * 2.0 / QK_ROPE_DIM))
    ang = pos_ref[...].astype(F32) * inv_freq
    cos = jnp.cos(ang)
    sin = jnp.sin(ang)
    c_tab = jnp.where(lane < QK_NOPE_DIM, 1.0, jnp.where(in_x1 | in_x2, cos, 0.0))
    s_fwd = jnp.where(in_x2, sin, 0.0)
    s_bwd = jnp.where(in_x1, -sin, 0.0)

    def qk_norm_rope(t, gain):
        t = _rms(t, gain, width=QK_HEAD_DIM)
        return (t * c_tab + pltpu.roll(t, ROPE_HALF, 1) * s_fwd
                + pltpu.roll(t, LANES - ROPE_HALF, 1) * s_bwd)

    q_all = _dot(cq, wuq_ref[...])
    kv_all = _dot(ckv, wkv_ref[...])
    scale = QK_HEAD_DIM ** -0.5 * math.log2(math.e)
    one_lane = (lane == DENOM_LANE).astype(F32)
    gq = gq_ref[...]
    gk = gk_ref[...]
    for hd in range(MLA_HEADS):
        sl = slice(hd * HEAD_PAD, (hd + 1) * HEAD_PAD)
        q_ref[0, hd] = (qk_norm_rope(q_all[:, sl], gq) * scale).astype(BF16)
        k_ref[0, hd] = qk_norm_rope(kv_all[:, sl] + k_rope, gk).astype(BF16)
        vsl = slice(MLA_HEADS * HEAD_PAD + hd * HEAD_PAD, MLA_HEADS * HEAD_PAD + (hd + 1) * HEAD_PAD)
        v_ref[0, hd] = (kv_all[:, vsl] + one_lane).astype(BF16)

    hh = _dot(h, wh_ref[...])
    hq_ref[...] = hh[:, :HG_WIDTH].astype(BF16)
    hf_ref[...] = hh[:, HG_WIDTH:2 * HG_WIDTH]
    hi_ref[...] = hh[:, 2 * HG_WIDTH:3 * HG_WIDTH].astype(BF16)
    hg_ref[...] = hh[:, 3 * HG_WIDTH:].astype(BF16)

    gates_ref[...] = jax.nn.sigmoid(_dot(h, wg_ref[...])).astype(BF16)


def _input_stage(x2d, pos2d, b, s, gmix, wlat, gqa, wuq, gkva, wkv, gq, gk, wh, wg):
    t = x2d.shape[0]
    tm = TOK_TILE
    nt = s // tm

    def tok(width):
        return pl.BlockSpec((tm, width), lambda i: (i, 0))

    def head_spec():
        return pl.BlockSpec((1, MLA_HEADS, tm, HEAD_PAD), lambda i: (i // nt, 0, i % nt, 0))

    head_shape = jax.ShapeDtypeStruct((b, MLA_HEADS, s, HEAD_PAD), BF16)
    consts = (gmix, wlat, gqa, wuq, gkva, wkv, gq, gk, wh, wg)
    return pl.pallas_call(
        _input_kernel,
        grid=(t // tm,),
        in_specs=[tok(D_MODEL), tok(1)] + [_const_spec(c.shape) for c in consts],
        out_specs=[head_spec(), head_spec(), head_spec(),
                   tok(HG_WIDTH), tok(HG_WIDTH), tok(HG_WIDTH), tok(HG_WIDTH), tok(2 * D_MODEL)],
        out_shape=[head_shape, head_shape, head_shape,
                   jax.ShapeDtypeStruct((t, HG_WIDTH), BF16),
                   jax.ShapeDtypeStruct((t, HG_WIDTH), F32),
                   jax.ShapeDtypeStruct((t, HG_WIDTH), BF16),
                   jax.ShapeDtypeStruct((t, HG_WIDTH), BF16),
                   jax.ShapeDtypeStruct((t, 2 * D_MODEL), BF16)],
        compiler_params=pltpu.CompilerParams(
            dimension_semantics=("arbitrary",), vmem_limit_bytes=VMEM_LIMIT),
        name="input_stage",
    )(x2d, pos2d, *consts)


def _attention_kernel(q_ref, k_ref, v_ref, o_ref, m_ref, acc_ref, sa_ref, sb_ref):
    qi = pl.program_id(2)
    tq = ATT_TILE
    row = lax.broadcasted_iota(jnp.int32, (tq, tq), 0)
    col = lax.broadcasted_iota(jnp.int32, (tq, tq), 1)
    m_ref[...] = jnp.full(m_ref.shape, -jnp.inf, F32)
    acc_ref[...] = jnp.zeros(acc_ref.shape, F32)

    def scores(j, s_ref):
        start = pl.multiple_of(j * tq, tq)
        for hd in range(2):
            s_ref[hd] = _dot_nt(q_ref[0, hd], k_ref[0, hd, pl.ds(start, tq), :])

    def accumulate(j, s_ref, masked):
        for hd in range(2):
            accumulate_head(j, hd, s_ref[hd], masked)

    def accumulate_head(j, hd, sc, masked):
        start = pl.multiple_of(j * tq, tq)
        if masked:
            sc = jnp.where(col <= row, sc, -jnp.inf)
        m_old = m_ref[hd]
        m_new = jnp.maximum(m_old, jnp.max(sc, axis=-1, keepdims=True))
        p = jnp.exp2(sc - jnp.concatenate([m_new] * (tq // LANES), axis=1)).astype(BF16)
        acc_ref[hd] = (jnp.exp2(m_old - m_new) * acc_ref[hd]
                       + _dot(p, v_ref[0, hd, pl.ds(start, tq), :]))
        m_ref[hd] = m_new

    scores(0, sa_ref)

    def body(t, carry):
        j = 2 * t
        scores(j + 1, sb_ref)
        accumulate(j, sa_ref, False)
        scores(j + 2, sa_ref)
        accumulate(j + 1, sb_ref, False)
        return carry

    lax.fori_loop(0, qi // 2, body, 0)

    @pl.when(qi % 2 == 0)
    def _():
        accumulate(qi, sa_ref, True)

    @pl.when(qi % 2 == 1)
    def _():
        scores(qi, sb_ref)
        accumulate(qi - 1, sa_ref, False)
        accumulate(qi, sb_ref, True)

    outs = []
    for hd in range(2):
        acc = acc_ref[hd]
        outs.append(acc[:, :V_HEAD_DIM] / acc[:, DENOM_LANE:DENOM_LANE + 1])
    o_ref[0] = jnp.concatenate(outs, axis=-1).astype(BF16)


def _attention_stage(q, k, v):
    b, nh, s, _ = q.shape
    tq = ATT_TILE
    return pl.pallas_call(
        _attention_kernel,
        grid=(b, nh // 2, s // tq),
        in_specs=[pl.BlockSpec((1, 2, tq, HEAD_PAD), lambda bi, hp, i: (bi, hp, i, 0)),
                  pl.BlockSpec((1, 2, s, HEAD_PAD), lambda bi, hp, i: (bi, hp, 0, 0)),
                  pl.BlockSpec((1, 2, s, HEAD_PAD), lambda bi, hp, i: (bi, hp, 0, 0))],
        out_specs=pl.BlockSpec((1, tq, 2 * V_HEAD_DIM), lambda bi, hp, i: (bi, i, hp)),
        out_shape=jax.ShapeDtypeStruct((b, s, BRANCH_WIDTH), BF16),
        scratch_shapes=[pltpu.VMEM((2, tq, LANES), F32), pltpu.VMEM((2, tq, HEAD_PAD), F32),
                        pltpu.VMEM((2, tq, tq), F32), pltpu.VMEM((2, tq, tq), F32)],
        compiler_params=pltpu.CompilerParams(
            dimension_semantics=("arbitrary", "arbitrary", "arbitrary"),
            vmem_limit_bytes=VMEM_LIMIT),
        name="causal_attention",
    )(q, k, v)


def _hgrn_span_matrix():
    c = HG_CHUNK
    x = np.arange(c)[:, None]
    y = np.arange(c)[None, :]
    mats = [(y <= x)]
    for b in HG_LEVELS:
        r = (x // (2 * b)) * (2 * b) + b
        mats.append((y > np.minimum(x, r)) & (y <= np.maximum(x, r)))
    m = np.concatenate(mats, axis=0).astype(np.float32)
    return np.concatenate([m, m], axis=1)


def _hgrn_chunk_local(q, hf, v_bf, lb, span, right_rows, pair_masks):
    c = HG_CHUNK
    f = lb + (1.0 - lb) * jax.nn.sigmoid(hf)
    lf = jnp.log2(f)
    kk = 1.0 - f
    lf_hi = lf.astype(BF16)
    lf_lo = (lf - lf_hi.astype(F32)).astype(BF16)
    spans = _dot(span, jnp.concatenate([lf_hi, lf_lo], axis=0))
    cum = spans[:c]
    last = cum[c - 1:c, :]
    q_dec = (q * jnp.exp2(cum)).astype(BF16)
    k_dec = (kk * jnp.exp2(last - cum)).astype(BF16)
    q_bf = q.astype(BF16)
    k_bf = kk.astype(BF16)
    zs = [(jnp.where(right_rows[lvl], q, kk)
           * jnp.exp2(spans[(lvl + 1) * c:(lvl + 2) * c])).astype(BF16)
          for lvl in range(len(HG_LEVELS))]

    o_intra, incr = [], []
    for hd in range(HG_HEADS):
        sl = slice(hd * HG_DIM, (hd + 1) * HG_DIM)
        att = jnp.where(pair_masks[0], _dot_nt(q_bf[:, sl], k_bf[:, sl]), 0.0)
        for lvl in range(len(HG_LEVELS)):
            z = zs[lvl][:, sl]
            att = jnp.where(pair_masks[lvl + 1], _dot_nt(z, z), att)
        v_h = v_bf[:, sl]
        o_intra.append(_dot(att.astype(BF16), v_h))
        incr.append(_dot(v_h.astype(F32).T.astype(BF16), k_dec[:, sl]))
    return o_intra, q_dec, incr, jnp.exp2(last)


def _hgrn_kernel(hq_ref, hf_ref, hi_ref, hg_ref, lb_ref, gout_ref, span_ref, o_ref, state_ref):
    @pl.when(pl.program_id(1) == 0)
    def _():
        state_ref[...] = jnp.zeros(state_ref.shape, F32)

    c = HG_CHUNK
    t_i = lax.broadcasted_iota(jnp.int32, (c, c), 0)
    s_i = lax.broadcasted_iota(jnp.int32, (c, c), 1)
    row = lax.broadcasted_iota(jnp.int32, (c, HG_WIDTH), 0)
    pair_masks = [t_i == s_i] + [((t_i // b) ^ (s_i // b) == 1) & (s_i < t_i) for b in HG_LEVELS]
    right_rows = [(row // b) % 2 == 1 for b in HG_LEVELS]
    span = span_ref[...]
    lb = lb_ref[...]
    n_chunks = HG_BLOCK // c

    local = []
    for ci in range(n_chunks):
        rows = slice(ci * c, (ci + 1) * c)
        local.append(_hgrn_chunk_local(hq_ref[0, rows, :].astype(F32), hf_ref[0, rows, :],
                                       hi_ref[0, rows, :], lb, span, right_rows, pair_masks))

    gout = gout_ref[...]
    for hd in range(HG_HEADS):
        sl = slice(hd * HG_DIM, (hd + 1) * HG_DIM)
        state_t = state_ref[hd]
        for ci in range(n_chunks):
            rows = slice(ci * c, (ci + 1) * c)
            o_intra, q_dec, incr, decay = local[ci]
            o = o_intra[hd] + _dot_nt(q_dec[:, sl], state_t.astype(BF16))
            state_t = state_t * decay[:, sl] + incr[hd]
            gate = hg_ref[0, rows, sl].astype(F32)
            o_ref[0, rows, sl] = (_rms(o, gout) * (gate * jax.nn.sigmoid(gate))).astype(BF16)
        state_ref[hd] = state_t


def _hgrn_stage(hq, hf, hi, hg, lb, gout):
    span = jnp.asarray(_hgrn_span_matrix(), BF16)
    b, s, _ = hq.shape
    tb = HG_BLOCK
    tok = pl.BlockSpec((1, tb, HG_WIDTH), lambda bi, i: (bi, i, 0))
    return pl.pallas_call(
        _hgrn_kernel,
        grid=(b, s // tb),
        in_specs=[tok, tok, tok, tok, _const_spec(lb.shape), _const_spec(gout.shape),
                  _const_spec(span.shape)],
        out_specs=tok,
        out_shape=jax.ShapeDtypeStruct((b, s, HG_WIDTH), BF16),
        scratch_shapes=[pltpu.VMEM((HG_HEADS, HG_DIM, HG_DIM), F32)],
        compiler_params=pltpu.CompilerParams(
            dimension_semantics=("arbitrary", "arbitrary"), vmem_limit_bytes=VMEM_LIMIT),
        name="hgrn2_recurrence",
    )(hq, hf, hi, hg, lb, gout, span)


def _output_kernel(x_ref, attn_ref, rec_ref, gates_ref, p_ref,
                   wb0_ref, wb1_ref, wout_ref, gffn_ref, wgate_ref, wup_ref, wdown_ref,
                   gpg_ref, wpg_ref, wpp_ref, gpost_ref, o_ref):
    y0 = _dot(attn_ref[...], wb0_ref[...])
    y1 = _dot(rec_ref[...], wb1_ref[...])
    g0 = gates_ref[:, :D_MODEL].astype(F32)
    g1 = gates_ref[:, D_MODEL:].astype(F32)
    merged = (g0 * y0 + g1 * y1).astype(BF16)
    x1 = x_ref[...] + _dot(merged, wout_ref[...])

    h2 = _rms(x1, gffn_ref[...]).astype(BF16)
    x2 = x1
    lo = 0
    for width in FFN_SPLITS:
        a = _dot(h2, wgate_ref[:, lo:lo + width])
        u = _dot(h2, wup_ref[:, lo:lo + width])
        z = (a * jax.nn.sigmoid(a) * u).astype(BF16)
        x2 = x2 + _dot(z, wdown_ref[lo:lo + width, :])
        lo += width

    e = _rms(_dot(p_ref[...].astype(BF16), wpp_ref[...]), gpost_ref[...])
    g = jax.nn.sigmoid(_dot(_rms(x2, gpg_ref[...]).astype(BF16), wpg_ref[...]))
    o_ref[...] = x2 + g * e


def _output_stage(x2d, attn, rec, gates, p2d, wb0, wb1, wout, gffn, wgate, wup, wdown,
                  gpg, wpg, wpp, gpost):
    t = x2d.shape[0]
    tm = TOK_TILE

    def tok(width):
        return pl.BlockSpec((tm, width), lambda i: (i, 0))

    consts = (wb0, wb1, wout, gffn, wgate, wup, wdown, gpg, wpg, wpp, gpost)
    return pl.pallas_call(
        _output_kernel,
        grid=(t // tm,),
        in_specs=[tok(D_MODEL), tok(BRANCH_WIDTH), tok(HG_WIDTH), tok(2 * D_MODEL), tok(PLE_DIM)]
        + [_const_spec(c.shape) for c in consts],
        out_specs=tok(D_MODEL),
        out_shape=jax.ShapeDtypeStruct((t, D_MODEL), F32),
        compiler_params=pltpu.CompilerParams(
            dimension_semantics=("arbitrary",), vmem_limit_bytes=VMEM_LIMIT),
        name="output_stage",
    )(x2d, attn, rec, gates, p2d, *consts)


def _pad_heads(w, heads, width):
    rows = w.shape[0]
    w = w.reshape(rows, heads, width)
    w = jnp.pad(w, ((0, 0), (0, 0), (0, HEAD_PAD - width)))
    return w.reshape(rows, heads * HEAD_PAD)


def _row(v):
    return v.reshape(1, -1).astype(F32)


def _pad_row(v, width):
    return jnp.pad(v.astype(F32), (0, width - v.shape[0])).reshape(1, width)


def kernel(x, p, positions, mix_norm_g, w_in, q_a_norm_g, w_uq, kv_a_norm_g, w_ukv, q_norm_g, k_norm_g, hg_lb_logits, hg_out_norm_g, w_branch, w_out, ffn_norm_g, w_ffn_gate, w_ffn_up, w_ffn_down, ple_gate_norm_g, w_ple_gate, w_ple_proj, ple_post_norm_g):
    b, s, d = x.shape
    depth = w_in.shape[0]
    lower_bounds = jnp.cumsum(jax.nn.softmax(hg_lb_logits.astype(F32), axis=0), axis=0)
    pos2d = positions.reshape(b * s, 1)
    x2d = x.reshape(b * s, d)

    for layer in range(depth):
        wi = w_in[layer]
        c0 = Q_LORA_RANK + KV_LORA_RANK
        c1 = c0 + QK_ROPE_DIM
        rope_tile = jnp.pad(wi[:, c0:c1], ((0, 0), (QK_NOPE_DIM, HEAD_PAD - QK_HEAD_DIM)))
        wlat = jnp.concatenate([wi[:, :c0], rope_tile], axis=1).astype(BF16)
        wh = wi[:, c1:c1 + 4 * HG_WIDTH].astype(BF16)
        wg = wi[:, c1 + 4 * HG_WIDTH:].astype(BF16)
        wuq = _pad_heads(w_uq[layer], MLA_HEADS, QK_HEAD_DIM).astype(BF16)
        wukv = w_ukv[layer].reshape(KV_LORA_RANK, MLA_HEADS, QK_NOPE_DIM + V_HEAD_DIM)
        wk = _pad_heads(wukv[:, :, :QK_NOPE_DIM].reshape(KV_LORA_RANK, -1), MLA_HEADS, QK_NOPE_DIM)
        wv = _pad_heads(wukv[:, :, QK_NOPE_DIM:].reshape(KV_LORA_RANK, -1), MLA_HEADS, V_HEAD_DIM)
        wkv = jnp.concatenate([wk, wv], axis=1).astype(BF16)

        q, k, v, hq, hf, hi, hg, gates = _input_stage(
            x2d, pos2d, b, s, _row(mix_norm_g[layer]), wlat, _row(q_a_norm_g[layer]), wuq,
            _row(kv_a_norm_g[layer]), wkv, _pad_row(q_norm_g[layer], HEAD_PAD),
            _pad_row(k_norm_g[layer], HEAD_PAD), wh, wg)

        attn = _attention_stage(q, k, v)
        rec = _hgrn_stage(hq.reshape(b, s, HG_WIDTH), hf.reshape(b, s, HG_WIDTH),
                          hi.reshape(b, s, HG_WIDTH), hg.reshape(b, s, HG_WIDTH),
                          _row(lower_bounds[layer]), _row(hg_out_norm_g[layer]))

        x2d = _output_stage(
            x2d, attn.reshape(b * s, BRANCH_WIDTH), rec.reshape(b * s, HG_WIDTH), gates,
            p[layer].reshape(b * s, PLE_DIM),
            w_branch[layer, 0].astype(BF16), w_branch[layer, 1].astype(BF16),
            w_out[layer].astype(BF16), _row(ffn_norm_g[layer]),
            w_ffn_gate[layer].astype(BF16), w_ffn_up[layer].astype(BF16),
            w_ffn_down[layer].astype(BF16), _row(ple_gate_norm_g[layer]),
            w_ple_gate[layer].astype(BF16), w_ple_proj[layer].astype(BF16),
            _row(ple_post_norm_g[layer]))
    return x2d.reshape(b, s, d)
```

```python
import functools
import math

import numpy as np

import jax
import jax.numpy as jnp
from jax import lax
from jax.experimental import pallas as pl
from jax.experimental.pallas import tpu as pltpu

D_MODEL = 1024
MLA_HEADS = 8
QK_NOPE_DIM = 64
QK_ROPE_DIM = 32
ROPE_HALF = QK_ROPE_DIM // 2
QK_HEAD_DIM = QK_NOPE_DIM + QK_ROPE_DIM
V_HEAD_DIM = 64
Q_LORA_RANK = 384
KV_LORA_RANK = 256
ROPE_BASE = 10000.0
HG_HEADS = 4
HG_DIM = 128
HG_WIDTH = HG_HEADS * HG_DIM
BRANCH_WIDTH = MLA_HEADS * V_HEAD_DIM
FFN_HIDDEN = 2816
PLE_DIM = 256
EPS = 1e-6

LANES = 128
HEAD_PAD = LANES
DENOM_LANE = V_HEAD_DIM

VMEM_LIMIT = 56 * 1024 * 1024

TOK_TILE = 512
ATT_TILE = 512
ATT_HEADS = 4
HG_BLOCK = 256
HG_CHUNK = 64
HG_LEVELS = tuple(HG_CHUNK >> (i + 1) for i in range(HG_CHUNK.bit_length() - 1))
FFN_SPLITS = (1024, 1024, 768)

BF16 = jnp.bfloat16
F32 = jnp.float32


def _const_spec(shape):
    nd = len(shape)
    return pl.BlockSpec(shape, lambda *_: (0,) * nd, pipeline_mode=pl.Buffered(1))


def _rms(v, gain, width=None):
    n = v.shape[-1] if width is None else width
    ms = jnp.sum(v * v, axis=-1, keepdims=True) * (1.0 / n)
    return v * lax.rsqrt(ms + EPS) * gain


def _dot(a, b):
    return jnp.dot(a, b, preferred_element_type=F32)


def _dot_nt(a, b):
    return lax.dot_general(a, b, (((1,), (1,)), ((), ())), preferred_element_type=F32)


def _input_kernel(x_ref, pos_ref, gmix_ref, wlat_ref, gqa_ref, wuq_ref, gkva_ref, wkv_ref,
                  gq_ref, gk_ref, wh_ref, wg_ref,
                  q_ref, k_ref, v_ref, hq_ref, hf_ref, hi_ref, hg_ref, gates_ref):
    x = x_ref[...]
    h = _rms(x, gmix_ref[...]).astype(BF16)

    lat = _dot(h, wlat_ref[...])
    cq = _rms(lat[:, :Q_LORA_RANK], gqa_ref[...]).astype(BF16)
    ckv = _rms(lat[:, Q_LORA_RANK:Q_LORA_RANK + KV_LORA_RANK], gkva_ref[...]).astype(BF16)
    k_rope = lat[:, Q_LORA_RANK + KV_LORA_RANK:]

    lane = lax.broadcasted_iota(jnp.int32, (1, LANES), 1)
    in_x1 = (lane >= QK_NOPE_DIM) & (lane < QK_NOPE_DIM + ROPE_HALF)
    in_x2 = (lane >= QK_NOPE_DIM + ROPE_HALF) & (lane < QK_HEAD_DIM)
    fidx = jnp.where(in_x2, lane - (QK_NOPE_DIM + ROPE_HALF),
                     jnp.where(in_x1, lane - QK_NOPE_DIM, 0)).astype(F32)
    inv_freq = jnp.exp(fidx * (-math.log(ROPE_BASE) * 2.0 / QK_ROPE_DIM))
    ang = pos_ref[...].astype(F32) * inv_freq
    cos = jnp.cos(ang)
    sin = jnp.sin(ang)
    c_tab = jnp.where(lane < QK_NOPE_DIM, 1.0, jnp.where(in_x1 | in_x2, cos, 0.0))
    s_fwd = jnp.where(in_x2, sin, 0.0)
    s_bwd = jnp.where(in_x1, -sin, 0.0)

    def qk_norm_rope(t, gain):
        t = _rms(t, gain, width=QK_HEAD_DIM)
        return (t * c_tab + pltpu.roll(t, ROPE_HALF, 1) * s_fwd
                + pltpu.roll(t, LANES - ROPE_HALF, 1) * s_bwd)

    q_all = _dot(cq, wuq_ref[...])
    kv_all = _dot(ckv, wkv_ref[...])
    scale = QK_HEAD_DIM ** -0.5 * math.log2(math.e)
    one_lane = (lane == DENOM_LANE).astype(F32)
    gq = gq_ref[...]
    gk = gk_ref[...]
    for hd in range(MLA_HEADS):
        sl = slice(hd * HEAD_PAD, (hd + 1) * HEAD_PAD)
        q_ref[0, hd] = (qk_norm_rope(q_all[:, sl], gq) * scale).astype(BF16)
        k_ref[0, hd] = qk_norm_rope(kv_all[:, sl] + k_rope, gk).astype(BF16)
        vsl = slice(MLA_HEADS * HEAD_PAD + hd * HEAD_PAD, MLA_HEADS * HEAD_PAD + (hd + 1) * HEAD_PAD)
        v_ref[0, hd] = (kv_all[:, vsl] + one_lane).astype(BF16)

    hh = _dot(h, wh_ref[...])
    hq_ref[...] = hh[:, :HG_WIDTH].astype(BF16)
    hf_ref[...] = hh[:, HG_WIDTH:2 * HG_WIDTH]
    hi_ref[...] = hh[:, 2 * HG_WIDTH:3 * HG_WIDTH].astype(BF16)
    hg_ref[...] = hh[:, 3 * HG_WIDTH:].astype(BF16)

    gates_ref[...] = jax.nn.sigmoid(_dot(h, wg_ref[...])).astype(BF16)


def _input_stage(x2d, pos2d, b, s, gmix, wlat, gqa, wuq, gkva, wkv, gq, gk, wh, wg):
    t = x2d.shape[0]
    tm = TOK_TILE
    nt = s // tm

    def tok(width):
        return pl.BlockSpec((tm, width), lambda i: (i, 0))

    def head_spec():
        return pl.BlockSpec((1, MLA_HEADS, tm, HEAD_PAD), lambda i: (i // nt, 0, i % nt, 0))

    head_shape = jax.ShapeDtypeStruct((b, MLA_HEADS, s, HEAD_PAD), BF16)
    consts = (gmix, wlat, gqa, wuq, gkva, wkv, gq, gk, wh, wg)
    return pl.pallas_call(
        _input_kernel,
        grid=(t // tm,),
        in_specs=[tok(D_MODEL), tok(1)] + [_const_spec(c.shape) for c in consts],
        out_specs=[head_spec(), head_spec(), head_spec(),
                   tok(HG_WIDTH), tok(HG_WIDTH), tok(HG_WIDTH), tok(HG_WIDTH), tok(2 * D_MODEL)],
        out_shape=[head_shape, head_shape, head_shape,
                   jax.ShapeDtypeStruct((t, HG_WIDTH), BF16),
                   jax.ShapeDtypeStruct((t, HG_WIDTH), F32),
                   jax.ShapeDtypeStruct((t, HG_WIDTH), BF16),
                   jax.ShapeDtypeStruct((t, HG_WIDTH), BF16),
                   jax.ShapeDtypeStruct((t, 2 * D_MODEL), BF16)],
        compiler_params=pltpu.CompilerParams(
            dimension_semantics=("arbitrary",), vmem_limit_bytes=VMEM_LIMIT),
        name="input_stage",
    )(x2d, pos2d, *consts)


def _attention_kernel(q_ref, k_ref, v_ref, o_ref, m_ref, acc_ref, sa_ref, sb_ref):
    qi = pl.program_id(2)
    tq = ATT_TILE
    row = lax.broadcasted_iota(jnp.int32, (tq, tq), 0)
    col = lax.broadcasted_iota(jnp.int32, (tq, tq), 1)
    m_ref[...] = jnp.full(m_ref.shape, -jnp.inf, F32)
    acc_ref[...] = jnp.zeros(acc_ref.shape, F32)

    def scores(j, s_ref):
        start = pl.multiple_of(j * tq, tq)
        for hd in range(ATT_HEADS):
            s_ref[hd] = _dot_nt(q_ref[0, hd], k_ref[0, hd, pl.ds(start, tq), :])

    def accumulate(j, s_ref, masked):
        for hd in range(ATT_HEADS):
            accumulate_head(j, hd, s_ref[hd], masked)

    def accumulate_head(j, hd, sc, masked):
        start = pl.multiple_of(j * tq, tq)
        if masked:
            sc = jnp.where(col <= row, sc, -jnp.inf)
        m_old = m_ref[hd]
        m_new = jnp.maximum(m_old, jnp.max(sc, axis=-1, keepdims=True))
        p = jnp.exp2(sc - jnp.concatenate([m_new] * (tq // LANES), axis=1)).astype(BF16)
        acc_ref[hd] = (jnp.exp2(m_old - m_new) * acc_ref[hd]
                       + _dot(p, v_ref[0, hd, pl.ds(start, tq), :]))
        m_ref[hd] = m_new

    scores(0, sa_ref)

    def body(t, carry):
        j = 2 * t
        scores(j + 1, sb_ref)
        accumulate(j, sa_ref, False)
        scores(j + 2, sa_ref)
        accumulate(j + 1, sb_ref, False)
        return carry

    lax.fori_loop(0, qi // 2, body, 0)

    @pl.when(qi % 2 == 0)
    def _():
        accumulate(qi, sa_ref, True)

    @pl.when(qi % 2 == 1)
    def _():
        scores(qi, sb_ref)
        accumulate(qi - 1, sa_ref, False)
        accumulate(qi, sb_ref, True)

    outs = []
    for hd in range(ATT_HEADS):
        acc = acc_ref[hd]
        outs.append(acc[:, :V_HEAD_DIM] / acc[:, DENOM_LANE:DENOM_LANE + 1])
    o_ref[0] = jnp.concatenate(outs, axis=-1).astype(BF16)


def _attention_stage(q, k, v):
    b, nh, s, _ = q.shape
    tq = ATT_TILE
    g = ATT_HEADS
    return pl.pallas_call(
        _attention_kernel,
        grid=(b, nh // g, s // tq),
        in_specs=[pl.BlockSpec((1, g, tq, HEAD_PAD), lambda bi, hp, i: (bi, hp, i, 0)),
                  pl.BlockSpec((1, g, s, HEAD_PAD), lambda bi, hp, i: (bi, hp, 0, 0)),
                  pl.BlockSpec((1, g, s, HEAD_PAD), lambda bi, hp, i: (bi, hp, 0, 0))],
        out_specs=pl.BlockSpec((1, tq, g * V_HEAD_DIM), lambda bi, hp, i: (bi, i, hp)),
        out_shape=jax.ShapeDtypeStruct((b, s, BRANCH_WIDTH), BF16),
        scratch_shapes=[pltpu.VMEM((g, tq, LANES), F32), pltpu.VMEM((g, tq, HEAD_PAD), F32),
                        pltpu.VMEM((g, tq, tq), F32), pltpu.VMEM((g, tq, tq), F32)],
        compiler_params=pltpu.CompilerParams(
            dimension_semantics=("arbitrary", "arbitrary", "arbitrary"),
            vmem_limit_bytes=VMEM_LIMIT),
        name="causal_attention",
    )(q, k, v)


def _hgrn_span_matrix():
    c = HG_CHUNK
    x = np.arange(c)[:, None]
    y = np.arange(c)[None, :]
    mats = [(y <= x)]
    for b in HG_LEVELS:
        r = (x // (2 * b)) * (2 * b) + b
        mats.append((y > np.minimum(x, r)) & (y <= np.maximum(x, r)))
    m = np.concatenate(mats, axis=0).astype(np.float32)
    return np.concatenate([m, m], axis=1)


def _hgrn_chunk_local(q, hf, v_bf, lb, span, right_rows, pair_masks):
    c = HG_CHUNK
    f = lb + (1.0 - lb) * jax.nn.sigmoid(hf)
    lf = jnp.log2(f)
    kk = 1.0 - f
    lf_hi = lf.astype(BF16)
    lf_lo = (lf - lf_hi.astype(F32)).astype(BF16)
    spans = _dot(span, jnp.concatenate([lf_hi, lf_lo], axis=0))
    cum = spans[:c]
    last = cum[c - 1:c, :]
    q_dec = (q * jnp.exp2(cum)).astype(BF16)
    k_dec = (kk * jnp.exp2(last - cum)).astype(BF16)
    q_bf = q.astype(BF16)
    k_bf = kk.astype(BF16)
    zs = [(jnp.where(right_rows[lvl], q, kk)
           * jnp.exp2(spans[(lvl + 1) * c:(lvl + 2) * c])).astype(BF16)
          for lvl in range(len(HG_LEVELS))]

    o_intra, incr = [], []
    for hd in range(HG_HEADS):
        sl = slice(hd * HG_DIM, (hd + 1) * HG_DIM)
        att = jnp.where(pair_masks[0], _dot_nt(q_bf[:, sl], k_bf[:, sl]), 0.0)
        for lvl in range(len(HG_LEVELS)):
            z = zs[lvl][:, sl]
            att = jnp.where(pair_masks[lvl + 1], _dot_nt(z, z), att)
        v_h = v_bf[:, sl]
        o_intra.append(_dot(att.astype(BF16), v_h))
        incr.append(_dot(v_h.astype(F32).T.astype(BF16), k_dec[:, sl]))
    return o_intra, q_dec, incr, jnp.exp2(last)


def _hgrn_kernel(hq_ref, hf_ref, hi_ref, hg_ref, lb_ref, gout_ref, span_ref, o_ref, state_ref):
    @pl.when(pl.program_id(1) == 0)
    def _():
        state_ref[...] = jnp.zeros(state_ref.shape, F32)

    c = HG_CHUNK
    t_i = lax.broadcasted_iota(jnp.int32, (c, c), 0)
    s_i = lax.broadcasted_iota(jnp.int32, (c, c), 1)
    row = lax.broadcasted_iota(jnp.int32, (c, HG_WIDTH), 0)
    pair_masks = [t_i == s_i] + [((t_i // b) ^ (s_i // b) == 1) & (s_i < t_i) for b in HG_LEVELS]
    right_rows = [(row // b) % 2 == 1 for b in HG_LEVELS]
    span = span_ref[...]
    lb = lb_ref[...]
    n_chunks = HG_BLOCK // c

    local = []
    for ci in range(n_chunks):
        rows = slice(ci * c, (ci + 1) * c)
        local.append(_hgrn_chunk_local(hq_ref[0, rows, :].astype(F32), hf_ref[0, rows, :],
                                       hi_ref[0, rows, :], lb, span, right_rows, pair_masks))

    gout = gout_ref[...]
    for hd in range(HG_HEADS):
        sl = slice(hd * HG_DIM, (hd + 1) * HG_DIM)
        state_t = state_ref[hd]
        for ci in range(n_chunks):
            rows = slice(ci * c, (ci + 1) * c)
            o_intra, q_dec, incr, decay = local[ci]
            o = o_intra[hd] + _dot_nt(q_dec[:, sl], state_t.astype(BF16))
            state_t = state_t * decay[:, sl] + incr[hd]
            gate = hg_ref[0, rows, sl].astype(F32)
            o_ref[0, rows, sl] = (_rms(o, gout) * (gate * jax.nn.sigmoid(gate))).astype(BF16)
        state_ref[hd] = state_t


def _hgrn_stage(hq, hf, hi, hg, lb, gout):
    span = jnp.asarray(_hgrn_span_matrix(), BF16)
    b, s, _ = hq.shape
    tb = HG_BLOCK
    tok = pl.BlockSpec((1, tb, HG_WIDTH), lambda bi, i: (bi, i, 0))
    return pl.pallas_call(
        _hgrn_kernel,
        grid=(b, s // tb),
        in_specs=[tok, tok, tok, tok, _const_spec(lb.shape), _const_spec(gout.shape),
                  _const_spec(span.shape)],
        out_specs=tok,
        out_shape=jax.ShapeDtypeStruct((b, s, HG_WIDTH), BF16),
        scratch_shapes=[pltpu.VMEM((HG_HEADS, HG_DIM, HG_DIM), F32)],
        compiler_params=pltpu.CompilerParams(
            dimension_semantics=("arbitrary", "arbitrary"), vmem_limit_bytes=VMEM_LIMIT),
        name="hgrn2_recurrence",
    )(hq, hf, hi, hg, lb, gout, span)


def _output_kernel(x_ref, attn_ref, rec_ref, gates_ref, p_ref,
                   wb0_ref, wb1_ref, wout_ref, gffn_ref, wgate_ref, wup_ref, wdown_ref,
                   gpg_ref, wpg_ref, wpp_ref, gpost_ref, o_ref):
    y0 = _dot(attn_ref[...], wb0_ref[...])
    y1 = _dot(rec_ref[...], wb1_ref[...])
    g0 = gates_ref[:, :D_MODEL].astype(F32)
    g1 = gates_ref[:, D_MODEL:].astype(F32)
    merged = (g0 * y0 + g1 * y1).astype(BF16)
    x1 = x_ref[...] + _dot(merged, wout_ref[...])

    h2 = _rms(x1, gffn_ref[...]).astype(BF16)
    x2 = x1
    lo = 0
    for width in FFN_SPLITS:
        a = _dot(h2, wgate_ref[:, lo:lo + width])
        u = _dot(h2, wup_ref[:, lo:lo + width])
        z = (a * jax.nn.sigmoid(a) * u).astype(BF16)
        x2 = x2 + _dot(z, wdown_ref[lo:lo + width, :])
        lo += width

    e = _rms(_dot(p_ref[...].astype(BF16), wpp_ref[...]), gpost_ref[...])
    g = jax.nn.sigmoid(_dot(_rms(x2, gpg_ref[...]).astype(BF16), wpg_ref[...]))
    o_ref[...] = x2 + g * e


def _output_stage(x2d, attn, rec, gates, p2d, wb0, wb1, wout, gffn, wgate, wup, wdown,
                  gpg, wpg, wpp, gpost):
    t = x2d.shape[0]
    tm = TOK_TILE

    def tok(width):
        return pl.BlockSpec((tm, width), lambda i: (i, 0))

    consts = (wb0, wb1, wout, gffn, wgate, wup, wdown, gpg, wpg, wpp, gpost)
    return pl.pallas_call(
        _output_kernel,
        grid=(t // tm,),
        in_specs=[tok(D_MODEL), tok(BRANCH_WIDTH), tok(HG_WIDTH), tok(2 * D_MODEL), tok(PLE_DIM)]
        + [_const_spec(c.shape) for c in consts],
        out_specs=tok(D_MODEL),
        out_shape=jax.ShapeDtypeStruct((t, D_MODEL), F32),
        compiler_params=pltpu.CompilerParams(
            dimension_semantics=("arbitrary",), vmem_limit_bytes=VMEM_LIMIT),
        name="output_stage",
    )(x2d, attn, rec, gates, p2d, *consts)


def _pad_heads(w, heads, width):
    rows = w.shape[0]
    w = w.reshape(rows, heads, width)
    w = jnp.pad(w, ((0, 0), (0, 0), (0, HEAD_PAD - width)))
    return w.reshape(rows, heads * HEAD_PAD)


def _row(v):
    return v.reshape(1, -1).astype(F32)


def _pad_row(v, width):
    return jnp.pad(v.astype(F32), (0, width - v.shape[0])).reshape(1, width)


def kernel(x, p, positions, mix_norm_g, w_in, q_a_norm_g, w_uq, kv_a_norm_g, w_ukv, q_norm_g, k_norm_g, hg_lb_logits, hg_out_norm_g, w_branch, w_out, ffn_norm_g, w_ffn_gate, w_ffn_up, w_ffn_down, ple_gate_norm_g, w_ple_gate, w_ple_proj, ple_post_norm_g):
    b, s, d = x.shape
    depth = w_in.shape[0]
    lower_bounds = jnp.cumsum(jax.nn.softmax(hg_lb_logits.astype(F32), axis=0), axis=0)
    pos2d = positions.reshape(b * s, 1)
    x2d = x.reshape(b * s, d)

    for layer in range(depth):
        wi = w_in[layer]
        c0 = Q_LORA_RANK + KV_LORA_RANK
        c1 = c0 + QK_ROPE_DIM
        rope_tile = jnp.pad(wi[:, c0:c1], ((0, 0), (QK_NOPE_DIM, HEAD_PAD - QK_HEAD_DIM)))
        wlat = jnp.concatenate([wi[:, :c0], rope_tile], axis=1).astype(BF16)
        wh = wi[:, c1:c1 + 4 * HG_WIDTH].astype(BF16)
        wg = wi[:, c1 + 4 * HG_WIDTH:].astype(BF16)
        wuq = _pad_heads(w_uq[layer], MLA_HEADS, QK_HEAD_DIM).astype(BF16)
        wukv = w_ukv[layer].reshape(KV_LORA_RANK, MLA_HEADS, QK_NOPE_DIM + V_HEAD_DIM)
        wk = _pad_heads(wukv[:, :, :QK_NOPE_DIM].reshape(KV_LORA_RANK, -1), MLA_HEADS, QK_NOPE_DIM)
        wv = _pad_heads(wukv[:, :, QK_NOPE_DIM:].reshape(KV_LORA_RANK, -1), MLA_HEADS, V_HEAD_DIM)
        wkv = jnp.concatenate([wk, wv], axis=1).astype(BF16)

        q, k, v, hq, hf, hi, hg, gates = _input_stage(
            x2d, pos2d, b, s, _row(mix_norm_g[layer]), wlat, _row(q_a_norm_g[layer]), wuq,
            _row(kv_a_norm_g[layer]), wkv, _pad_row(q_norm_g[layer], HEAD_PAD),
            _pad_row(k_norm_g[layer], HEAD_PAD), wh, wg)

        attn = _attention_stage(q, k, v)
        rec = _hgrn_stage(hq.reshape(b, s, HG_WIDTH), hf.reshape(b, s, HG_WIDTH),
                          hi.reshape(b, s, HG_WIDTH), hg.reshape(b, s, HG_WIDTH),
                          _row(lower_bounds[layer]), _row(hg_out_norm_g[layer]))

        x2d = _output_stage(
            x2d, attn.reshape(b * s, BRANCH_WIDTH), rec.reshape(b * s, HG_WIDTH), gates,
            p[layer].reshape(b * s, PLE_DIM),
            w_branch[layer, 0].astype(BF16), w_branch[layer, 1].astype(BF16),
            w_out[layer].astype(BF16), _row(ffn_norm_g[layer]),
            w_ffn_gate[layer].astype(BF16), w_ffn_up[layer].astype(BF16),
            w_ffn_down[layer].astype(BF16), _row(ple_gate_norm_g[layer]),
            w_ple_gate[layer].astype(BF16), w_ple_proj[layer].astype(BF16),
            _row(ple_post_norm_g[layer]))
    return x2d.reshape(b, s, d)
```

```python
import functools
import math

import numpy as np

import jax
import jax.numpy as jnp
from jax import lax
from jax.experimental import pallas as pl
from jax.experimental.pallas import tpu as pltpu

D_MODEL = 1024
MLA_HEADS = 8
QK_NOPE_DIM = 64
QK_ROPE_DIM = 32
ROPE_HALF = QK_ROPE_DIM // 2
QK_HEAD_DIM = QK_NOPE_DIM + QK_ROPE_DIM
V_HEAD_DIM = 64
Q_LORA_RANK = 384
KV_LORA_RANK = 256
ROPE_BASE = 10000.0
HG_HEADS = 4
HG_DIM = 128
HG_WIDTH = HG_HEADS * HG_DIM
BRANCH_WIDTH = MLA_HEADS * V_HEAD_DIM
FFN_HIDDEN = 2816
PLE_DIM = 256
EPS = 1e-6

LANES = 128
HEAD_PAD = LANES
HEAD_HALF_NOPE = QK_NOPE_DIM // 2
W_LAT = Q_LORA_RANK + KV_LORA_RANK + HEAD_PAD
DENOM_LANE = V_HEAD_DIM

VMEM_LIMIT = 56 * 1024 * 1024

TOK_TILE = 512
ATT_TILE = 512
ATT_HEADS = 4
HG_BLOCK = 256
HG_CHUNK = 64
HG_LEVELS = tuple(HG_CHUNK >> (i + 1) for i in range(HG_CHUNK.bit_length() - 1))
FFN_SPLITS = (1024, 1024, 768)

BF16 = jnp.bfloat16
F32 = jnp.float32


def _const_spec(shape):
    nd = len(shape)
    return pl.BlockSpec(shape, lambda *_: (0,) * nd, pipeline_mode=pl.Buffered(1))


def _rms(v, gain, width=None):
    n = v.shape[-1] if width is None else width
    ms = jnp.sum(v * v, axis=-1, keepdims=True) * (1.0 / n)
    return v * lax.rsqrt(ms + EPS) * gain


def _dot(a, b):
    return jnp.dot(a, b, preferred_element_type=F32)


def _dot_nt(a, b):
    return lax.dot_general(a, b, (((1,), (1,)), ((), ())), preferred_element_type=F32)


def _rope_tables(pos_row):
    tm = pos_row.shape[1]
    fidx = lax.broadcasted_iota(jnp.int32, (ROPE_HALF, 1), 0).astype(F32)
    inv_freq = jnp.exp(fidx * (-math.log(ROPE_BASE) * 2.0 / QK_ROPE_DIM))
    ang = pos_row.astype(F32) * inv_freq
    cos = jnp.cos(ang)
    sin = jnp.sin(ang)
    ones = jnp.ones((HEAD_HALF_NOPE, tm), F32)
    zeros_n = jnp.zeros((HEAD_HALF_NOPE, tm), F32)
    zeros_p = jnp.zeros((ROPE_HALF, tm), F32)
    cos_t = jnp.concatenate([ones, cos, zeros_p, ones, cos, zeros_p], axis=0)
    sin_t = jnp.concatenate([zeros_n, -sin, zeros_p, zeros_n, sin, zeros_p], axis=0)
    return cos_t.T, sin_t.T


def _input_kernel(x_ref, pos_ref, gmix_ref, wall_ref, gqa_ref, wuq_ref, gkva_ref, wkv_ref,
                  gq_ref, gk_ref,
                  q_ref, k_ref, v_ref, hq_ref, hf_ref, hi_ref, hg_ref, gates_ref):
    x = x_ref[...]
    h = _rms(x, gmix_ref[...]).astype(BF16)

    lat = _dot(h, wall_ref[:, :W_LAT])
    cq = _rms(lat[:, :Q_LORA_RANK], gqa_ref[...]).astype(BF16)
    ckv = _rms(lat[:, Q_LORA_RANK:Q_LORA_RANK + KV_LORA_RANK], gkva_ref[...]).astype(BF16)
    k_rope = lat[:, Q_LORA_RANK + KV_LORA_RANK:]

    cos_tab, sin_tab = _rope_tables(pos_ref[0])
    gq = gq_ref[...]
    gk = gk_ref[...]
    q_cos = cos_tab * gq
    q_sin = sin_tab * pltpu.roll(gq, LANES // 2, 1)
    kr = k_rope * gk
    kr = kr * cos_tab + pltpu.roll(kr, LANES // 2, 1) * sin_tab
    kr_ss = jnp.sum(k_rope * k_rope, axis=-1, keepdims=True)

    q_all = _dot(cq, wuq_ref[...])
    kv_all = _dot(ckv, wkv_ref[...])
    lane = lax.broadcasted_iota(jnp.int32, (1, LANES), 1)
    one_lane = (lane == DENOM_LANE).astype(F32)
    inv_width = 1.0 / QK_HEAD_DIM
    for hd in range(MLA_HEADS):
        sl = slice(hd * HEAD_PAD, (hd + 1) * HEAD_PAD)
        t = q_all[:, sl]
        r = lax.rsqrt(jnp.sum(t * t, axis=-1, keepdims=True) * inv_width + EPS)
        q_ref[0, hd] = ((t * q_cos + pltpu.roll(t, LANES // 2, 1) * q_sin) * r).astype(BF16)
        t = kv_all[:, sl]
        r = lax.rsqrt((jnp.sum(t * t, axis=-1, keepdims=True) + kr_ss) * inv_width + EPS)
        k_ref[0, hd] = ((t * gk + kr) * r).astype(BF16)
        vsl = slice(MLA_HEADS * HEAD_PAD + hd * HEAD_PAD, MLA_HEADS * HEAD_PAD + (hd + 1) * HEAD_PAD)
        v_ref[0, hd] = (kv_all[:, vsl] + one_lane).astype(BF16)

    hh = _dot(h, wall_ref[:, W_LAT:W_LAT + 4 * HG_WIDTH])
    hq_ref[...] = hh[:, :HG_WIDTH].astype(BF16)
    hf_ref[...] = hh[:, HG_WIDTH:2 * HG_WIDTH]
    hi_ref[...] = hh[:, 2 * HG_WIDTH:3 * HG_WIDTH].astype(BF16)
    hg_ref[...] = hh[:, 3 * HG_WIDTH:].astype(BF16)

    gates_ref[...] = jax.nn.sigmoid(_dot(h, wall_ref[:, W_LAT + 4 * HG_WIDTH:])).astype(BF16)


def _input_stage(x2d, pos3d, b, s, gmix, wall, gqa, wuq, gkva, wkv, gq, gk):
    t = x2d.shape[0]
    tm = TOK_TILE
    nt = s // tm

    def tok(width):
        return pl.BlockSpec((tm, width), lambda i: (i, 0))

    def head_spec():
        return pl.BlockSpec((1, MLA_HEADS, tm, HEAD_PAD), lambda i: (i // nt, 0, i % nt, 0))

    head_shape = jax.ShapeDtypeStruct((b, MLA_HEADS, s, HEAD_PAD), BF16)
    consts = (gmix, wall, gqa, wuq, gkva, wkv, gq, gk)
    return pl.pallas_call(
        _input_kernel,
        grid=(t // tm,),
        in_specs=[tok(D_MODEL), pl.BlockSpec((1, 1, tm), lambda i: (i, 0, 0))]
        + [_const_spec(c.shape) for c in consts],
        out_specs=[head_spec(), head_spec(), head_spec(),
                   tok(HG_WIDTH), tok(HG_WIDTH), tok(HG_WIDTH), tok(HG_WIDTH), tok(2 * D_MODEL)],
        out_shape=[head_shape, head_shape, head_shape,
                   jax.ShapeDtypeStruct((t, HG_WIDTH), BF16),
                   jax.ShapeDtypeStruct((t, HG_WIDTH), F32),
                   jax.ShapeDtypeStruct((t, HG_WIDTH), BF16),
                   jax.ShapeDtypeStruct((t, HG_WIDTH), BF16),
                   jax.ShapeDtypeStruct((t, 2 * D_MODEL), BF16)],
        compiler_params=pltpu.CompilerParams(
            dimension_semantics=("arbitrary",), vmem_limit_bytes=VMEM_LIMIT),
        name="input_stage",
    )(x2d, pos3d, *consts)


def _attention_kernel(q_ref, k_ref, v_ref, o_ref, m_ref, acc_ref, sa_ref, sb_ref):
    qi = pl.program_id(2)
    tq = ATT_TILE
    row = lax.broadcasted_iota(jnp.int32, (tq, tq), 0)
    col = lax.broadcasted_iota(jnp.int32, (tq, tq), 1)
    m_ref[...] = jnp.full(m_ref.shape, -jnp.inf, F32)
    acc_ref[...] = jnp.zeros(acc_ref.shape, F32)

    def scores(j, s_ref):
        start = pl.multiple_of(j * tq, tq)
        for hd in range(ATT_HEADS):
            s_ref[hd] = _dot_nt(q_ref[0, hd], k_ref[0, hd, pl.ds(start, tq), :])

    def accumulate(j, s_ref, masked):
        for hd in range(ATT_HEADS):
            accumulate_head(j, hd, s_ref[hd], masked)

    def accumulate_head(j, hd, sc, masked):
        start = pl.multiple_of(j * tq, tq)
        if masked:
            sc = jnp.where(col <= row, sc, -jnp.inf)
        m_old = m_ref[hd]
        m_new = jnp.maximum(m_old, jnp.max(sc, axis=-1, keepdims=True))
        p = jnp.exp2(sc - jnp.concatenate([m_new] * (tq // LANES), axis=1)).astype(BF16)
        acc_ref[hd] = (jnp.exp2(m_old - m_new) * acc_ref[hd]
                       + _dot(p, v_ref[0, hd, pl.ds(start, tq), :]))
        m_ref[hd] = m_new

    scores(0, sa_ref)

    def body(t, carry):
        j = 2 * t
        scores(j + 1, sb_ref)
        accumulate(j, sa_ref, False)
        scores(j + 2, sa_ref)
        accumulate(j + 1, sb_ref, False)
        return carry

    lax.fori_loop(0, qi // 2, body, 0)

    @pl.when(qi % 2 == 0)
    def _():
        accumulate(qi, sa_ref, True)

    @pl.when(qi % 2 == 1)
    def _():
        scores(qi, sb_ref)
        accumulate(qi - 1, sa_ref, False)
        accumulate(qi, sb_ref, True)

    outs = []
    for hd in range(ATT_HEADS):
        acc = acc_ref[hd]
        outs.append(acc[:, :V_HEAD_DIM] / acc[:, DENOM_LANE:DENOM_LANE + 1])
    o_ref[0] = jnp.concatenate(outs, axis=-1).astype(BF16)


def _attention_stage(q, k, v):
    b, nh, s, _ = q.shape
    tq = ATT_TILE
    g = ATT_HEADS
    return pl.pallas_call(
        _attention_kernel,
        grid=(b, nh // g, s // tq),
        in_specs=[pl.BlockSpec((1, g, tq, HEAD_PAD), lambda bi, hp, i: (bi, hp, i, 0)),
                  pl.BlockSpec((1, g, s, HEAD_PAD), lambda bi, hp, i: (bi, hp, 0, 0)),
                  pl.BlockSpec((1, g, s, HEAD_PAD), lambda bi, hp, i: (bi, hp, 0, 0))],
        out_specs=pl.BlockSpec((1, tq, g * V_HEAD_DIM), lambda bi, hp, i: (bi, i, hp)),
        out_shape=jax.ShapeDtypeStruct((b, s, BRANCH_WIDTH), BF16),
        scratch_shapes=[pltpu.VMEM((g, tq, LANES), F32), pltpu.VMEM((g, tq, HEAD_PAD), F32),
                        pltpu.VMEM((g, tq, tq), F32), pltpu.VMEM((g, tq, tq), F32)],
        compiler_params=pltpu.CompilerParams(
            dimension_semantics=("arbitrary", "arbitrary", "arbitrary"),
            vmem_limit_bytes=VMEM_LIMIT),
        name="causal_attention",
    )(q, k, v)


def _hgrn_span_matrix():
    c = HG_CHUNK
    x = np.arange(c)[:, None]
    y = np.arange(c)[None, :]
    mats = [(y <= x)]
    for b in HG_LEVELS:
        r = (x // (2 * b)) * (2 * b) + b
        mats.append((y > np.minimum(x, r)) & (y <= np.maximum(x, r)))
    m = np.concatenate(mats, axis=0).astype(np.float32)
    return np.concatenate([m, m], axis=1)


def _hgrn_chunk_local(q, hf, v_bf, lb, span, right_rows, pair_masks):
    c = HG_CHUNK
    f = lb + (1.0 - lb) * jax.nn.sigmoid(hf)
    lf = jnp.log2(f)
    kk = 1.0 - f
    lf_hi = lf.astype(BF16)
    lf_lo = (lf - lf_hi.astype(F32)).astype(BF16)
    spans = _dot(span, jnp.concatenate([lf_hi, lf_lo], axis=0))
    cum = spans[:c]
    last = cum[c - 1:c, :]
    q_dec = (q * jnp.exp2(cum)).astype(BF16)
    k_dec = (kk * jnp.exp2(last - cum)).astype(BF16)
    q_bf = q.astype(BF16)
    k_bf = kk.astype(BF16)
    zs = [(jnp.where(right_rows[lvl], q, kk)
           * jnp.exp2(spans[(lvl + 1) * c:(lvl + 2) * c])).astype(BF16)
          for lvl in range(len(HG_LEVELS))]

    o_intra, incr = [], []
    for hd in range(HG_HEADS):
        sl = slice(hd * HG_DIM, (hd + 1) * HG_DIM)
        att = jnp.where(pair_masks[0], _dot_nt(q_bf[:, sl], k_bf[:, sl]), 0.0)
        for lvl in range(len(HG_LEVELS)):
            z = zs[lvl][:, sl]
            att = jnp.where(pair_masks[lvl + 1], _dot_nt(z, z), att)
        v_h = v_bf[:, sl]
        o_intra.append(_dot(att.astype(BF16), v_h))
        incr.append(_dot(v_h.astype(F32).T.astype(BF16), k_dec[:, sl]))
    return o_intra, q_dec, incr, jnp.exp2(last)


def _hgrn_kernel(hq_ref, hf_ref, hi_ref, hg_ref, lb_ref, gout_ref, span_ref, o_ref, state_ref):
    @pl.when(pl.program_id(1) == 0)
    def _():
        state_ref[...] = jnp.zeros(state_ref.shape, F32)

    c = HG_CHUNK
    t_i = lax.broadcasted_iota(jnp.int32, (c, c), 0)
    s_i = lax.broadcasted_iota(jnp.int32, (c, c), 1)
    row = lax.broadcasted_iota(jnp.int32, (c, HG_WIDTH), 0)
    pair_masks = [t_i == s_i] + [((t_i // b) ^ (s_i // b) == 1) & (s_i < t_i) for b in HG_LEVELS]
    right_rows = [(row // b) % 2 == 1 for b in HG_LEVELS]
    span = span_ref[...]
    lb = lb_ref[...]
    n_chunks = HG_BLOCK // c

    local = []
    for ci in range(n_chunks):
        rows = slice(ci * c, (ci + 1) * c)
        local.append(_hgrn_chunk_local(hq_ref[0, rows, :].astype(F32), hf_ref[0, rows, :],
                                       hi_ref[0, rows, :], lb, span, right_rows, pair_masks))

    gout = gout_ref[...]
    for hd in range(HG_HEADS):
        sl = slice(hd * HG_DIM, (hd + 1) * HG_DIM)
        state_t = state_ref[hd]
        for ci in range(n_chunks):
            rows = slice(ci * c, (ci + 1) * c)
            o_intra, q_dec, incr, decay = local[ci]
            o = o_intra[hd] + _dot_nt(q_dec[:, sl], state_t.astype(BF16))
            state_t = state_t * decay[:, sl] + incr[hd]
            gate = hg_ref[0, rows, sl].astype(F32)
            o_ref[0, rows, sl] = (_rms(o, gout) * (gate * jax.nn.sigmoid(gate))).astype(BF16)
        state_ref[hd] = state_t


def _hgrn_stage(hq, hf, hi, hg, lb, gout):
    span = jnp.asarray(_hgrn_span_matrix(), BF16)
    b, s, _ = hq.shape
    tb = HG_BLOCK
    tok = pl.BlockSpec((1, tb, HG_WIDTH), lambda bi, i: (bi, i, 0))
    return pl.pallas_call(
        _hgrn_kernel,
        grid=(b, s // tb),
        in_specs=[tok, tok, tok, tok, _const_spec(lb.shape), _const_spec(gout.shape),
                  _const_spec(span.shape)],
        out_specs=tok,
        out_shape=jax.ShapeDtypeStruct((b, s, HG_WIDTH), BF16),
        scratch_shapes=[pltpu.VMEM((HG_HEADS, HG_DIM, HG_DIM), F32)],
        compiler_params=pltpu.CompilerParams(
            dimension_semantics=("arbitrary", "arbitrary"), vmem_limit_bytes=VMEM_LIMIT),
        name="hgrn2_recurrence",
    )(hq, hf, hi, hg, lb, gout, span)


def _output_kernel(x_ref, attn_ref, rec_ref, gates_ref, p_ref,
                   wb0_ref, wb1_ref, wout_ref, gffn_ref, wgate_ref, wup_ref, wdown_ref,
                   gpg_ref, wpg_ref, wpp_ref, gpost_ref, o_ref):
    y0 = _dot(attn_ref[...], wb0_ref[...])
    y1 = _dot(rec_ref[...], wb1_ref[...])
    g0 = gates_ref[:, :D_MODEL].astype(F32)
    g1 = gates_ref[:, D_MODEL:].astype(F32)
    merged = (g0 * y0 + g1 * y1).astype(BF16)
    x1 = x_ref[...] + _dot(merged, wout_ref[...])

    h2 = _rms(x1, gffn_ref[...]).astype(BF16)
    x2 = x1
    lo = 0
    for width in FFN_SPLITS:
        a = _dot(h2, wgate_ref[:, lo:lo + width])
        u = _dot(h2, wup_ref[:, lo:lo + width])
        z = (a * jax.nn.sigmoid(a) * u).astype(BF16)
        x2 = x2 + _dot(z, wdown_ref[lo:lo + width, :])
        lo += width

    e = _rms(_dot(p_ref[...].astype(BF16), wpp_ref[...]), gpost_ref[...])
    g = jax.nn.sigmoid(_dot(_rms(x2, gpg_ref[...]).astype(BF16), wpg_ref[...]))
    o_ref[...] = x2 + g * e


def _output_stage(x2d, attn, rec, gates, p2d, wb0, wb1, wout, gffn, wgate, wup, wdown,
                  gpg, wpg, wpp, gpost):
    t = x2d.shape[0]
    tm = TOK_TILE

    def tok(width):
        return pl.BlockSpec((tm, width), lambda i: (i, 0))

    consts = (wb0, wb1, wout, gffn, wgate, wup, wdown, gpg, wpg, wpp, gpost)
    return pl.pallas_call(
        _output_kernel,
        grid=(t // tm,),
        in_specs=[tok(D_MODEL), tok(BRANCH_WIDTH), tok(HG_WIDTH), tok(2 * D_MODEL), tok(PLE_DIM)]
        + [_const_spec(c.shape) for c in consts],
        out_specs=tok(D_MODEL),
        out_shape=jax.ShapeDtypeStruct((t, D_MODEL), F32),
        compiler_params=pltpu.CompilerParams(
            dimension_semantics=("arbitrary",), vmem_limit_bytes=VMEM_LIMIT),
        name="output_stage",
    )(x2d, attn, rec, gates, p2d, *consts)


def _pad_heads(w, heads, width):
    rows = w.shape[0]
    w = w.reshape(rows, heads, width)
    w = jnp.pad(w, ((0, 0), (0, 0), (0, HEAD_PAD - width)))
    return w.reshape(rows, heads * HEAD_PAD)


def _qk_head_layout(nope, rope):
    like = nope if nope is not None else rope

    def z(width):
        return jnp.zeros(like.shape[:-1] + (width,), like.dtype)

    n0 = nope[..., :HEAD_HALF_NOPE] if nope is not None else z(HEAD_HALF_NOPE)
    n1 = nope[..., HEAD_HALF_NOPE:] if nope is not None else z(HEAD_HALF_NOPE)
    x1 = rope[..., :ROPE_HALF] if rope is not None else z(ROPE_HALF)
    x2 = rope[..., ROPE_HALF:] if rope is not None else z(ROPE_HALF)
    return jnp.concatenate([n0, x1, z(ROPE_HALF), n1, x2, z(ROPE_HALF)], axis=-1)


def _row(v):
    return v.reshape(1, -1).astype(F32)


def kernel(x, p, positions, mix_norm_g, w_in, q_a_norm_g, w_uq, kv_a_norm_g, w_ukv, q_norm_g, k_norm_g, hg_lb_logits, hg_out_norm_g, w_branch, w_out, ffn_norm_g, w_ffn_gate, w_ffn_up, w_ffn_down, ple_gate_norm_g, w_ple_gate, w_ple_proj, ple_post_norm_g):
    b, s, d = x.shape
    depth = w_in.shape[0]
    lower_bounds = jnp.cumsum(jax.nn.softmax(hg_lb_logits.astype(F32), axis=0), axis=0)
    pos3d = positions.reshape(b * s // TOK_TILE, 1, TOK_TILE)
    x2d = x.reshape(b * s, d)
    q_scale = QK_HEAD_DIM ** -0.5 * math.log2(math.e)

    for layer in range(depth):
        wi = w_in[layer]
        c0 = Q_LORA_RANK + KV_LORA_RANK
        c1 = c0 + QK_ROPE_DIM
        wall = jnp.concatenate([wi[:, :c0], _qk_head_layout(None, wi[:, c0:c1]), wi[:, c1:]],
                               axis=1).astype(BF16)
        wq3 = w_uq[layer].reshape(Q_LORA_RANK, MLA_HEADS, QK_HEAD_DIM)
        wuq = _qk_head_layout(wq3[..., :QK_NOPE_DIM], wq3[..., QK_NOPE_DIM:])
        wuq = wuq.reshape(Q_LORA_RANK, MLA_HEADS * HEAD_PAD).astype(BF16)
        wukv = w_ukv[layer].reshape(KV_LORA_RANK, MLA_HEADS, QK_NOPE_DIM + V_HEAD_DIM)
        wk = _qk_head_layout(wukv[..., :QK_NOPE_DIM], None).reshape(KV_LORA_RANK, -1)
        wv = _pad_heads(wukv[:, :, QK_NOPE_DIM:].reshape(KV_LORA_RANK, -1), MLA_HEADS, V_HEAD_DIM)
        wkv = jnp.concatenate([wk, wv], axis=1).astype(BF16)
        gq = q_norm_g[layer].astype(F32) * q_scale
        gk = k_norm_g[layer].astype(F32)

        q, k, v, hq, hf, hi, hg, gates = _input_stage(
            x2d, pos3d, b, s, _row(mix_norm_g[layer]), wall, _row(q_a_norm_g[layer]), wuq,
            _row(kv_a_norm_g[layer]), wkv,
            _row(_qk_head_layout(gq[:QK_NOPE_DIM], gq[QK_NOPE_DIM:])),
            _row(_qk_head_layout(gk[:QK_NOPE_DIM], gk[QK_NOPE_DIM:])))

        attn = _attention_stage(q, k, v)
        rec = _hgrn_stage(hq.reshape(b, s, HG_WIDTH), hf.reshape(b, s, HG_WIDTH),
                          hi.reshape(b, s, HG_WIDTH), hg.reshape(b, s, HG_WIDTH),
                          _row(lower_bounds[layer]), _row(hg_out_norm_g[layer]))

        x2d = _output_stage(
            x2d, attn.reshape(b * s, BRANCH_WIDTH), rec.reshape(b * s, HG_WIDTH), gates,
            p[layer].reshape(b * s, PLE_DIM),
            w_branch[layer, 0].astype(BF16), w_branch[layer, 1].astype(BF16),
            w_out[layer].astype(BF16), _row(ffn_norm_g[layer]),
            w_ffn_gate[layer].astype(BF16), w_ffn_up[layer].astype(BF16),
            w_ffn_down[layer].astype(BF16), _row(ple_gate_norm_g[layer]),
            w_ple_gate[layer].astype(BF16), w_ple_proj[layer].astype(BF16),
            _row(ple_post_norm_g[layer]))
    return x2d.reshape(b, s, d)
```

```python
import functools
import math

import numpy as np

import jax
import jax.numpy as jnp
from jax import lax
from jax.experimental import pallas as pl
from jax.experimental.pallas import tpu as pltpu

D_MODEL = 1024
MLA_HEADS = 8
QK_NOPE_DIM = 64
QK_ROPE_DIM = 32
ROPE_HALF = QK_ROPE_DIM // 2
QK_HEAD_DIM = QK_NOPE_DIM + QK_ROPE_DIM
V_HEAD_DIM = 64
Q_LORA_RANK = 384
KV_LORA_RANK = 256
ROPE_BASE = 10000.0
HG_HEADS = 4
HG_DIM = 128
HG_WIDTH = HG_HEADS * HG_DIM
BRANCH_WIDTH = MLA_HEADS * V_HEAD_DIM
FFN_HIDDEN = 2816
PLE_DIM = 256
EPS = 1e-6

LANES = 128
HEAD_PAD = LANES
HEAD_HALF_NOPE = QK_NOPE_DIM // 2
W_LAT = Q_LORA_RANK + KV_LORA_RANK + HEAD_PAD
DENOM_LANE = V_HEAD_DIM

VMEM_LIMIT = 56 * 1024 * 1024

TOK_TILE = 512
ATT_TILE = 512
ATT_HEADS = 4
HG_BLOCK = 256
HG_CHUNK = 64
HG_LEVELS = tuple(HG_CHUNK >> (i + 1) for i in range(HG_CHUNK.bit_length() - 1))
FFN_SPLITS = (1024, 1024, 768)

BF16 = jnp.bfloat16
F32 = jnp.float32


def _const_spec(shape):
    nd = len(shape)
    return pl.BlockSpec(shape, lambda *_: (0,) * nd, pipeline_mode=pl.Buffered(1))


def _rms(v, gain, width=None):
    n = v.shape[-1] if width is None else width
    ms = jnp.sum(v * v, axis=-1, keepdims=True) * (1.0 / n)
    return v * lax.rsqrt(ms + EPS) * gain


def _dot(a, b):
    return jnp.dot(a, b, preferred_element_type=F32)


def _dot_nt(a, b):
    return lax.dot_general(a, b, (((1,), (1,)), ((), ())), preferred_element_type=F32)


def _rope_tables(pos_row):
    tm = pos_row.shape[1]
    fidx = lax.broadcasted_iota(jnp.int32, (ROPE_HALF, 1), 0).astype(F32)
    inv_freq = jnp.exp(fidx * (-math.log(ROPE_BASE) * 2.0 / QK_ROPE_DIM))
    ang = pos_row.astype(F32) * inv_freq
    cos = jnp.cos(ang)
    sin = jnp.sin(ang)
    ones = jnp.ones((HEAD_HALF_NOPE, tm), F32)
    zeros_n = jnp.zeros((HEAD_HALF_NOPE, tm), F32)
    zeros_p = jnp.zeros((ROPE_HALF, tm), F32)
    cos_t = jnp.concatenate([ones, cos, zeros_p, ones, cos, zeros_p], axis=0)
    sin_t = jnp.concatenate([zeros_n, -sin, zeros_p, zeros_n, sin, zeros_p], axis=0)
    return cos_t.T, sin_t.T


def _input_kernel(x_ref, pos_ref, gmix_ref, wall_ref, gqa_ref, wuq_ref, gkva_ref, wkv_ref,
                  gq_ref, gk_ref,
                  q_ref, k_ref, v_ref, hq_ref, hf_ref, hi_ref, hg_ref, gates_ref):
    x = x_ref[...]
    h = _rms(x, gmix_ref[...]).astype(BF16)

    lat = _dot(h, wall_ref[:, :W_LAT])
    cq = _rms(lat[:, :Q_LORA_RANK], gqa_ref[...]).astype(BF16)
    ckv = _rms(lat[:, Q_LORA_RANK:Q_LORA_RANK + KV_LORA_RANK], gkva_ref[...]).astype(BF16)
    k_rope = lat[:, Q_LORA_RANK + KV_LORA_RANK:]

    cos_tab, sin_tab = _rope_tables(pos_ref[0])
    gq = gq_ref[...]
    gk = gk_ref[...]
    q_cos = cos_tab * gq
    q_sin = sin_tab * pltpu.roll(gq, LANES // 2, 1)
    kr = k_rope * gk
    kr = kr * cos_tab + pltpu.roll(kr, LANES // 2, 1) * sin_tab
    kr_ss = jnp.sum(k_rope * k_rope, axis=-1, keepdims=True)

    q_all = _dot(cq, wuq_ref[...])
    kv_all = _dot(ckv, wkv_ref[...])
    lane = lax.broadcasted_iota(jnp.int32, (1, LANES), 1)
    one_lane = (lane == DENOM_LANE).astype(F32)
    inv_width = 1.0 / QK_HEAD_DIM
    for hd in range(MLA_HEADS):
        sl = slice(hd * HEAD_PAD, (hd + 1) * HEAD_PAD)
        t = q_all[:, sl]
        r = lax.rsqrt(jnp.sum(t * t, axis=-1, keepdims=True) * inv_width + EPS)
        q_ref[0, hd] = ((t * q_cos + pltpu.roll(t, LANES // 2, 1) * q_sin) * r).astype(BF16)
        t = kv_all[:, sl]
        r = lax.rsqrt((jnp.sum(t * t, axis=-1, keepdims=True) + kr_ss) * inv_width + EPS)
        k_ref[0, hd] = ((t * gk + kr) * r).astype(BF16)
        vsl = slice(MLA_HEADS * HEAD_PAD + hd * HEAD_PAD, MLA_HEADS * HEAD_PAD + (hd + 1) * HEAD_PAD)
        v_ref[0, hd] = (kv_all[:, vsl] + one_lane).astype(BF16)

    hh = _dot(h, wall_ref[:, W_LAT:W_LAT + 4 * HG_WIDTH])
    hq_ref[...] = hh[:, :HG_WIDTH].astype(BF16)
    hf_ref[...] = hh[:, HG_WIDTH:2 * HG_WIDTH]
    hi_ref[...] = hh[:, 2 * HG_WIDTH:3 * HG_WIDTH].astype(BF16)
    hg_ref[...] = hh[:, 3 * HG_WIDTH:].astype(BF16)

    gates_ref[...] = jax.nn.sigmoid(_dot(h, wall_ref[:, W_LAT + 4 * HG_WIDTH:])).astype(BF16)


def _input_weight_kernel(w_ref, o_ref):
    c0 = Q_LORA_RANK + KV_LORA_RANK
    c1 = c0 + QK_ROPE_DIM
    w = w_ref[...]
    rows = w.shape[0]
    z_n = jnp.zeros((rows, HEAD_HALF_NOPE), F32)
    z_p = jnp.zeros((rows, ROPE_HALF), F32)
    x1 = w[:, c0:c0 + ROPE_HALF]
    x2 = w[:, c0 + ROPE_HALF:c1]
    o_ref[...] = jnp.concatenate([w[:, :c0], z_n, x1, z_p, z_n, x2, z_p, w[:, c1:]],
                                 axis=1).astype(BF16)


def _input_weight_layout(wi):
    rows, cols = wi.shape
    out_cols = cols - QK_ROPE_DIM + HEAD_PAD
    tr = 128
    return pl.pallas_call(
        _input_weight_kernel,
        grid=(rows // tr,),
        in_specs=[pl.BlockSpec((tr, cols), lambda i: (i, 0))],
        out_specs=pl.BlockSpec((tr, out_cols), lambda i: (i, 0)),
        out_shape=jax.ShapeDtypeStruct((rows, out_cols), BF16),
        compiler_params=pltpu.CompilerParams(
            dimension_semantics=("arbitrary",), vmem_limit_bytes=VMEM_LIMIT),
        name="input_weight_layout",
    )(wi)


def _input_stage(x2d, pos3d, b, s, gmix, wall, gqa, wuq, gkva, wkv, gq, gk):
    t = x2d.shape[0]
    tm = TOK_TILE
    nt = s // tm

    def tok(width):
        return pl.BlockSpec((tm, width), lambda i: (i, 0))

    def head_spec():
        return pl.BlockSpec((1, MLA_HEADS, tm, HEAD_PAD), lambda i: (i // nt, 0, i % nt, 0))

    head_shape = jax.ShapeDtypeStruct((b, MLA_HEADS, s, HEAD_PAD), BF16)
    consts = (gmix, wall, gqa, wuq, gkva, wkv, gq, gk)
    return pl.pallas_call(
        _input_kernel,
        grid=(t // tm,),
        in_specs=[tok(D_MODEL), pl.BlockSpec((1, 1, tm), lambda i: (i, 0, 0))]
        + [_const_spec(c.shape) for c in consts],
        out_specs=[head_spec(), head_spec(), head_spec(),
                   tok(HG_WIDTH), tok(HG_WIDTH), tok(HG_WIDTH), tok(HG_WIDTH), tok(2 * D_MODEL)],
        out_shape=[head_shape, head_shape, head_shape,
                   jax.ShapeDtypeStruct((t, HG_WIDTH), BF16),
                   jax.ShapeDtypeStruct((t, HG_WIDTH), F32),
                   jax.ShapeDtypeStruct((t, HG_WIDTH), BF16),
                   jax.ShapeDtypeStruct((t, HG_WIDTH), BF16),
                   jax.ShapeDtypeStruct((t, 2 * D_MODEL), BF16)],
        compiler_params=pltpu.CompilerParams(
            dimension_semantics=("arbitrary",), vmem_limit_bytes=VMEM_LIMIT),
        name="input_stage",
    )(x2d, pos3d, *consts)


def _attention_kernel(q_ref, k_ref, v_ref, o_ref, m_ref, acc_ref, sa_ref, sb_ref):
    qi = pl.program_id(2)
    tq = ATT_TILE
    row = lax.broadcasted_iota(jnp.int32, (tq, tq), 0)
    col = lax.broadcasted_iota(jnp.int32, (tq, tq), 1)
    m_ref[...] = jnp.full(m_ref.shape, -jnp.inf, F32)
    acc_ref[...] = jnp.zeros(acc_ref.shape, F32)

    def scores(j, s_ref):
        start = pl.multiple_of(j * tq, tq)
        for hd in range(ATT_HEADS):
            s_ref[hd] = _dot_nt(q_ref[0, hd], k_ref[0, hd, pl.ds(start, tq), :])

    def accumulate(j, s_ref, masked):
        for hd in range(ATT_HEADS):
            accumulate_head(j, hd, s_ref[hd], masked)

    def accumulate_head(j, hd, sc, masked):
        start = pl.multiple_of(j * tq, tq)
        if masked:
            sc = jnp.where(col <= row, sc, -jnp.inf)
        m_old = m_ref[hd]
        m_new = jnp.maximum(m_old, jnp.max(sc, axis=-1, keepdims=True))
        p = jnp.exp2(sc - jnp.concatenate([m_new] * (tq // LANES), axis=1)).astype(BF16)
        acc_ref[hd] = (jnp.exp2(m_old - m_new) * acc_ref[hd]
                       + _dot(p, v_ref[0, hd, pl.ds(start, tq), :]))
        m_ref[hd] = m_new

    scores(0, sa_ref)

    def body(t, carry):
        j = 2 * t
        scores(j + 1, sb_ref)
        accumulate(j, sa_ref, False)
        scores(j + 2, sa_ref)
        accumulate(j + 1, sb_ref, False)
        return carry

    lax.fori_loop(0, qi // 2, body, 0)

    @pl.when(qi % 2 == 0)
    def _():
        accumulate(qi, sa_ref, True)

    @pl.when(qi % 2 == 1)
    def _():
        scores(qi, sb_ref)
        accumulate(qi - 1, sa_ref, False)
        accumulate(qi, sb_ref, True)

    outs = []
    for hd in range(ATT_HEADS):
        acc = acc_ref[hd]
        outs.append(acc[:, :V_HEAD_DIM] / acc[:, DENOM_LANE:DENOM_LANE + 1])
    o_ref[0] = jnp.concatenate(outs, axis=-1).astype(BF16)


def _attention_stage(q, k, v):
    b, nh, s, _ = q.shape
    tq = ATT_TILE
    g = ATT_HEADS
    return pl.pallas_call(
        _attention_kernel,
        grid=(b, nh // g, s // tq),
        in_specs=[pl.BlockSpec((1, g, tq, HEAD_PAD), lambda bi, hp, i: (bi, hp, i, 0)),
                  pl.BlockSpec((1, g, s, HEAD_PAD), lambda bi, hp, i: (bi, hp, 0, 0)),
                  pl.BlockSpec((1, g, s, HEAD_PAD), lambda bi, hp, i: (bi, hp, 0, 0))],
        out_specs=pl.BlockSpec((1, tq, g * V_HEAD_DIM), lambda bi, hp, i: (bi, i, hp)),
        out_shape=jax.ShapeDtypeStruct((b, s, BRANCH_WIDTH), BF16),
        scratch_shapes=[pltpu.VMEM((g, tq, LANES), F32), pltpu.VMEM((g, tq, HEAD_PAD), F32),
                        pltpu.VMEM((g, tq, tq), F32), pltpu.VMEM((g, tq, tq), F32)],
        compiler_params=pltpu.CompilerParams(
            dimension_semantics=("arbitrary", "arbitrary", "arbitrary"),
            vmem_limit_bytes=VMEM_LIMIT),
        name="causal_attention",
    )(q, k, v)


def _hgrn_span_matrix():
    c = HG_CHUNK
    x = np.arange(c)[:, None]
    y = np.arange(c)[None, :]
    mats = [(y <= x)]
    for b in HG_LEVELS:
        r = (x // (2 * b)) * (2 * b) + b
        mats.append((y > np.minimum(x, r)) & (y <= np.maximum(x, r)))
    m = np.concatenate(mats, axis=0).astype(np.float32)
    return np.concatenate([m, m], axis=1)


def _hgrn_chunk_local(q, hf, v_bf, lb, span, right_rows, pair_masks):
    c = HG_CHUNK
    f = lb + (1.0 - lb) * jax.nn.sigmoid(hf)
    lf = jnp.log2(f)
    kk = 1.0 - f
    lf_hi = lf.astype(BF16)
    lf_lo = (lf - lf_hi.astype(F32)).astype(BF16)
    spans = _dot(span, jnp.concatenate([lf_hi, lf_lo], axis=0))
    cum = spans[:c]
    last = cum[c - 1:c, :]
    q_dec = (q * jnp.exp2(cum)).astype(BF16)
    k_dec = (kk * jnp.exp2(last - cum)).astype(BF16)
    q_bf = q.astype(BF16)
    k_bf = kk.astype(BF16)
    zs = [(jnp.where(right_rows[lvl], q, kk)
           * jnp.exp2(spans[(lvl + 1) * c:(lvl + 2) * c])).astype(BF16)
          for lvl in range(len(HG_LEVELS))]

    o_intra, incr = [], []
    for hd in range(HG_HEADS):
        sl = slice(hd * HG_DIM, (hd + 1) * HG_DIM)
        att = jnp.where(pair_masks[0], _dot_nt(q_bf[:, sl], k_bf[:, sl]), 0.0)
        for lvl in range(len(HG_LEVELS)):
            z = zs[lvl][:, sl]
            att = jnp.where(pair_masks[lvl + 1], _dot_nt(z, z), att)
        v_h = v_bf[:, sl]
        o_intra.append(_dot(att.astype(BF16), v_h))
        incr.append(_dot(v_h.astype(F32).T.astype(BF16), k_dec[:, sl]))
    return o_intra, q_dec, incr, jnp.exp2(last)


def _hgrn_kernel(hq_ref, hf_ref, hi_ref, hg_ref, lb_ref, gout_ref, span_ref, o_ref, state_ref):
    @pl.when(pl.program_id(1) == 0)
    def _():
        state_ref[...] = jnp.zeros(state_ref.shape, F32)

    c = HG_CHUNK
    t_i = lax.broadcasted_iota(jnp.int32, (c, c), 0)
    s_i = lax.broadcasted_iota(jnp.int32, (c, c), 1)
    row = lax.broadcasted_iota(jnp.int32, (c, HG_WIDTH), 0)
    pair_masks = [t_i == s_i] + [((t_i // b) ^ (s_i // b) == 1) & (s_i < t_i) for b in HG_LEVELS]
    right_rows = [(row // b) % 2 == 1 for b in HG_LEVELS]
    span = span_ref[...]
    lb = lb_ref[...]
    n_chunks = HG_BLOCK // c

    local = []
    for ci in range(n_chunks):
        rows = slice(ci * c, (ci + 1) * c)
        local.append(_hgrn_chunk_local(hq_ref[0, rows, :].astype(F32), hf_ref[0, rows, :],
                                       hi_ref[0, rows, :], lb, span, right_rows, pair_masks))

    gout = gout_ref[...]
    for hd in range(HG_HEADS):
        sl = slice(hd * HG_DIM, (hd + 1) * HG_DIM)
        state_t = state_ref[hd]
        for ci in range(n_chunks):
            rows = slice(ci * c, (ci + 1) * c)
            o_intra, q_dec, incr, decay = local[ci]
            o = o_intra[hd] + _dot_nt(q_dec[:, sl], state_t.astype(BF16))
            state_t = state_t * decay[:, sl] + incr[hd]
            gate = hg_ref[0, rows, sl].astype(F32)
            o_ref[0, rows, sl] = (_rms(o, gout) * (gate * jax.nn.sigmoid(gate))).astype(BF16)
        state_ref[hd] = state_t


def _hgrn_stage(hq, hf, hi, hg, lb, gout):
    span = jnp.asarray(_hgrn_span_matrix(), BF16)
    b, s, _ = hq.shape
    tb = HG_BLOCK
    tok = pl.BlockSpec((1, tb, HG_WIDTH), lambda bi, i: (bi, i, 0))
    return pl.pallas_call(
        _hgrn_kernel,
        grid=(b, s // tb),
        in_specs=[tok, tok, tok, tok, _const_spec(lb.shape), _const_spec(gout.shape),
                  _const_spec(span.shape)],
        out_specs=tok,
        out_shape=jax.ShapeDtypeStruct((b, s, HG_WIDTH), BF16),
        scratch_shapes=[pltpu.VMEM((HG_HEADS, HG_DIM, HG_DIM), F32)],
        compiler_params=pltpu.CompilerParams(
            dimension_semantics=("arbitrary", "arbitrary"), vmem_limit_bytes=VMEM_LIMIT),
        name="hgrn2_recurrence",
    )(hq, hf, hi, hg, lb, gout, span)


def _output_kernel(x_ref, attn_ref, rec_ref, gates_ref, p_ref,
                   wb0_ref, wb1_ref, wout_ref, gffn_ref, wgate_ref, wup_ref, wdown_ref,
                   gpg_ref, wpg_ref, wpp_ref, gpost_ref, o_ref):
    y0 = _dot(attn_ref[...], wb0_ref[...])
    y1 = _dot(rec_ref[...], wb1_ref[...])
    g0 = gates_ref[:, :D_MODEL].astype(F32)
    g1 = gates_ref[:, D_MODEL:].astype(F32)
    merged = (g0 * y0 + g1 * y1).astype(BF16)
    x1 = x_ref[...] + _dot(merged, wout_ref[...])

    h2 = _rms(x1, gffn_ref[...]).astype(BF16)
    x2 = x1
    lo = 0
    for width in FFN_SPLITS:
        a = _dot(h2, wgate_ref[:, lo:lo + width])
        u = _dot(h2, wup_ref[:, lo:lo + width])
        z = (a * jax.nn.sigmoid(a) * u).astype(BF16)
        x2 = x2 + _dot(z, wdown_ref[lo:lo + width, :])
        lo += width

    e = _rms(_dot(p_ref[...].astype(BF16), wpp_ref[...]), gpost_ref[...])
    g = jax.nn.sigmoid(_dot(_rms(x2, gpg_ref[...]).astype(BF16), wpg_ref[...]))
    o_ref[...] = x2 + g * e


def _output_stage(x2d, attn, rec, gates, p2d, wb0, wb1, wout, gffn, wgate, wup, wdown,
                  gpg, wpg, wpp, gpost):
    t = x2d.shape[0]
    tm = TOK_TILE

    def tok(width):
        return pl.BlockSpec((tm, width), lambda i: (i, 0))

    consts = (wb0, wb1, wout, gffn, wgate, wup, wdown, gpg, wpg, wpp, gpost)
    return pl.pallas_call(
        _output_kernel,
        grid=(t // tm,),
        in_specs=[tok(D_MODEL), tok(BRANCH_WIDTH), tok(HG_WIDTH), tok(2 * D_MODEL), tok(PLE_DIM)]
        + [_const_spec(c.shape) for c in consts],
        out_specs=tok(D_MODEL),
        out_shape=jax.ShapeDtypeStruct((t, D_MODEL), F32),
        compiler_params=pltpu.CompilerParams(
            dimension_semantics=("arbitrary",), vmem_limit_bytes=VMEM_LIMIT),
        name="output_stage",
    )(x2d, attn, rec, gates, p2d, *consts)


def _pad_heads(w, heads, width):
    rows = w.shape[0]
    w = w.reshape(rows, heads, width)
    w = jnp.pad(w, ((0, 0), (0, 0), (0, HEAD_PAD - width)))
    return w.reshape(rows, heads * HEAD_PAD)


def _qk_head_layout(nope, rope):
    like = nope if nope is not None else rope

    def z(width):
        return jnp.zeros(like.shape[:-1] + (width,), like.dtype)

    n0 = nope[..., :HEAD_HALF_NOPE] if nope is not None else z(HEAD_HALF_NOPE)
    n1 = nope[..., HEAD_HALF_NOPE:] if nope is not None else z(HEAD_HALF_NOPE)
    x1 = rope[..., :ROPE_HALF] if rope is not None else z(ROPE_HALF)
    x2 = rope[..., ROPE_HALF:] if rope is not None else z(ROPE_HALF)
    return jnp.concatenate([n0, x1, z(ROPE_HALF), n1, x2, z(ROPE_HALF)], axis=-1)


def _row(v):
    return v.reshape(1, -1).astype(F32)


def kernel(x, p, positions, mix_norm_g, w_in, q_a_norm_g, w_uq, kv_a_norm_g, w_ukv, q_norm_g, k_norm_g, hg_lb_logits, hg_out_norm_g, w_branch, w_out, ffn_norm_g, w_ffn_gate, w_ffn_up, w_ffn_down, ple_gate_norm_g, w_ple_gate, w_ple_proj, ple_post_norm_g):
    b, s, d = x.shape
    depth = w_in.shape[0]
    lower_bounds = jnp.cumsum(jax.nn.softmax(hg_lb_logits.astype(F32), axis=0), axis=0)
    pos3d = positions.reshape(b * s // TOK_TILE, 1, TOK_TILE)
    x2d = x.reshape(b * s, d)
    q_scale = QK_HEAD_DIM ** -0.5 * math.log2(math.e)

    for layer in range(depth):
        wi = w_in[layer]
        c0 = Q_LORA_RANK + KV_LORA_RANK
        c1 = c0 + QK_ROPE_DIM
        wall = _input_weight_layout(wi)
        wq3 = w_uq[layer].reshape(Q_LORA_RANK, MLA_HEADS, QK_HEAD_DIM)
        wuq = _qk_head_layout(wq3[..., :QK_NOPE_DIM], wq3[..., QK_NOPE_DIM:])
        wuq = wuq.reshape(Q_LORA_RANK, MLA_HEADS * HEAD_PAD).astype(BF16)
        wukv = w_ukv[layer].reshape(KV_LORA_RANK, MLA_HEADS, QK_NOPE_DIM + V_HEAD_DIM)
        wk = _qk_head_layout(wukv[..., :QK_NOPE_DIM], None).reshape(KV_LORA_RANK, -1)
        wv = _pad_heads(wukv[:, :, QK_NOPE_DIM:].reshape(KV_LORA_RANK, -1), MLA_HEADS, V_HEAD_DIM)
        wkv = jnp.concatenate([wk, wv], axis=1).astype(BF16)
        gq = q_norm_g[layer].astype(F32) * q_scale
        gk = k_norm_g[layer].astype(F32)

        q, k, v, hq, hf, hi, hg, gates = _input_stage(
            x2d, pos3d, b, s, _row(mix_norm_g[layer]), wall, _row(q_a_norm_g[layer]), wuq,
            _row(kv_a_norm_g[layer]), wkv,
            _row(_qk_head_layout(gq[:QK_NOPE_DIM], gq[QK_NOPE_DIM:])),
            _row(_qk_head_layout(gk[:QK_NOPE_DIM], gk[QK_NOPE_DIM:])))

        attn = _attention_stage(q, k, v)
        rec = _hgrn_stage(hq.reshape(b, s, HG_WIDTH), hf.reshape(b, s, HG_WIDTH),
                          hi.reshape(b, s, HG_WIDTH), hg.reshape(b, s, HG_WIDTH),
                          _row(lower_bounds[layer]), _row(hg_out_norm_g[layer]))

        x2d = _output_stage(
            x2d, attn.reshape(b * s, BRANCH_WIDTH), rec.reshape(b * s, HG_WIDTH), gates,
            p[layer].reshape(b * s, PLE_DIM),
            w_branch[layer, 0].astype(BF16), w_branch[layer, 1].astype(BF16),
            w_out[layer].astype(BF16), _row(ffn_norm_g[layer]),
            w_ffn_gate[layer].astype(BF16), w_ffn_up[layer].astype(BF16),
            w_ffn_down[layer].astype(BF16), _row(ple_gate_norm_g[layer]),
            w_ple_gate[layer].astype(BF16), w_ple_proj[layer].astype(BF16),
            _row(ple_post_norm_g[layer]))
    return x2d.reshape(b, s, d)
```

```python
import functools
import math

import numpy as np

import jax
import jax.numpy as jnp
from jax import lax
from jax.experimental import pallas as pl
from jax.experimental.pallas import tpu as pltpu

D_MODEL = 1024
MLA_HEADS = 8
QK_NOPE_DIM = 64
QK_ROPE_DIM = 32
ROPE_HALF = QK_ROPE_DIM // 2
QK_HEAD_DIM = QK_NOPE_DIM + QK_ROPE_DIM
V_HEAD_DIM = 64
Q_LORA_RANK = 384
KV_LORA_RANK = 256
ROPE_BASE = 10000.0
HG_HEADS = 4
HG_DIM = 128
HG_WIDTH = HG_HEADS * HG_DIM
BRANCH_WIDTH = MLA_HEADS * V_HEAD_DIM
FFN_HIDDEN = 2816
PLE_DIM = 256
EPS = 1e-6

LANES = 128
HEAD_PAD = LANES
HEAD_HALF_NOPE = QK_NOPE_DIM // 2
W_LAT = Q_LORA_RANK + KV_LORA_RANK + HEAD_PAD
DENOM_LANE = V_HEAD_DIM

VMEM_LIMIT = 56 * 1024 * 1024

TOK_TILE = 512
ATT_TILE = 512
ATT_HEADS = 4
HG_BLOCK = 256
HG_CHUNK = 64
HG_LEVELS = tuple(HG_CHUNK >> (i + 1) for i in range(HG_CHUNK.bit_length() - 1))
FFN_SPLITS = (1024, 1024, 768)

BF16 = jnp.bfloat16
F32 = jnp.float32


def _const_spec(shape):
    nd = len(shape)
    return pl.BlockSpec(shape, lambda *_: (0,) * nd, pipeline_mode=pl.Buffered(1))


def _rms(v, gain, width=None):
    n = v.shape[-1] if width is None else width
    ms = jnp.sum(v * v, axis=-1, keepdims=True) * (1.0 / n)
    return v * lax.rsqrt(ms + EPS) * gain


def _dot(a, b):
    return jnp.dot(a, b, preferred_element_type=F32)


def _dot_nt(a, b):
    return lax.dot_general(a, b, (((1,), (1,)), ((), ())), preferred_element_type=F32)


def _rope_tables(pos_row):
    tm = pos_row.shape[1]
    fidx = lax.broadcasted_iota(jnp.int32, (ROPE_HALF, 1), 0).astype(F32)
    inv_freq = jnp.exp(fidx * (-math.log(ROPE_BASE) * 2.0 / QK_ROPE_DIM))
    ang = pos_row.astype(F32) * inv_freq
    cos = jnp.cos(ang)
    sin = jnp.sin(ang)
    ones = jnp.ones((HEAD_HALF_NOPE, tm), F32)
    zeros_n = jnp.zeros((HEAD_HALF_NOPE, tm), F32)
    zeros_p = jnp.zeros((ROPE_HALF, tm), F32)
    cos_t = jnp.concatenate([ones, cos, zeros_p, ones, cos, zeros_p], axis=0)
    sin_t = jnp.concatenate([zeros_n, -sin, zeros_p, zeros_n, sin, zeros_p], axis=0)
    return cos_t.T, sin_t.T


def _input_kernel(x_ref, pos_ref, gmix_ref, wall_ref, gqa_ref, wuq_ref, gkva_ref, wkv_ref,
                  gq_ref, gk_ref,
                  q_ref, k_ref, v_ref, hq_ref, hf_ref, hi_ref, hg_ref, gates_ref):
    x = x_ref[...]
    h = _rms(x, gmix_ref[...]).astype(BF16)

    lat = _dot(h, wall_ref[:, :W_LAT])
    cq = _rms(lat[:, :Q_LORA_RANK], gqa_ref[...]).astype(BF16)
    ckv = _rms(lat[:, Q_LORA_RANK:Q_LORA_RANK + KV_LORA_RANK], gkva_ref[...]).astype(BF16)
    k_rope = lat[:, Q_LORA_RANK + KV_LORA_RANK:]

    cos_tab, sin_tab = _rope_tables(pos_ref[0])
    gq = gq_ref[...]
    gk = gk_ref[...]
    q_cos = cos_tab * gq
    q_sin = sin_tab * pltpu.roll(gq, LANES // 2, 1)
    kr = k_rope * gk
    kr = kr * cos_tab + pltpu.roll(kr, LANES // 2, 1) * sin_tab
    kr_ss = jnp.sum(k_rope * k_rope, axis=-1, keepdims=True)

    q_all = _dot(cq, wuq_ref[...])
    kv_all = _dot(ckv, wkv_ref[...])
    lane = lax.broadcasted_iota(jnp.int32, (1, LANES), 1)
    one_lane = (lane == DENOM_LANE).astype(F32)
    inv_width = 1.0 / QK_HEAD_DIM
    for hd in range(MLA_HEADS):
        sl = slice(hd * HEAD_PAD, (hd + 1) * HEAD_PAD)
        t = q_all[:, sl]
        r = lax.rsqrt(jnp.sum(t * t, axis=-1, keepdims=True) * inv_width + EPS)
        q_ref[0, hd] = ((t * q_cos + pltpu.roll(t, LANES // 2, 1) * q_sin) * r).astype(BF16)
        t = kv_all[:, sl]
        r = lax.rsqrt((jnp.sum(t * t, axis=-1, keepdims=True) + kr_ss) * inv_width + EPS)
        k_ref[0, hd] = ((t * gk + kr) * r).astype(BF16)
        vsl = slice(MLA_HEADS * HEAD_PAD + hd * HEAD_PAD, MLA_HEADS * HEAD_PAD + (hd + 1) * HEAD_PAD)
        v_ref[0, hd] = (kv_all[:, vsl] + one_lane).astype(BF16)

    hh = _dot(h, wall_ref[:, W_LAT:W_LAT + 4 * HG_WIDTH])
    hq_ref[...] = hh[:, :HG_WIDTH].astype(BF16)
    hf_ref[...] = hh[:, HG_WIDTH:2 * HG_WIDTH]
    hi_ref[...] = hh[:, 2 * HG_WIDTH:3 * HG_WIDTH].astype(BF16)
    hg_ref[...] = hh[:, 3 * HG_WIDTH:].astype(BF16)

    gates_ref[...] = jax.nn.sigmoid(_dot(h, wall_ref[:, W_LAT + 4 * HG_WIDTH:])).astype(BF16)


W_PREP_COLS = 256


def _input_weight_kernel(wt_ref, o_ref):
    c0 = Q_LORA_RANK + KV_LORA_RANK
    c1 = c0 + QK_ROPE_DIM
    j = pl.program_id(0)
    n = W_PREP_COLS
    lat_full = c0 // n

    @pl.when(j < lat_full)
    def _():
        start = pl.multiple_of(j * n, n)
        o_ref[...] = wt_ref[pl.ds(start, n), :].T.astype(BF16)

    @pl.when(j == lat_full)
    def _():
        d = wt_ref.shape[1]
        z_n = jnp.zeros((HEAD_HALF_NOPE, d), F32)
        z_p = jnp.zeros((ROPE_HALF, d), F32)
        blk = jnp.concatenate([wt_ref[lat_full * n:c0, :], z_n, wt_ref[c0:c0 + ROPE_HALF, :], z_p,
                               z_n, wt_ref[c0 + ROPE_HALF:c1, :], z_p], axis=0)
        o_ref[...] = blk.T.astype(BF16)

    @pl.when(j > lat_full)
    def _():
        start = pl.multiple_of(c1 + (j - lat_full - 1) * n, 2 * ROPE_HALF)
        o_ref[...] = wt_ref[pl.ds(start, n), :].T.astype(BF16)


def _input_weight_layout(wt):
    cols, rows = wt.shape
    out_cols = cols - QK_ROPE_DIM + HEAD_PAD
    return pl.pallas_call(
        _input_weight_kernel,
        grid=(out_cols // W_PREP_COLS,),
        in_specs=[_const_spec(wt.shape)],
        out_specs=pl.BlockSpec((rows, W_PREP_COLS), lambda j: (0, j)),
        out_shape=jax.ShapeDtypeStruct((rows, out_cols), BF16),
        compiler_params=pltpu.CompilerParams(
            dimension_semantics=("arbitrary",), vmem_limit_bytes=VMEM_LIMIT),
        name="input_weight_layout",
    )(wt)


def _input_stage(x2d, pos3d, b, s, gmix, wall, gqa, wuq, gkva, wkv, gq, gk):
    t = x2d.shape[0]
    tm = TOK_TILE
    nt = s // tm

    def tok(width):
        return pl.BlockSpec((tm, width), lambda i: (i, 0))

    def head_spec():
        return pl.BlockSpec((1, MLA_HEADS, tm, HEAD_PAD), lambda i: (i // nt, 0, i % nt, 0))

    head_shape = jax.ShapeDtypeStruct((b, MLA_HEADS, s, HEAD_PAD), BF16)
    consts = (gmix, wall, gqa, wuq, gkva, wkv, gq, gk)
    return pl.pallas_call(
        _input_kernel,
        grid=(t // tm,),
        in_specs=[tok(D_MODEL), pl.BlockSpec((1, 1, tm), lambda i: (i, 0, 0))]
        + [_const_spec(c.shape) for c in consts],
        out_specs=[head_spec(), head_spec(), head_spec(),
                   tok(HG_WIDTH), tok(HG_WIDTH), tok(HG_WIDTH), tok(HG_WIDTH), tok(2 * D_MODEL)],
        out_shape=[head_shape, head_shape, head_shape,
                   jax.ShapeDtypeStruct((t, HG_WIDTH), BF16),
                   jax.ShapeDtypeStruct((t, HG_WIDTH), F32),
                   jax.ShapeDtypeStruct((t, HG_WIDTH), BF16),
                   jax.ShapeDtypeStruct((t, HG_WIDTH), BF16),
                   jax.ShapeDtypeStruct((t, 2 * D_MODEL), BF16)],
        compiler_params=pltpu.CompilerParams(
            dimension_semantics=("arbitrary",), vmem_limit_bytes=VMEM_LIMIT),
        name="input_stage",
    )(x2d, pos3d, *consts)


def _attention_kernel(q_ref, k_ref, v_ref, o_ref, m_ref, acc_ref, sa_ref, sb_ref):
    qi = pl.program_id(2)
    tq = ATT_TILE
    row = lax.broadcasted_iota(jnp.int32, (tq, tq), 0)
    col = lax.broadcasted_iota(jnp.int32, (tq, tq), 1)
    m_ref[...] = jnp.full(m_ref.shape, -jnp.inf, F32)
    acc_ref[...] = jnp.zeros(acc_ref.shape, F32)

    def scores(j, s_ref):
        start = pl.multiple_of(j * tq, tq)
        for hd in range(ATT_HEADS):
            s_ref[hd] = _dot_nt(q_ref[0, hd], k_ref[0, hd, pl.ds(start, tq), :])

    def accumulate(j, s_ref, masked):
        for hd in range(ATT_HEADS):
            accumulate_head(j, hd, s_ref[hd], masked)

    def accumulate_head(j, hd, sc, masked):
        start = pl.multiple_of(j * tq, tq)
        if masked:
            sc = jnp.where(col <= row, sc, -jnp.inf)
        m_old = m_ref[hd]
        m_new = jnp.maximum(m_old, jnp.max(sc, axis=-1, keepdims=True))
        p = jnp.exp2(sc - jnp.concatenate([m_new] * (tq // LANES), axis=1)).astype(BF16)
        acc_ref[hd] = (jnp.exp2(m_old - m_new) * acc_ref[hd]
                       + _dot(p, v_ref[0, hd, pl.ds(start, tq), :]))
        m_ref[hd] = m_new

    scores(0, sa_ref)

    def body(t, carry):
        j = 2 * t
        scores(j + 1, sb_ref)
        accumulate(j, sa_ref, False)
        scores(j + 2, sa_ref)
        accumulate(j + 1, sb_ref, False)
        return carry

    lax.fori_loop(0, qi // 2, body, 0)

    @pl.when(qi % 2 == 0)
    def _():
        accumulate(qi, sa_ref, True)

    @pl.when(qi % 2 == 1)
    def _():
        scores(qi, sb_ref)
        accumulate(qi - 1, sa_ref, False)
        accumulate(qi, sb_ref, True)

    outs = []
    for hd in range(ATT_HEADS):
        acc = acc_ref[hd]
        outs.append(acc[:, :V_HEAD_DIM] / acc[:, DENOM_LANE:DENOM_LANE + 1])
    o_ref[0] = jnp.concatenate(outs, axis=-1).astype(BF16)


def _attention_stage(q, k, v):
    b, nh, s, _ = q.shape
    tq = ATT_TILE
    g = ATT_HEADS
    return pl.pallas_call(
        _attention_kernel,
        grid=(b, nh // g, s // tq),
        in_specs=[pl.BlockSpec((1, g, tq, HEAD_PAD), lambda bi, hp, i: (bi, hp, i, 0)),
                  pl.BlockSpec((1, g, s, HEAD_PAD), lambda bi, hp, i: (bi, hp, 0, 0)),
                  pl.BlockSpec((1, g, s, HEAD_PAD), lambda bi, hp, i: (bi, hp, 0, 0))],
        out_specs=pl.BlockSpec((1, tq, g * V_HEAD_DIM), lambda bi, hp, i: (bi, i, hp)),
        out_shape=jax.ShapeDtypeStruct((b, s, BRANCH_WIDTH), BF16),
        scratch_shapes=[pltpu.VMEM((g, tq, LANES), F32), pltpu.VMEM((g, tq, HEAD_PAD), F32),
                        pltpu.VMEM((g, tq, tq), F32), pltpu.VMEM((g, tq, tq), F32)],
        compiler_params=pltpu.CompilerParams(
            dimension_semantics=("arbitrary", "arbitrary", "arbitrary"),
            vmem_limit_bytes=VMEM_LIMIT),
        name="causal_attention",
    )(q, k, v)


def _hgrn_span_matrix():
    c = HG_CHUNK
    x = np.arange(c)[:, None]
    y = np.arange(c)[None, :]
    mats = [(y <= x)]
    for b in HG_LEVELS:
        r = (x // (2 * b)) * (2 * b) + b
        mats.append((y > np.minimum(x, r)) & (y <= np.maximum(x, r)))
    m = np.concatenate(mats, axis=0).astype(np.float32)
    return np.concatenate([m, m], axis=1)


def _hgrn_chunk_local(q, hf, v_bf, lb, span, right_rows, pair_masks):
    c = HG_CHUNK
    f = lb + (1.0 - lb) * jax.nn.sigmoid(hf)
    lf = jnp.log2(f)
    kk = 1.0 - f
    lf_hi = lf.astype(BF16)
    lf_lo = (lf - lf_hi.astype(F32)).astype(BF16)
    spans = _dot(span, jnp.concatenate([lf_hi, lf_lo], axis=0))
    cum = spans[:c]
    last = cum[c - 1:c, :]
    q_dec = (q * jnp.exp2(cum)).astype(BF16)
    k_dec = (kk * jnp.exp2(last - cum)).astype(BF16)
    q_bf = q.astype(BF16)
    k_bf = kk.astype(BF16)
    zs = [(jnp.where(right_rows[lvl], q, kk)
           * jnp.exp2(spans[(lvl + 1) * c:(lvl + 2) * c])).astype(BF16)
          for lvl in range(len(HG_LEVELS))]

    o_intra, incr = [], []
    for hd in range(HG_HEADS):
        sl = slice(hd * HG_DIM, (hd + 1) * HG_DIM)
        att = jnp.where(pair_masks[0], _dot_nt(q_bf[:, sl], k_bf[:, sl]), 0.0)
        for lvl in range(len(HG_LEVELS)):
            z = zs[lvl][:, sl]
            att = jnp.where(pair_masks[lvl + 1], _dot_nt(z, z), att)
        v_h = v_bf[:, sl]
        o_intra.append(_dot(att.astype(BF16), v_h))
        incr.append(_dot(v_h.astype(F32).T.astype(BF16), k_dec[:, sl]))
    return o_intra, q_dec, incr, jnp.exp2(last)


def _hgrn_kernel(hq_ref, hf_ref, hi_ref, hg_ref, lb_ref, gout_ref, span_ref, o_ref, state_ref):
    @pl.when(pl.program_id(1) == 0)
    def _():
        state_ref[...] = jnp.zeros(state_ref.shape, F32)

    c = HG_CHUNK
    t_i = lax.broadcasted_iota(jnp.int32, (c, c), 0)
    s_i = lax.broadcasted_iota(jnp.int32, (c, c), 1)
    row = lax.broadcasted_iota(jnp.int32, (c, HG_WIDTH), 0)
    pair_masks = [t_i == s_i] + [((t_i // b) ^ (s_i // b) == 1) & (s_i < t_i) for b in HG_LEVELS]
    right_rows = [(row // b) % 2 == 1 for b in HG_LEVELS]
    span = span_ref[...]
    lb = lb_ref[...]
    n_chunks = HG_BLOCK // c

    local = []
    for ci in range(n_chunks):
        rows = slice(ci * c, (ci + 1) * c)
        local.append(_hgrn_chunk_local(hq_ref[0, rows, :].astype(F32), hf_ref[0, rows, :],
                                       hi_ref[0, rows, :], lb, span, right_rows, pair_masks))

    gout = gout_ref[...]
    for hd in range(HG_HEADS):
        sl = slice(hd * HG_DIM, (hd + 1) * HG_DIM)
        state_t = state_ref[hd]
        for ci in range(n_chunks):
            rows = slice(ci * c, (ci + 1) * c)
            o_intra, q_dec, incr, decay = local[ci]
            o = o_intra[hd] + _dot_nt(q_dec[:, sl], state_t.astype(BF16))
            state_t = state_t * decay[:, sl] + incr[hd]
            gate = hg_ref[0, rows, sl].astype(F32)
            o_ref[0, rows, sl] = (_rms(o, gout) * (gate * jax.nn.sigmoid(gate))).astype(BF16)
        state_ref[hd] = state_t


def _hgrn_stage(hq, hf, hi, hg, lb, gout):
    span = jnp.asarray(_hgrn_span_matrix(), BF16)
    b, s, _ = hq.shape
    tb = HG_BLOCK
    tok = pl.BlockSpec((1, tb, HG_WIDTH), lambda bi, i: (bi, i, 0))
    return pl.pallas_call(
        _hgrn_kernel,
        grid=(b, s // tb),
        in_specs=[tok, tok, tok, tok, _const_spec(lb.shape), _const_spec(gout.shape),
                  _const_spec(span.shape)],
        out_specs=tok,
        out_shape=jax.ShapeDtypeStruct((b, s, HG_WIDTH), BF16),
        scratch_shapes=[pltpu.VMEM((HG_HEADS, HG_DIM, HG_DIM), F32)],
        compiler_params=pltpu.CompilerParams(
            dimension_semantics=("arbitrary", "arbitrary"), vmem_limit_bytes=VMEM_LIMIT),
        name="hgrn2_recurrence",
    )(hq, hf, hi, hg, lb, gout, span)


def _output_kernel(x_ref, attn_ref, rec_ref, gates_ref, p_ref,
                   wb0_ref, wb1_ref, wout_ref, gffn_ref, wgate_ref, wup_ref, wdown_ref,
                   gpg_ref, wpg_ref, wpp_ref, gpost_ref, o_ref):
    y0 = _dot(attn_ref[...], wb0_ref[...])
    y1 = _dot(rec_ref[...], wb1_ref[...])
    g0 = gates_ref[:, :D_MODEL].astype(F32)
    g1 = gates_ref[:, D_MODEL:].astype(F32)
    merged = (g0 * y0 + g1 * y1).astype(BF16)
    x1 = x_ref[...] + _dot(merged, wout_ref[...])

    h2 = _rms(x1, gffn_ref[...]).astype(BF16)
    x2 = x1
    lo = 0
    for width in FFN_SPLITS:
        a = _dot(h2, wgate_ref[:, lo:lo + width])
        u = _dot(h2, wup_ref[:, lo:lo + width])
        z = (a * jax.nn.sigmoid(a) * u).astype(BF16)
        x2 = x2 + _dot(z, wdown_ref[lo:lo + width, :])
        lo += width

    e = _rms(_dot(p_ref[...].astype(BF16), wpp_ref[...]), gpost_ref[...])
    g = jax.nn.sigmoid(_dot(_rms(x2, gpg_ref[...]).astype(BF16), wpg_ref[...]))
    o_ref[...] = x2 + g * e


def _output_stage(x2d, attn, rec, gates, p2d, wb0, wb1, wout, gffn, wgate, wup, wdown,
                  gpg, wpg, wpp, gpost):
    t = x2d.shape[0]
    tm = TOK_TILE

    def tok(width):
        return pl.BlockSpec((tm, width), lambda i: (i, 0))

    consts = (wb0, wb1, wout, gffn, wgate, wup, wdown, gpg, wpg, wpp, gpost)
    return pl.pallas_call(
        _output_kernel,
        grid=(t // tm,),
        in_specs=[tok(D_MODEL), tok(BRANCH_WIDTH), tok(HG_WIDTH), tok(2 * D_MODEL), tok(PLE_DIM)]
        + [_const_spec(c.shape) for c in consts],
        out_specs=tok(D_MODEL),
        out_shape=jax.ShapeDtypeStruct((t, D_MODEL), F32),
        compiler_params=pltpu.CompilerParams(
            dimension_semantics=("arbitrary",), vmem_limit_bytes=VMEM_LIMIT),
        name="output_stage",
    )(x2d, attn, rec, gates, p2d, *consts)


def _pad_heads(w, heads, width):
    rows = w.shape[0]
    w = w.reshape(rows, heads, width)
    w = jnp.pad(w, ((0, 0), (0, 0), (0, HEAD_PAD - width)))
    return w.reshape(rows, heads * HEAD_PAD)


def _qk_head_layout(nope, rope):
    like = nope if nope is not None else rope

    def z(width):
        return jnp.zeros(like.shape[:-1] + (width,), like.dtype)

    n0 = nope[..., :HEAD_HALF_NOPE] if nope is not None else z(HEAD_HALF_NOPE)
    n1 = nope[..., HEAD_HALF_NOPE:] if nope is not None else z(HEAD_HALF_NOPE)
    x1 = rope[..., :ROPE_HALF] if rope is not None else z(ROPE_HALF)
    x2 = rope[..., ROPE_HALF:] if rope is not None else z(ROPE_HALF)
    return jnp.concatenate([n0, x1, z(ROPE_HALF), n1, x2, z(ROPE_HALF)], axis=-1)


def _row(v):
    return v.reshape(1, -1).astype(F32)


def kernel(x, p, positions, mix_norm_g, w_in, q_a_norm_g, w_uq, kv_a_norm_g, w_ukv, q_norm_g, k_norm_g, hg_lb_logits, hg_out_norm_g, w_branch, w_out, ffn_norm_g, w_ffn_gate, w_ffn_up, w_ffn_down, ple_gate_norm_g, w_ple_gate, w_ple_proj, ple_post_norm_g):
    b, s, d = x.shape
    depth = w_in.shape[0]
    lower_bounds = jnp.cumsum(jax.nn.softmax(hg_lb_logits.astype(F32), axis=0), axis=0)
    pos3d = positions.reshape(b * s // TOK_TILE, 1, TOK_TILE)
    x2d = x.reshape(b * s, d)
    q_scale = QK_HEAD_DIM ** -0.5 * math.log2(math.e)

    for layer in range(depth):
        wi = w_in[layer]
        c0 = Q_LORA_RANK + KV_LORA_RANK
        c1 = c0 + QK_ROPE_DIM
        wall = _input_weight_layout(wi.T)
        wq3 = w_uq[layer].reshape(Q_LORA_RANK, MLA_HEADS, QK_HEAD_DIM)
        wuq = _qk_head_layout(wq3[..., :QK_NOPE_DIM], wq3[..., QK_NOPE_DIM:])
        wuq = wuq.reshape(Q_LORA_RANK, MLA_HEADS * HEAD_PAD).astype(BF16)
        wukv = w_ukv[layer].reshape(KV_LORA_RANK, MLA_HEADS, QK_NOPE_DIM + V_HEAD_DIM)
        wk = _qk_head_layout(wukv[..., :QK_NOPE_DIM], None).reshape(KV_LORA_RANK, -1)
        wv = _pad_heads(wukv[:, :, QK_NOPE_DIM:].reshape(KV_LORA_RANK, -1), MLA_HEADS, V_HEAD_DIM)
        wkv = jnp.concatenate([wk, wv], axis=1).astype(BF16)
        gq = q_norm_g[layer].astype(F32) * q_scale
        gk = k_norm_g[layer].astype(F32)

        q, k, v, hq, hf, hi, hg, gates = _input_stage(
            x2d, pos3d, b, s, _row(mix_norm_g[layer]), wall, _row(q_a_norm_g[layer]), wuq,
            _row(kv_a_norm_g[layer]), wkv,
            _row(_qk_head_layout(gq[:QK_NOPE_DIM], gq[QK_NOPE_DIM:])),
            _row(_qk_head_layout(gk[:QK_NOPE_DIM], gk[QK_NOPE_DIM:])))

        attn = _attention_stage(q, k, v)
        rec = _hgrn_stage(hq.reshape(b, s, HG_WIDTH), hf.reshape(b, s, HG_WIDTH),
                          hi.reshape(b, s, HG_WIDTH), hg.reshape(b, s, HG_WIDTH),
                          _row(lower_bounds[layer]), _row(hg_out_norm_g[layer]))

        x2d = _output_stage(
            x2d, attn.reshape(b * s, BRANCH_WIDTH), rec.reshape(b * s, HG_WIDTH), gates,
            p[layer].reshape(b * s, PLE_DIM),
            w_branch[layer, 0].astype(BF16), w_branch[layer, 1].astype(BF16),
            w_out[layer].astype(BF16), _row(ffn_norm_g[layer]),
            w_ffn_gate[layer].astype(BF16), w_ffn_up[layer].astype(BF16),
            w_ffn_down[layer].astype(BF16), _row(ple_gate_norm_g[layer]),
            w_ple_gate[layer].astype(BF16), w_ple_proj[layer].astype(BF16),
            _row(ple_post_norm_g[layer]))
    return x2d.reshape(b, s, d)
```

```python
import functools
import math

import numpy as np

import jax
import jax.numpy as jnp
from jax import lax
from jax.experimental import pallas as pl
from jax.experimental.pallas import tpu as pltpu

D_MODEL = 1024
MLA_HEADS = 8
QK_NOPE_DIM = 64
QK_ROPE_DIM = 32
ROPE_HALF = QK_ROPE_DIM // 2
QK_HEAD_DIM = QK_NOPE_DIM + QK_ROPE_DIM
V_HEAD_DIM = 64
Q_LORA_RANK = 384
KV_LORA_RANK = 256
ROPE_BASE = 10000.0
HG_HEADS = 4
HG_DIM = 128
HG_WIDTH = HG_HEADS * HG_DIM
BRANCH_WIDTH = MLA_HEADS * V_HEAD_DIM
FFN_HIDDEN = 2816
PLE_DIM = 256
EPS = 1e-6

LANES = 128
HEAD_PAD = LANES
HEAD_HALF_NOPE = QK_NOPE_DIM // 2
W_LAT = Q_LORA_RANK + KV_LORA_RANK + HEAD_PAD
DENOM_LANE = V_HEAD_DIM

VMEM_LIMIT = 56 * 1024 * 1024

TOK_TILE = 512
ATT_TILE = 512
ATT_HEADS = 4
HG_BLOCK = 512
HG_CHUNK = 64
HG_LEVELS = tuple(HG_CHUNK >> (i + 1) for i in range(HG_CHUNK.bit_length() - 1))
FFN_SPLITS = (1024, 1024, 768)

BF16 = jnp.bfloat16
F32 = jnp.float32


def _const_spec(shape):
    nd = len(shape)
    return pl.BlockSpec(shape, lambda *_: (0,) * nd, pipeline_mode=pl.Buffered(1))


def _rms(v, gain, width=None):
    n = v.shape[-1] if width is None else width
    ms = jnp.sum(v * v, axis=-1, keepdims=True) * (1.0 / n)
    return v * lax.rsqrt(ms + EPS) * gain


def _dot(a, b):
    return jnp.dot(a, b, preferred_element_type=F32)


def _dot_nt(a, b):
    return lax.dot_general(a, b, (((1,), (1,)), ((), ())), preferred_element_type=F32)


def _rope_tables(pos_row):
    tm = pos_row.shape[1]
    fidx = lax.broadcasted_iota(jnp.int32, (ROPE_HALF, 1), 0).astype(F32)
    inv_freq = jnp.exp(fidx * (-math.log(ROPE_BASE) * 2.0 / QK_ROPE_DIM))
    ang = pos_row.astype(F32) * inv_freq
    cos = jnp.cos(ang)
    sin = jnp.sin(ang)
    ones = jnp.ones((HEAD_HALF_NOPE, tm), F32)
    zeros_n = jnp.zeros((HEAD_HALF_NOPE, tm), F32)
    zeros_p = jnp.zeros((ROPE_HALF, tm), F32)
    cos_t = jnp.concatenate([ones, cos, zeros_p, ones, cos, zeros_p], axis=0)
    sin_t = jnp.concatenate([zeros_n, -sin, zeros_p, zeros_n, sin, zeros_p], axis=0)
    return cos_t.T, sin_t.T


def _input_kernel(x_ref, pos_ref, gmix_ref, wall_ref, gqa_ref, wuq_ref, gkva_ref, wkv_ref,
                  gq_ref, gk_ref,
                  q_ref, k_ref, v_ref, hq_ref, hf_ref, hi_ref, hg_ref, gates_ref):
    x = x_ref[...]
    h = _rms(x, gmix_ref[...]).astype(BF16)

    lat = _dot(h, wall_ref[:, :W_LAT])
    cq = _rms(lat[:, :Q_LORA_RANK], gqa_ref[...]).astype(BF16)
    ckv = _rms(lat[:, Q_LORA_RANK:Q_LORA_RANK + KV_LORA_RANK], gkva_ref[...]).astype(BF16)
    k_rope = lat[:, Q_LORA_RANK + KV_LORA_RANK:]

    cos_tab, sin_tab = _rope_tables(pos_ref[0])
    gq = gq_ref[...]
    gk = gk_ref[...]
    q_cos = cos_tab * gq
    q_sin = sin_tab * pltpu.roll(gq, LANES // 2, 1)
    kr = k_rope * gk
    kr = kr * cos_tab + pltpu.roll(kr, LANES // 2, 1) * sin_tab
    kr_ss = jnp.sum(k_rope * k_rope, axis=-1, keepdims=True)

    q_all = _dot(cq, wuq_ref[...])
    kv_all = _dot(ckv, wkv_ref[...])
    lane = lax.broadcasted_iota(jnp.int32, (1, LANES), 1)
    one_lane = (lane == DENOM_LANE).astype(F32)
    inv_width = 1.0 / QK_HEAD_DIM
    for hd in range(MLA_HEADS):
        sl = slice(hd * HEAD_PAD, (hd + 1) * HEAD_PAD)
        t = q_all[:, sl]
        r = lax.rsqrt(jnp.sum(t * t, axis=-1, keepdims=True) * inv_width + EPS)
        q_ref[0, hd] = ((t * q_cos + pltpu.roll(t, LANES // 2, 1) * q_sin) * r).astype(BF16)
        t = kv_all[:, sl]
        r = lax.rsqrt((jnp.sum(t * t, axis=-1, keepdims=True) + kr_ss) * inv_width + EPS)
        k_ref[0, hd] = ((t * gk + kr) * r).astype(BF16)
        vsl = slice(MLA_HEADS * HEAD_PAD + hd * HEAD_PAD, MLA_HEADS * HEAD_PAD + (hd + 1) * HEAD_PAD)
        v_ref[0, hd] = (kv_all[:, vsl] + one_lane).astype(BF16)

    hh = _dot(h, wall_ref[:, W_LAT:W_LAT + 4 * HG_WIDTH])
    hq_ref[...] = hh[:, :HG_WIDTH].astype(BF16)
    hf_ref[...] = hh[:, HG_WIDTH:2 * HG_WIDTH]
    hi_ref[...] = hh[:, 2 * HG_WIDTH:3 * HG_WIDTH].astype(BF16)
    hg_ref[...] = hh[:, 3 * HG_WIDTH:].astype(BF16)

    gates_ref[...] = jax.nn.sigmoid(_dot(h, wall_ref[:, W_LAT + 4 * HG_WIDTH:])).astype(BF16)


W_PREP_COLS = 256


def _input_weight_kernel(wt_ref, o_ref):
    c0 = Q_LORA_RANK + KV_LORA_RANK
    c1 = c0 + QK_ROPE_DIM
    j = pl.program_id(0)
    n = W_PREP_COLS
    lat_full = c0 // n

    @pl.when(j < lat_full)
    def _():
        start = pl.multiple_of(j * n, n)
        o_ref[...] = wt_ref[pl.ds(start, n), :].T.astype(BF16)

    @pl.when(j == lat_full)
    def _():
        d = wt_ref.shape[1]
        z_n = jnp.zeros((HEAD_HALF_NOPE, d), F32)
        z_p = jnp.zeros((ROPE_HALF, d), F32)
        blk = jnp.concatenate([wt_ref[lat_full * n:c0, :], z_n, wt_ref[c0:c0 + ROPE_HALF, :], z_p,
                               z_n, wt_ref[c0 + ROPE_HALF:c1, :], z_p], axis=0)
        o_ref[...] = blk.T.astype(BF16)

    @pl.when(j > lat_full)
    def _():
        start = pl.multiple_of(c1 + (j - lat_full - 1) * n, 2 * ROPE_HALF)
        o_ref[...] = wt_ref[pl.ds(start, n), :].T.astype(BF16)


def _input_weight_layout(wt):
    cols, rows = wt.shape
    out_cols = cols - QK_ROPE_DIM + HEAD_PAD
    return pl.pallas_call(
        _input_weight_kernel,
        grid=(out_cols // W_PREP_COLS,),
        in_specs=[_const_spec(wt.shape)],
        out_specs=pl.BlockSpec((rows, W_PREP_COLS), lambda j: (0, j)),
        out_shape=jax.ShapeDtypeStruct((rows, out_cols), BF16),
        compiler_params=pltpu.CompilerParams(
            dimension_semantics=("arbitrary",), vmem_limit_bytes=VMEM_LIMIT),
        name="input_weight_layout",
    )(wt)


def _input_stage(x2d, pos3d, b, s, gmix, wall, gqa, wuq, gkva, wkv, gq, gk):
    t = x2d.shape[0]
    tm = TOK_TILE
    nt = s // tm

    def tok(width):
        return pl.BlockSpec((tm, width), lambda i: (i, 0))

    def head_spec():
        return pl.BlockSpec((1, MLA_HEADS, tm, HEAD_PAD), lambda i: (i // nt, 0, i % nt, 0))

    head_shape = jax.ShapeDtypeStruct((b, MLA_HEADS, s, HEAD_PAD), BF16)
    consts = (gmix, wall, gqa, wuq, gkva, wkv, gq, gk)
    return pl.pallas_call(
        _input_kernel,
        grid=(t // tm,),
        in_specs=[tok(D_MODEL), pl.BlockSpec((1, 1, tm), lambda i: (i, 0, 0))]
        + [_const_spec(c.shape) for c in consts],
        out_specs=[head_spec(), head_spec(), head_spec(),
                   tok(HG_WIDTH), tok(HG_WIDTH), tok(HG_WIDTH), tok(HG_WIDTH), tok(2 * D_MODEL)],
        out_shape=[head_shape, head_shape, head_shape,
                   jax.ShapeDtypeStruct((t, HG_WIDTH), BF16),
                   jax.ShapeDtypeStruct((t, HG_WIDTH), F32),
                   jax.ShapeDtypeStruct((t, HG_WIDTH), BF16),
                   jax.ShapeDtypeStruct((t, HG_WIDTH), BF16),
                   jax.ShapeDtypeStruct((t, 2 * D_MODEL), BF16)],
        compiler_params=pltpu.CompilerParams(
            dimension_semantics=("arbitrary",), vmem_limit_bytes=VMEM_LIMIT),
        name="input_stage",
    )(x2d, pos3d, *consts)


def _attention_kernel(q_ref, k_ref, v_ref, o_ref, m_ref, acc_ref, sa_ref, sb_ref):
    qi = pl.program_id(2)
    tq = ATT_TILE
    row = lax.broadcasted_iota(jnp.int32, (tq, tq), 0)
    col = lax.broadcasted_iota(jnp.int32, (tq, tq), 1)
    m_ref[...] = jnp.full(m_ref.shape, -jnp.inf, F32)
    acc_ref[...] = jnp.zeros(acc_ref.shape, F32)

    def scores(j, s_ref):
        start = pl.multiple_of(j * tq, tq)
        for hd in range(ATT_HEADS):
            s_ref[hd] = _dot_nt(q_ref[0, hd], k_ref[0, hd, pl.ds(start, tq), :])

    def accumulate(j, s_ref, masked):
        for hd in range(ATT_HEADS):
            accumulate_head(j, hd, s_ref[hd], masked)

    def accumulate_head(j, hd, sc, masked):
        start = pl.multiple_of(j * tq, tq)
        if masked:
            sc = jnp.where(col <= row, sc, -jnp.inf)
        m_old = m_ref[hd]
        m_new = jnp.maximum(m_old, jnp.max(sc, axis=-1, keepdims=True))
        p = jnp.exp2(sc - jnp.concatenate([m_new] * (tq // LANES), axis=1)).astype(BF16)
        acc_ref[hd] = (jnp.exp2(m_old - m_new) * acc_ref[hd]
                       + _dot(p, v_ref[0, hd, pl.ds(start, tq), :]))
        m_ref[hd] = m_new

    scores(0, sa_ref)

    def body(t, carry):
        j = 2 * t
        scores(j + 1, sb_ref)
        accumulate(j, sa_ref, False)
        scores(j + 2, sa_ref)
        accumulate(j + 1, sb_ref, False)
        return carry

    lax.fori_loop(0, qi // 2, body, 0)

    @pl.when(qi % 2 == 0)
    def _():
        accumulate(qi, sa_ref, True)

    @pl.when(qi % 2 == 1)
    def _():
        scores(qi, sb_ref)
        accumulate(qi - 1, sa_ref, False)
        accumulate(qi, sb_ref, True)

    outs = []
    for hd in range(ATT_HEADS):
        acc = acc_ref[hd]
        outs.append(acc[:, :V_HEAD_DIM] / acc[:, DENOM_LANE:DENOM_LANE + 1])
    o_ref[0] = jnp.concatenate(outs, axis=-1).astype(BF16)


def _attention_stage(q, k, v):
    b, nh, s, _ = q.shape
    tq = ATT_TILE
    g = ATT_HEADS
    return pl.pallas_call(
        _attention_kernel,
        grid=(b, nh // g, s // tq),
        in_specs=[pl.BlockSpec((1, g, tq, HEAD_PAD), lambda bi, hp, i: (bi, hp, i, 0)),
                  pl.BlockSpec((1, g, s, HEAD_PAD), lambda bi, hp, i: (bi, hp, 0, 0)),
                  pl.BlockSpec((1, g, s, HEAD_PAD), lambda bi, hp, i: (bi, hp, 0, 0))],
        out_specs=pl.BlockSpec((1, tq, g * V_HEAD_DIM), lambda bi, hp, i: (bi, i, hp)),
        out_shape=jax.ShapeDtypeStruct((b, s, BRANCH_WIDTH), BF16),
        scratch_shapes=[pltpu.VMEM((g, tq, LANES), F32), pltpu.VMEM((g, tq, HEAD_PAD), F32),
                        pltpu.VMEM((g, tq, tq), F32), pltpu.VMEM((g, tq, tq), F32)],
        compiler_params=pltpu.CompilerParams(
            dimension_semantics=("arbitrary", "arbitrary", "arbitrary"),
            vmem_limit_bytes=VMEM_LIMIT),
        name="causal_attention",
    )(q, k, v)


def _hgrn_span_matrix():
    c = HG_CHUNK
    x = np.arange(c)[:, None]
    y = np.arange(c)[None, :]
    mats = [(y <= x)]
    for b in HG_LEVELS:
        r = (x // (2 * b)) * (2 * b) + b
        mats.append((y > np.minimum(x, r)) & (y <= np.maximum(x, r)))
    m = np.concatenate(mats, axis=0).astype(np.float32)
    return np.concatenate([m, m], axis=1)


def _hgrn_chunk_local(q, hf, v_bf, lb, span, right_rows, pair_masks):
    c = HG_CHUNK
    f = lb + (1.0 - lb) * jax.nn.sigmoid(hf)
    lf = jnp.log2(f)
    kk = 1.0 - f
    lf_hi = lf.astype(BF16)
    lf_lo = (lf - lf_hi.astype(F32)).astype(BF16)
    spans = _dot(span, jnp.concatenate([lf_hi, lf_lo], axis=0))
    cum = spans[:c]
    last = cum[c - 1:c, :]
    q_dec = (q * jnp.exp2(cum)).astype(BF16)
    k_dec = (kk * jnp.exp2(last - cum)).astype(BF16)
    q_bf = q.astype(BF16)
    k_bf = kk.astype(BF16)
    zs = [(jnp.where(right_rows[lvl], q, kk)
           * jnp.exp2(spans[(lvl + 1) * c:(lvl + 2) * c])).astype(BF16)
          for lvl in range(len(HG_LEVELS))]

    o_intra, incr = [], []
    for hd in range(HG_HEADS):
        sl = slice(hd * HG_DIM, (hd + 1) * HG_DIM)
        att = jnp.where(pair_masks[0], _dot_nt(q_bf[:, sl], k_bf[:, sl]), 0.0)
        for lvl in range(len(HG_LEVELS)):
            z = zs[lvl][:, sl]
            att = jnp.where(pair_masks[lvl + 1], _dot_nt(z, z), att)
        v_h = v_bf[:, sl]
        o_intra.append(_dot(att.astype(BF16), v_h))
        incr.append(_dot(v_h.astype(F32).T.astype(BF16), k_dec[:, sl]))
    return o_intra, q_dec, incr, jnp.exp2(last)


def _hgrn_kernel(hq_ref, hf_ref, hi_ref, hg_ref, lb_ref, gout_ref, span_ref, o_ref, state_ref):
    @pl.when(pl.program_id(1) == 0)
    def _():
        state_ref[...] = jnp.zeros(state_ref.shape, F32)

    c = HG_CHUNK
    t_i = lax.broadcasted_iota(jnp.int32, (c, c), 0)
    s_i = lax.broadcasted_iota(jnp.int32, (c, c), 1)
    row = lax.broadcasted_iota(jnp.int32, (c, HG_WIDTH), 0)
    pair_masks = [t_i == s_i] + [((t_i // b) ^ (s_i // b) == 1) & (s_i < t_i) for b in HG_LEVELS]
    right_rows = [(row // b) % 2 == 1 for b in HG_LEVELS]
    span = span_ref[...]
    lb = lb_ref[...]
    n_chunks = HG_BLOCK // c

    local = []
    for ci in range(n_chunks):
        rows = slice(ci * c, (ci + 1) * c)
        local.append(_hgrn_chunk_local(hq_ref[0, rows, :].astype(F32), hf_ref[0, rows, :],
                                       hi_ref[0, rows, :], lb, span, right_rows, pair_masks))

    gout = gout_ref[...]
    for hd in range(HG_HEADS):
        sl = slice(hd * HG_DIM, (hd + 1) * HG_DIM)
        state_t = state_ref[hd]
        for ci in range(n_chunks):
            rows = slice(ci * c, (ci + 1) * c)
            o_intra, q_dec, incr, decay = local[ci]
            o = o_intra[hd] + _dot_nt(q_dec[:, sl], state_t.astype(BF16))
            state_t = state_t * decay[:, sl] + incr[hd]
            gate = hg_ref[0, rows, sl].astype(F32)
            o_ref[0, rows, sl] = (_rms(o, gout) * (gate * jax.nn.sigmoid(gate))).astype(BF16)
        state_ref[hd] = state_t


def _hgrn_stage(hq, hf, hi, hg, lb, gout):
    span = jnp.asarray(_hgrn_span_matrix(), BF16)
    b, s, _ = hq.shape
    tb = HG_BLOCK
    tok = pl.BlockSpec((1, tb, HG_WIDTH), lambda bi, i: (bi, i, 0))
    return pl.pallas_call(
        _hgrn_kernel,
        grid=(b, s // tb),
        in_specs=[tok, tok, tok, tok, _const_spec(lb.shape), _const_spec(gout.shape),
                  _const_spec(span.shape)],
        out_specs=tok,
        out_shape=jax.ShapeDtypeStruct((b, s, HG_WIDTH), BF16),
        scratch_shapes=[pltpu.VMEM((HG_HEADS, HG_DIM, HG_DIM), F32)],
        compiler_params=pltpu.CompilerParams(
            dimension_semantics=("arbitrary", "arbitrary"), vmem_limit_bytes=VMEM_LIMIT),
        name="hgrn2_recurrence",
    )(hq, hf, hi, hg, lb, gout, span)


def _output_kernel(x_ref, attn_ref, rec_ref, gates_ref, p_ref,
                   wb0_ref, wb1_ref, wout_ref, gffn_ref, wgate_ref, wup_ref, wdown_ref,
                   gpg_ref, wpg_ref, wpp_ref, gpost_ref, o_ref):
    y0 = _dot(attn_ref[...], wb0_ref[...])
    y1 = _dot(rec_ref[...], wb1_ref[...])
    g0 = gates_ref[:, :D_MODEL].astype(F32)
    g1 = gates_ref[:, D_MODEL:].astype(F32)
    merged = (g0 * y0 + g1 * y1).astype(BF16)
    x1 = x_ref[...] + _dot(merged, wout_ref[...])

    h2 = _rms(x1, gffn_ref[...]).astype(BF16)
    x2 = x1
    lo = 0
    for width in FFN_SPLITS:
        a = _dot(h2, wgate_ref[:, lo:lo + width])
        u = _dot(h2, wup_ref[:, lo:lo + width])
        z = (a * jax.nn.sigmoid(a) * u).astype(BF16)
        x2 = x2 + _dot(z, wdown_ref[lo:lo + width, :])
        lo += width

    e = _rms(_dot(p_ref[...].astype(BF16), wpp_ref[...]), gpost_ref[...])
    g = jax.nn.sigmoid(_dot(_rms(x2, gpg_ref[...]).astype(BF16), wpg_ref[...]))
    o_ref[...] = x2 + g * e


def _output_stage(x2d, attn, rec, gates, p2d, wb0, wb1, wout, gffn, wgate, wup, wdown,
                  gpg, wpg, wpp, gpost):
    t = x2d.shape[0]
    tm = TOK_TILE

    def tok(width):
        return pl.BlockSpec((tm, width), lambda i: (i, 0))

    consts = (wb0, wb1, wout, gffn, wgate, wup, wdown, gpg, wpg, wpp, gpost)
    return pl.pallas_call(
        _output_kernel,
        grid=(t // tm,),
        in_specs=[tok(D_MODEL), tok(BRANCH_WIDTH), tok(HG_WIDTH), tok(2 * D_MODEL), tok(PLE_DIM)]
        + [_const_spec(c.shape) for c in consts],
        out_specs=tok(D_MODEL),
        out_shape=jax.ShapeDtypeStruct((t, D_MODEL), F32),
        compiler_params=pltpu.CompilerParams(
            dimension_semantics=("arbitrary",), vmem_limit_bytes=VMEM_LIMIT),
        name="output_stage",
    )(x2d, attn, rec, gates, p2d, *consts)


def _pad_heads(w, heads, width):
    rows = w.shape[0]
    w = w.reshape(rows, heads, width)
    w = jnp.pad(w, ((0, 0), (0, 0), (0, HEAD_PAD - width)))
    return w.reshape(rows, heads * HEAD_PAD)


def _qk_head_layout(nope, rope):
    like = nope if nope is not None else rope

    def z(width):
        return jnp.zeros(like.shape[:-1] + (width,), like.dtype)

    n0 = nope[..., :HEAD_HALF_NOPE] if nope is not None else z(HEAD_HALF_NOPE)
    n1 = nope[..., HEAD_HALF_NOPE:] if nope is not None else z(HEAD_HALF_NOPE)
    x1 = rope[..., :ROPE_HALF] if rope is not None else z(ROPE_HALF)
    x2 = rope[..., ROPE_HALF:] if rope is not None else z(ROPE_HALF)
    return jnp.concatenate([n0, x1, z(ROPE_HALF), n1, x2, z(ROPE_HALF)], axis=-1)


def _row(v):
    return v.reshape(1, -1).astype(F32)


def kernel(x, p, positions, mix_norm_g, w_in, q_a_norm_g, w_uq, kv_a_norm_g, w_ukv, q_norm_g, k_norm_g, hg_lb_logits, hg_out_norm_g, w_branch, w_out, ffn_norm_g, w_ffn_gate, w_ffn_up, w_ffn_down, ple_gate_norm_g, w_ple_gate, w_ple_proj, ple_post_norm_g):
    b, s, d = x.shape
    depth = w_in.shape[0]
    lower_bounds = jnp.cumsum(jax.nn.softmax(hg_lb_logits.astype(F32), axis=0), axis=0)
    pos3d = positions.reshape(b * s // TOK_TILE, 1, TOK_TILE)
    x2d = x.reshape(b * s, d)
    q_scale = QK_HEAD_DIM ** -0.5 * math.log2(math.e)

    for layer in range(depth):
        wi = w_in[layer]
        wall = _input_weight_layout(wi.T)
        wq3 = w_uq[layer].reshape(Q_LORA_RANK, MLA_HEADS, QK_HEAD_DIM)
        wuq = _qk_head_layout(wq3[..., :QK_NOPE_DIM], wq3[..., QK_NOPE_DIM:])
        wuq = wuq.reshape(Q_LORA_RANK, MLA_HEADS * HEAD_PAD).astype(BF16)
        wukv = w_ukv[layer].reshape(KV_LORA_RANK, MLA_HEADS, QK_NOPE_DIM + V_HEAD_DIM)
        wk = _qk_head_layout(wukv[..., :QK_NOPE_DIM], None).reshape(KV_LORA_RANK, -1)
        wv = _pad_heads(wukv[:, :, QK_NOPE_DIM:].reshape(KV_LORA_RANK, -1), MLA_HEADS, V_HEAD_DIM)
        wkv = jnp.concatenate([wk, wv], axis=1).astype(BF16)
        gq = q_norm_g[layer].astype(F32) * q_scale
        gk = k_norm_g[layer].astype(F32)

        q, k, v, hq, hf, hi, hg, gates = _input_stage(
            x2d, pos3d, b, s, _row(mix_norm_g[layer]), wall, _row(q_a_norm_g[layer]), wuq,
            _row(kv_a_norm_g[layer]), wkv,
            _row(_qk_head_layout(gq[:QK_NOPE_DIM], gq[QK_NOPE_DIM:])),
            _row(_qk_head_layout(gk[:QK_NOPE_DIM], gk[QK_NOPE_DIM:])))

        attn = _attention_stage(q, k, v)
        rec = _hgrn_stage(hq.reshape(b, s, HG_WIDTH), hf.reshape(b, s, HG_WIDTH),
                          hi.reshape(b, s, HG_WIDTH), hg.reshape(b, s, HG_WIDTH),
                          _row(lower_bounds[layer]), _row(hg_out_norm_g[layer]))

        x2d = _output_stage(
            x2d, attn.reshape(b * s, BRANCH_WIDTH), rec.reshape(b * s, HG_WIDTH), gates,
            p[layer].reshape(b * s, PLE_DIM),
            w_branch[layer, 0].astype(BF16), w_branch[layer, 1].astype(BF16),
            w_out[layer].astype(BF16), _row(ffn_norm_g[layer]),
            w_ffn_gate[layer].astype(BF16), w_ffn_up[layer].astype(BF16),
            w_ffn_down[layer].astype(BF16), _row(ple_gate_norm_g[layer]),
            w_ple_gate[layer].astype(BF16), w_ple_proj[layer].astype(BF16),
            _row(ple_post_norm_g[layer]))
    return x2d.reshape(b, s, d)
```

```python
import functools
import math

import numpy as np

import jax
import jax.numpy as jnp
from jax import lax
from jax.experimental import pallas as pl
from jax.experimental.pallas import tpu as pltpu

D_MODEL = 1024
MLA_HEADS = 8
QK_NOPE_DIM = 64
QK_ROPE_DIM = 32
ROPE_HALF = QK_ROPE_DIM // 2
QK_HEAD_DIM = QK_NOPE_DIM + QK_ROPE_DIM
V_HEAD_DIM = 64
Q_LORA_RANK = 384
KV_LORA_RANK = 256
ROPE_BASE = 10000.0
HG_HEADS = 4
HG_DIM = 128
HG_WIDTH = HG_HEADS * HG_DIM
BRANCH_WIDTH = MLA_HEADS * V_HEAD_DIM
FFN_HIDDEN = 2816
PLE_DIM = 256
EPS = 1e-6

LANES = 128
HEAD_PAD = LANES
HEAD_HALF_NOPE = QK_NOPE_DIM // 2
W_LAT = Q_LORA_RANK + KV_LORA_RANK + HEAD_PAD
DENOM_LANE = V_HEAD_DIM

VMEM_LIMIT = 60 * 1024 * 1024

TOK_TILE = 512
ATT_TILE = 512
ATT_HEADS = 8
HG_BLOCK = 512
HG_CHUNK = 64
HG_LEVELS = tuple(HG_CHUNK >> (i + 1) for i in range(HG_CHUNK.bit_length() - 1))
FFN_SPLITS = (1024, 1024, 768)

BF16 = jnp.bfloat16
F32 = jnp.float32


def _const_spec(shape):
    nd = len(shape)
    return pl.BlockSpec(shape, lambda *_: (0,) * nd, pipeline_mode=pl.Buffered(1))


def _rms(v, gain, width=None):
    n = v.shape[-1] if width is None else width
    ms = jnp.sum(v * v, axis=-1, keepdims=True) * (1.0 / n)
    return v * lax.rsqrt(ms + EPS) * gain


def _dot(a, b):
    return jnp.dot(a, b, preferred_element_type=F32)


def _dot_nt(a, b):
    return lax.dot_general(a, b, (((1,), (1,)), ((), ())), preferred_element_type=F32)


def _rope_tables(pos_row):
    tm = pos_row.shape[1]
    fidx = lax.broadcasted_iota(jnp.int32, (ROPE_HALF, 1), 0).astype(F32)
    inv_freq = jnp.exp(fidx * (-math.log(ROPE_BASE) * 2.0 / QK_ROPE_DIM))
    ang = pos_row.astype(F32) * inv_freq
    cos = jnp.cos(ang)
    sin = jnp.sin(ang)
    ones = jnp.ones((HEAD_HALF_NOPE, tm), F32)
    zeros_n = jnp.zeros((HEAD_HALF_NOPE, tm), F32)
    zeros_p = jnp.zeros((ROPE_HALF, tm), F32)
    cos_t = jnp.concatenate([ones, cos, zeros_p, ones, cos, zeros_p], axis=0)
    sin_t = jnp.concatenate([zeros_n, -sin, zeros_p, zeros_n, sin, zeros_p], axis=0)
    return cos_t.T, sin_t.T


def _input_kernel(x_ref, pos_ref, gmix_ref, wall_ref, gqa_ref, wuq_ref, gkva_ref, wkv_ref,
                  gq_ref, gk_ref,
                  q_ref, k_ref, v_ref, hq_ref, hf_ref, hi_ref, hg_ref, gates_ref):
    x = x_ref[...]
    h = _rms(x, gmix_ref[...]).astype(BF16)

    lat = _dot(h, wall_ref[:, :W_LAT])
    cq = _rms(lat[:, :Q_LORA_RANK], gqa_ref[...]).astype(BF16)
    ckv = _rms(lat[:, Q_LORA_RANK:Q_LORA_RANK + KV_LORA_RANK], gkva_ref[...]).astype(BF16)
    k_rope = lat[:, Q_LORA_RANK + KV_LORA_RANK:]

    cos_tab, sin_tab = _rope_tables(pos_ref[0])
    gq = gq_ref[...]
    gk = gk_ref[...]
    q_cos = cos_tab * gq
    q_sin = sin_tab * pltpu.roll(gq, LANES // 2, 1)
    kr = k_rope * gk
    kr = kr * cos_tab + pltpu.roll(kr, LANES // 2, 1) * sin_tab
    kr_ss = jnp.sum(k_rope * k_rope, axis=-1, keepdims=True)

    q_all = _dot(cq, wuq_ref[...])
    kv_all = _dot(ckv, wkv_ref[...])
    lane = lax.broadcasted_iota(jnp.int32, (1, LANES), 1)
    one_lane = (lane == DENOM_LANE).astype(F32)
    inv_width = 1.0 / QK_HEAD_DIM
    for hd in range(MLA_HEADS):
        sl = slice(hd * HEAD_PAD, (hd + 1) * HEAD_PAD)
        t = q_all[:, sl]
        r = lax.rsqrt(jnp.sum(t * t, axis=-1, keepdims=True) * inv_width + EPS)
        q_ref[0, hd] = ((t * q_cos + pltpu.roll(t, LANES // 2, 1) * q_sin) * r).astype(BF16)
        t = kv_all[:, sl]
        r = lax.rsqrt((jnp.sum(t * t, axis=-1, keepdims=True) + kr_ss) * inv_width + EPS)
        k_ref[0, hd] = ((t * gk + kr) * r).astype(BF16)
        vsl = slice(MLA_HEADS * HEAD_PAD + hd * HEAD_PAD, MLA_HEADS * HEAD_PAD + (hd + 1) * HEAD_PAD)
        v_ref[0, hd] = (kv_all[:, vsl] + one_lane).astype(BF16)

    hh = _dot(h, wall_ref[:, W_LAT:W_LAT + 4 * HG_WIDTH])
    hq_ref[...] = hh[:, :HG_WIDTH].astype(BF16)
    hf_ref[...] = hh[:, HG_WIDTH:2 * HG_WIDTH]
    hi_ref[...] = hh[:, 2 * HG_WIDTH:3 * HG_WIDTH].astype(BF16)
    hg_ref[...] = hh[:, 3 * HG_WIDTH:].astype(BF16)

    gates_ref[...] = jax.nn.sigmoid(_dot(h, wall_ref[:, W_LAT + 4 * HG_WIDTH:])).astype(BF16)


W_PREP_COLS = 256


def _input_weight_kernel(wt_ref, o_ref):
    c0 = Q_LORA_RANK + KV_LORA_RANK
    c1 = c0 + QK_ROPE_DIM
    j = pl.program_id(0)
    n = W_PREP_COLS
    lat_full = c0 // n

    @pl.when(j < lat_full)
    def _():
        start = pl.multiple_of(j * n, n)
        o_ref[...] = wt_ref[pl.ds(start, n), :].T.astype(BF16)

    @pl.when(j == lat_full)
    def _():
        d = wt_ref.shape[1]
        z_n = jnp.zeros((HEAD_HALF_NOPE, d), F32)
        z_p = jnp.zeros((ROPE_HALF, d), F32)
        blk = jnp.concatenate([wt_ref[lat_full * n:c0, :], z_n, wt_ref[c0:c0 + ROPE_HALF, :], z_p,
                               z_n, wt_ref[c0 + ROPE_HALF:c1, :], z_p], axis=0)
        o_ref[...] = blk.T.astype(BF16)

    @pl.when(j > lat_full)
    def _():
        start = pl.multiple_of(c1 + (j - lat_full - 1) * n, 2 * ROPE_HALF)
        o_ref[...] = wt_ref[pl.ds(start, n), :].T.astype(BF16)


def _input_weight_layout(wt):
    cols, rows = wt.shape
    out_cols = cols - QK_ROPE_DIM + HEAD_PAD
    return pl.pallas_call(
        _input_weight_kernel,
        grid=(out_cols // W_PREP_COLS,),
        in_specs=[_const_spec(wt.shape)],
        out_specs=pl.BlockSpec((rows, W_PREP_COLS), lambda j: (0, j)),
        out_shape=jax.ShapeDtypeStruct((rows, out_cols), BF16),
        compiler_params=pltpu.CompilerParams(
            dimension_semantics=("arbitrary",), vmem_limit_bytes=VMEM_LIMIT),
        name="input_weight_layout",
    )(wt)


def _input_stage(x2d, pos3d, b, s, gmix, wall, gqa, wuq, gkva, wkv, gq, gk):
    t = x2d.shape[0]
    tm = TOK_TILE
    nt = s // tm

    def tok(width):
        return pl.BlockSpec((tm, width), lambda i: (i, 0))

    def head_spec():
        return pl.BlockSpec((1, MLA_HEADS, tm, HEAD_PAD), lambda i: (i // nt, 0, i % nt, 0))

    head_shape = jax.ShapeDtypeStruct((b, MLA_HEADS, s, HEAD_PAD), BF16)
    consts = (gmix, wall, gqa, wuq, gkva, wkv, gq, gk)
    return pl.pallas_call(
        _input_kernel,
        grid=(t // tm,),
        in_specs=[tok(D_MODEL), pl.BlockSpec((1, 1, tm), lambda i: (i, 0, 0))]
        + [_const_spec(c.shape) for c in consts],
        out_specs=[head_spec(), head_spec(), head_spec(),
                   tok(HG_WIDTH), tok(HG_WIDTH), tok(HG_WIDTH), tok(HG_WIDTH), tok(2 * D_MODEL)],
        out_shape=[head_shape, head_shape, head_shape,
                   jax.ShapeDtypeStruct((t, HG_WIDTH), BF16),
                   jax.ShapeDtypeStruct((t, HG_WIDTH), F32),
                   jax.ShapeDtypeStruct((t, HG_WIDTH), BF16),
                   jax.ShapeDtypeStruct((t, HG_WIDTH), BF16),
                   jax.ShapeDtypeStruct((t, 2 * D_MODEL), BF16)],
        compiler_params=pltpu.CompilerParams(
            dimension_semantics=("arbitrary",), vmem_limit_bytes=VMEM_LIMIT),
        name="input_stage",
    )(x2d, pos3d, *consts)


def _attention_kernel(q_ref, k_ref, v_ref, o_ref, m_ref, acc_ref, s_ref):
    qi = pl.program_id(2)
    tq = ATT_TILE
    row = lax.broadcasted_iota(jnp.int32, (tq, tq), 0)
    col = lax.broadcasted_iota(jnp.int32, (tq, tq), 1)
    m_ref[...] = jnp.full(m_ref.shape, -jnp.inf, F32)
    acc_ref[...] = jnp.zeros(acc_ref.shape, F32)

    def block(j, masked):
        start = pl.multiple_of(j * tq, tq)
        for hd in range(ATT_HEADS):
            s_ref[hd] = _dot_nt(q_ref[0, hd], k_ref[0, hd, pl.ds(start, tq), :])
        for hd in range(ATT_HEADS):
            sc = s_ref[hd]
            if masked:
                sc = jnp.where(col <= row, sc, -jnp.inf)
            m_old = m_ref[hd]
            m_new = jnp.maximum(m_old, jnp.max(sc, axis=-1, keepdims=True))
            p = jnp.exp2(sc - jnp.concatenate([m_new] * (tq // LANES), axis=1)).astype(BF16)
            acc_ref[hd] = (jnp.exp2(m_old - m_new) * acc_ref[hd]
                           + _dot(p, v_ref[0, hd, pl.ds(start, tq), :]))
            m_ref[hd] = m_new

    def body(j, carry):
        block(j, False)
        return carry

    lax.fori_loop(0, qi, body, 0)
    block(qi, True)

    outs = []
    for hd in range(ATT_HEADS):
        acc = acc_ref[hd]
        outs.append(acc[:, :V_HEAD_DIM] / acc[:, DENOM_LANE:DENOM_LANE + 1])
    o_ref[0] = jnp.concatenate(outs, axis=-1).astype(BF16)


def _attention_stage(q, k, v):
    b, nh, s, _ = q.shape
    tq = ATT_TILE
    g = ATT_HEADS
    return pl.pallas_call(
        _attention_kernel,
        grid=(b, nh // g, s // tq),
        in_specs=[pl.BlockSpec((1, g, tq, HEAD_PAD), lambda bi, hp, i: (bi, hp, i, 0)),
                  pl.BlockSpec((1, g, s, HEAD_PAD), lambda bi, hp, i: (bi, hp, 0, 0)),
                  pl.BlockSpec((1, g, s, HEAD_PAD), lambda bi, hp, i: (bi, hp, 0, 0))],
        out_specs=pl.BlockSpec((1, tq, g * V_HEAD_DIM), lambda bi, hp, i: (bi, i, hp)),
        out_shape=jax.ShapeDtypeStruct((b, s, BRANCH_WIDTH), BF16),
        scratch_shapes=[pltpu.VMEM((g, tq, LANES), F32), pltpu.VMEM((g, tq, HEAD_PAD), F32),
                        pltpu.VMEM((g, tq, tq), F32)],
        compiler_params=pltpu.CompilerParams(
            dimension_semantics=("arbitrary", "arbitrary", "arbitrary"),
            vmem_limit_bytes=VMEM_LIMIT),
        name="causal_attention",
    )(q, k, v)


def _hgrn_span_matrix():
    c = HG_CHUNK
    x = np.arange(c)[:, None]
    y = np.arange(c)[None, :]
    mats = [(y <= x)]
    for b in HG_LEVELS:
        r = (x // (2 * b)) * (2 * b) + b
        mats.append((y > np.minimum(x, r)) & (y <= np.maximum(x, r)))
    m = np.concatenate(mats, axis=0).astype(np.float32)
    return np.concatenate([m, m], axis=1)


def _hgrn_chunk_local(q, hf, v_bf, lb, span, right_rows, pair_masks):
    c = HG_CHUNK
    f = lb + (1.0 - lb) * jax.nn.sigmoid(hf)
    lf = jnp.log2(f)
    kk = 1.0 - f
    lf_hi = lf.astype(BF16)
    lf_lo = (lf - lf_hi.astype(F32)).astype(BF16)
    spans = _dot(span, jnp.concatenate([lf_hi, lf_lo], axis=0))
    cum = spans[:c]
    last = cum[c - 1:c, :]
    q_dec = (q * jnp.exp2(cum)).astype(BF16)
    k_dec = (kk * jnp.exp2(last - cum)).astype(BF16)
    q_bf = q.astype(BF16)
    k_bf = kk.astype(BF16)
    zs = [(jnp.where(right_rows[lvl], q, kk)
           * jnp.exp2(spans[(lvl + 1) * c:(lvl + 2) * c])).astype(BF16)
          for lvl in range(len(HG_LEVELS))]

    o_intra, incr = [], []
    for hd in range(HG_HEADS):
        sl = slice(hd * HG_DIM, (hd + 1) * HG_DIM)
        att = jnp.where(pair_masks[0], _dot_nt(q_bf[:, sl], k_bf[:, sl]), 0.0)
        for lvl in range(len(HG_LEVELS)):
            z = zs[lvl][:, sl]
            att = jnp.where(pair_masks[lvl + 1], _dot_nt(z, z), att)
        v_h = v_bf[:, sl]
        o_intra.append(_dot(att.astype(BF16), v_h))
        incr.append(_dot(v_h.astype(F32).T.astype(BF16), k_dec[:, sl]))
    return o_intra, q_dec, incr, jnp.exp2(last)


def _hgrn_kernel(hq_ref, hf_ref, hi_ref, hg_ref, lb_ref, gout_ref, span_ref, o_ref, state_ref):
    @pl.when(pl.program_id(1) == 0)
    def _():
        state_ref[...] = jnp.zeros(state_ref.shape, F32)

    c = HG_CHUNK
    t_i = lax.broadcasted_iota(jnp.int32, (c, c), 0)
    s_i = lax.broadcasted_iota(jnp.int32, (c, c), 1)
    row = lax.broadcasted_iota(jnp.int32, (c, HG_WIDTH), 0)
    pair_masks = [t_i == s_i] + [((t_i // b) ^ (s_i // b) == 1) & (s_i < t_i) for b in HG_LEVELS]
    right_rows = [(row // b) % 2 == 1 for b in HG_LEVELS]
    span = span_ref[...]
    lb = lb_ref[...]
    n_chunks = HG_BLOCK // c

    local = []
    for ci in range(n_chunks):
        rows = slice(ci * c, (ci + 1) * c)
        local.append(_hgrn_chunk_local(hq_ref[0, rows, :].astype(F32), hf_ref[0, rows, :],
                                       hi_ref[0, rows, :], lb, span, right_rows, pair_masks))

    gout = gout_ref[...]
    for hd in range(HG_HEADS):
        sl = slice(hd * HG_DIM, (hd + 1) * HG_DIM)
        state_t = state_ref[hd]
        for ci in range(n_chunks):
            rows = slice(ci * c, (ci + 1) * c)
            o_intra, q_dec, incr, decay = local[ci]
            o = o_intra[hd] + _dot_nt(q_dec[:, sl], state_t.astype(BF16))
            state_t = state_t * decay[:, sl] + incr[hd]
            gate = hg_ref[0, rows, sl].astype(F32)
            o_ref[0, rows, sl] = (_rms(o, gout) * (gate * jax.nn.sigmoid(gate))).astype(BF16)
        state_ref[hd] = state_t


def _hgrn_stage(hq, hf, hi, hg, lb, gout):
    span = jnp.asarray(_hgrn_span_matrix(), BF16)
    b, s, _ = hq.shape
    tb = HG_BLOCK
    tok = pl.BlockSpec((1, tb, HG_WIDTH), lambda bi, i: (bi, i, 0))
    return pl.pallas_call(
        _hgrn_kernel,
        grid=(b, s // tb),
        in_specs=[tok, tok, tok, tok, _const_spec(lb.shape), _const_spec(gout.shape),
                  _const_spec(span.shape)],
        out_specs=tok,
        out_shape=jax.ShapeDtypeStruct((b, s, HG_WIDTH), BF16),
        scratch_shapes=[pltpu.VMEM((HG_HEADS, HG_DIM, HG_DIM), F32)],
        compiler_params=pltpu.CompilerParams(
            dimension_semantics=("arbitrary", "arbitrary"), vmem_limit_bytes=VMEM_LIMIT),
        name="hgrn2_recurrence",
    )(hq, hf, hi, hg, lb, gout, span)


def _output_kernel(x_ref, attn_ref, rec_ref, gates_ref, p_ref,
                   wb0_ref, wb1_ref, wout_ref, gffn_ref, wgate_ref, wup_ref, wdown_ref,
                   gpg_ref, wpg_ref, wpp_ref, gpost_ref, o_ref):
    y0 = _dot(attn_ref[...], wb0_ref[...])
    y1 = _dot(rec_ref[...], wb1_ref[...])
    g0 = gates_ref[:, :D_MODEL].astype(F32)
    g1 = gates_ref[:, D_MODEL:].astype(F32)
    merged = (g0 * y0 + g1 * y1).astype(BF16)
    x1 = x_ref[...] + _dot(merged, wout_ref[...])

    h2 = _rms(x1, gffn_ref[...]).astype(BF16)
    x2 = x1
    lo = 0
    for width in FFN_SPLITS:
        a = _dot(h2, wgate_ref[:, lo:lo + width])
        u = _dot(h2, wup_ref[:, lo:lo + width])
        z = (a * jax.nn.sigmoid(a) * u).astype(BF16)
        x2 = x2 + _dot(z, wdown_ref[lo:lo + width, :])
        lo += width

    e = _rms(_dot(p_ref[...].astype(BF16), wpp_ref[...]), gpost_ref[...])
    g = jax.nn.sigmoid(_dot(_rms(x2, gpg_ref[...]).astype(BF16), wpg_ref[...]))
    o_ref[...] = x2 + g * e


def _output_stage(x2d, attn, rec, gates, p2d, wb0, wb1, wout, gffn, wgate, wup, wdown,
                  gpg, wpg, wpp, gpost):
    t = x2d.shape[0]
    tm = TOK_TILE

    def tok(width):
        return pl.BlockSpec((tm, width), lambda i: (i, 0))

    consts = (wb0, wb1, wout, gffn, wgate, wup, wdown, gpg, wpg, wpp, gpost)
    return pl.pallas_call(
        _output_kernel,
        grid=(t // tm,),
        in_specs=[tok(D_MODEL), tok(BRANCH_WIDTH), tok(HG_WIDTH), tok(2 * D_MODEL), tok(PLE_DIM)]
        + [_const_spec(c.shape) for c in consts],
        out_specs=tok(D_MODEL),
        out_shape=jax.ShapeDtypeStruct((t, D_MODEL), F32),
        compiler_params=pltpu.CompilerParams(
            dimension_semantics=("arbitrary",), vmem_limit_bytes=VMEM_LIMIT),
        name="output_stage",
    )(x2d, attn, rec, gates, p2d, *consts)


def _pad_heads(w, heads, width):
    rows = w.shape[0]
    w = w.reshape(rows, heads, width)
    w = jnp.pad(w, ((0, 0), (0, 0), (0, HEAD_PAD - width)))
    return w.reshape(rows, heads * HEAD_PAD)


def _qk_head_layout(nope, rope):
    like = nope if nope is not None else rope

    def z(width):
        return jnp.zeros(like.shape[:-1] + (width,), like.dtype)

    n0 = nope[..., :HEAD_HALF_NOPE] if nope is not None else z(HEAD_HALF_NOPE)
    n1 = nope[..., HEAD_HALF_NOPE:] if nope is not None else z(HEAD_HALF_NOPE)
    x1 = rope[..., :ROPE_HALF] if rope is not None else z(ROPE_HALF)
    x2 = rope[..., ROPE_HALF:] if rope is not None else z(ROPE_HALF)
    return jnp.concatenate([n0, x1, z(ROPE_HALF), n1, x2, z(ROPE_HALF)], axis=-1)


def _row(v):
    return v.reshape(1, -1).astype(F32)


def kernel(x, p, positions, mix_norm_g, w_in, q_a_norm_g, w_uq, kv_a_norm_g, w_ukv, q_norm_g, k_norm_g, hg_lb_logits, hg_out_norm_g, w_branch, w_out, ffn_norm_g, w_ffn_gate, w_ffn_up, w_ffn_down, ple_gate_norm_g, w_ple_gate, w_ple_proj, ple_post_norm_g):
    b, s, d = x.shape
    depth = w_in.shape[0]
    lower_bounds = jnp.cumsum(jax.nn.softmax(hg_lb_logits.astype(F32), axis=0), axis=0)
    pos3d = positions.reshape(b * s // TOK_TILE, 1, TOK_TILE)
    x2d = x.reshape(b * s, d)
    q_scale = QK_HEAD_DIM ** -0.5 * math.log2(math.e)

    for layer in range(depth):
        wi = w_in[layer]
        wall = _input_weight_layout(wi.T)
        wq3 = w_uq[layer].reshape(Q_LORA_RANK, MLA_HEADS, QK_HEAD_DIM)
        wuq = _qk_head_layout(wq3[..., :QK_NOPE_DIM], wq3[..., QK_NOPE_DIM:])
        wuq = wuq.reshape(Q_LORA_RANK, MLA_HEADS * HEAD_PAD).astype(BF16)
        wukv = w_ukv[layer].reshape(KV_LORA_RANK, MLA_HEADS, QK_NOPE_DIM + V_HEAD_DIM)
        wk = _qk_head_layout(wukv[..., :QK_NOPE_DIM], None).reshape(KV_LORA_RANK, -1)
        wv = _pad_heads(wukv[:, :, QK_NOPE_DIM:].reshape(KV_LORA_RANK, -1), MLA_HEADS, V_HEAD_DIM)
        wkv = jnp.concatenate([wk, wv], axis=1).astype(BF16)
        gq = q_norm_g[layer].astype(F32) * q_scale
        gk = k_norm_g[layer].astype(F32)

        q, k, v, hq, hf, hi, hg, gates = _input_stage(
            x2d, pos3d, b, s, _row(mix_norm_g[layer]), wall, _row(q_a_norm_g[layer]), wuq,
            _row(kv_a_norm_g[layer]), wkv,
            _row(_qk_head_layout(gq[:QK_NOPE_DIM], gq[QK_NOPE_DIM:])),
            _row(_qk_head_layout(gk[:QK_NOPE_DIM], gk[QK_NOPE_DIM:])))

        attn = _attention_stage(q, k, v)
        rec = _hgrn_stage(hq.reshape(b, s, HG_WIDTH), hf.reshape(b, s, HG_WIDTH),
                          hi.reshape(b, s, HG_WIDTH), hg.reshape(b, s, HG_WIDTH),
                          _row(lower_bounds[layer]), _row(hg_out_norm_g[layer]))

        x2d = _output_stage(
            x2d, attn.reshape(b * s, BRANCH_WIDTH), rec.reshape(b * s, HG_WIDTH), gates,
            p[layer].reshape(b * s, PLE_DIM),
            w_branch[layer, 0].astype(BF16), w_branch[layer, 1].astype(BF16),
            w_out[layer].astype(BF16), _row(ffn_norm_g[layer]),
            w_ffn_gate[layer].astype(BF16), w_ffn_up[layer].astype(BF16),
            w_ffn_down[layer].astype(BF16), _row(ple_gate_norm_g[layer]),
            w_ple_gate[layer].astype(BF16), w_ple_proj[layer].astype(BF16),
            _row(ple_post_norm_g[layer]))
    return x2d.reshape(b, s, d)
```

```python
import functools
import math

import numpy as np

import jax
import jax.numpy as jnp
from jax import lax
from jax.experimental import pallas as pl
from jax.experimental.pallas import tpu as pltpu

D_MODEL = 1024
MLA_HEADS = 8
QK_NOPE_DIM = 64
QK_ROPE_DIM = 32
ROPE_HALF = QK_ROPE_DIM // 2
QK_HEAD_DIM = QK_NOPE_DIM + QK_ROPE_DIM
V_HEAD_DIM = 64
Q_LORA_RANK = 384
KV_LORA_RANK = 256
ROPE_BASE = 10000.0
HG_HEADS = 4
HG_DIM = 128
HG_WIDTH = HG_HEADS * HG_DIM
BRANCH_WIDTH = MLA_HEADS * V_HEAD_DIM
FFN_HIDDEN = 2816
PLE_DIM = 256
EPS = 1e-6

LANES = 128
BF16_ROWS = 16
HEAD_PAD = LANES
HEAD_HALF_NOPE = QK_NOPE_DIM // 2
W_LAT = Q_LORA_RANK + KV_LORA_RANK + HEAD_PAD
DENOM_LANE = V_HEAD_DIM

VMEM_LIMIT = 60 * 1024 * 1024

TOK_TILE = 512
ATT_TILE = 512
ATT_HEADS = 8
HG_BLOCK = 512
HG_CHUNK = 64
HG_LEVELS = tuple(HG_CHUNK >> (i + 1) for i in range(HG_CHUNK.bit_length() - 1))
FFN_SPLITS = (1024, 1024, 768)

BF16 = jnp.bfloat16
F32 = jnp.float32


def _const_spec(shape):
    nd = len(shape)
    return pl.BlockSpec(shape, lambda *_: (0,) * nd, pipeline_mode=pl.Buffered(1))


def _rms(v, gain, width=None):
    n = v.shape[-1] if width is None else width
    ms = jnp.sum(v * v, axis=-1, keepdims=True) * (1.0 / n)
    return v * lax.rsqrt(ms + EPS) * gain


def _dot(a, b):
    return jnp.dot(a, b, preferred_element_type=F32)


def _dot_nt(a, b):
    return lax.dot_general(a, b, (((1,), (1,)), ((), ())), preferred_element_type=F32)


def _rope_tables(pos_row):
    tm = pos_row.shape[1]
    fidx = lax.broadcasted_iota(jnp.int32, (ROPE_HALF, 1), 0).astype(F32)
    inv_freq = jnp.exp(fidx * (-math.log(ROPE_BASE) * 2.0 / QK_ROPE_DIM))
    ang = pos_row.astype(F32) * inv_freq
    cos = jnp.cos(ang)
    sin = jnp.sin(ang)
    ones = jnp.ones((HEAD_HALF_NOPE, tm), F32)
    zeros_n = jnp.zeros((HEAD_HALF_NOPE, tm), F32)
    zeros_p = jnp.zeros((ROPE_HALF, tm), F32)
    cos_t = jnp.concatenate([ones, cos, zeros_p, ones, cos, zeros_p], axis=0)
    sin_t = jnp.concatenate([zeros_n, -sin, zeros_p, zeros_n, sin, zeros_p], axis=0)
    return cos_t.T, sin_t.T


def _input_kernel(x_ref, pos_ref, gmix_ref, wall_ref, gqa_ref, wuq_ref, gkva_ref, wkv_ref,
                  gq_ref, gk_ref, *rest):
    n_side = (len(rest) - 8) // 2
    side_in = rest[:n_side]
    q_ref, k_ref, v_ref, hq_ref, hf_ref, hi_ref, hg_ref, gates_ref = rest[n_side:n_side + 8]
    side_out = rest[n_side + 8:]
    for w_ref, wb_ref in zip(side_in, side_out):
        wb_ref[...] = w_ref[...].astype(BF16)

    x = x_ref[...]
    h = _rms(x, gmix_ref[...]).astype(BF16)

    lat = _dot(h, wall_ref[:, :W_LAT])
    cq = _rms(lat[:, :Q_LORA_RANK], gqa_ref[...]).astype(BF16)
    ckv = _rms(lat[:, Q_LORA_RANK:Q_LORA_RANK + KV_LORA_RANK], gkva_ref[...]).astype(BF16)
    k_rope = lat[:, Q_LORA_RANK + KV_LORA_RANK:]

    cos_tab, sin_tab = _rope_tables(pos_ref[0])
    gq = gq_ref[...]
    gk = gk_ref[...]
    q_cos = cos_tab * gq
    q_sin = sin_tab * pltpu.roll(gq, LANES // 2, 1)
    kr = k_rope * gk
    kr = kr * cos_tab + pltpu.roll(kr, LANES // 2, 1) * sin_tab
    kr_ss = jnp.sum(k_rope * k_rope, axis=-1, keepdims=True)

    q_all = _dot(cq, wuq_ref[...])
    kv_all = _dot(ckv, wkv_ref[...])
    lane = lax.broadcasted_iota(jnp.int32, (1, LANES), 1)
    one_lane = (lane == DENOM_LANE).astype(F32)
    inv_width = 1.0 / QK_HEAD_DIM
    for hd in range(MLA_HEADS):
        sl = slice(hd * HEAD_PAD, (hd + 1) * HEAD_PAD)
        t = q_all[:, sl]
        r = lax.rsqrt(jnp.sum(t * t, axis=-1, keepdims=True) * inv_width + EPS)
        q_ref[0, hd] = ((t * q_cos + pltpu.roll(t, LANES // 2, 1) * q_sin) * r).astype(BF16)
        t = kv_all[:, sl]
        r = lax.rsqrt((jnp.sum(t * t, axis=-1, keepdims=True) + kr_ss) * inv_width + EPS)
        k_ref[0, hd] = ((t * gk + kr) * r).astype(BF16)
        vsl = slice(MLA_HEADS * HEAD_PAD + hd * HEAD_PAD, MLA_HEADS * HEAD_PAD + (hd + 1) * HEAD_PAD)
        v_ref[0, hd] = (kv_all[:, vsl] + one_lane).astype(BF16)

    hh = _dot(h, wall_ref[:, W_LAT:W_LAT + 4 * HG_WIDTH])
    hq_ref[...] = hh[:, :HG_WIDTH].astype(BF16)
    hf_ref[...] = hh[:, HG_WIDTH:2 * HG_WIDTH]
    hi_ref[...] = hh[:, 2 * HG_WIDTH:3 * HG_WIDTH].astype(BF16)
    hg_ref[...] = hh[:, 3 * HG_WIDTH:].astype(BF16)

    gates_ref[...] = jax.nn.sigmoid(_dot(h, wall_ref[:, W_LAT + 4 * HG_WIDTH:])).astype(BF16)


W_PREP_COLS = 256


def _input_weight_kernel(wt_ref, o_ref):
    c0 = Q_LORA_RANK + KV_LORA_RANK
    c1 = c0 + QK_ROPE_DIM
    j = pl.program_id(0)
    n = W_PREP_COLS
    lat_full = c0 // n

    @pl.when(j < lat_full)
    def _():
        start = pl.multiple_of(j * n, n)
        o_ref[...] = wt_ref[pl.ds(start, n), :].T.astype(BF16)

    @pl.when(j == lat_full)
    def _():
        d = wt_ref.shape[1]
        z_n = jnp.zeros((HEAD_HALF_NOPE, d), F32)
        z_p = jnp.zeros((ROPE_HALF, d), F32)
        blk = jnp.concatenate([wt_ref[lat_full * n:c0, :], z_n, wt_ref[c0:c0 + ROPE_HALF, :], z_p,
                               z_n, wt_ref[c0 + ROPE_HALF:c1, :], z_p], axis=0)
        o_ref[...] = blk.T.astype(BF16)

    @pl.when(j > lat_full)
    def _():
        start = pl.multiple_of(c1 + (j - lat_full - 1) * n, 2 * ROPE_HALF)
        o_ref[...] = wt_ref[pl.ds(start, n), :].T.astype(BF16)


def _input_weight_layout(wt):
    cols, rows = wt.shape
    out_cols = cols - QK_ROPE_DIM + HEAD_PAD
    return pl.pallas_call(
        _input_weight_kernel,
        grid=(out_cols // W_PREP_COLS,),
        in_specs=[_const_spec(wt.shape)],
        out_specs=pl.BlockSpec((rows, W_PREP_COLS), lambda j: (0, j)),
        out_shape=jax.ShapeDtypeStruct((rows, out_cols), BF16),
        compiler_params=pltpu.CompilerParams(
            dimension_semantics=("arbitrary",), vmem_limit_bytes=VMEM_LIMIT),
        name="input_weight_layout",
    )(wt)


def _side_cast_spec(rows, cols, steps):
    for share in (1, 2):
        blk, rem = divmod(rows * share, steps)
        if rem == 0 and blk % BF16_ROWS == 0:
            return pl.BlockSpec((blk, cols), lambda i: (i // share, 0))
    raise ValueError(f"no row block for a ({rows},{cols}) weight over {steps} steps")


def _input_stage(x2d, pos3d, b, s, gmix, wall, gqa, wuq, gkva, wkv, gq, gk, side_weights):
    t = x2d.shape[0]
    tm = TOK_TILE
    nt = s // tm
    steps = t // tm

    def tok(width):
        return pl.BlockSpec((tm, width), lambda i: (i, 0))

    def head_spec():
        return pl.BlockSpec((1, MLA_HEADS, tm, HEAD_PAD), lambda i: (i // nt, 0, i % nt, 0))

    head_shape = jax.ShapeDtypeStruct((b, MLA_HEADS, s, HEAD_PAD), BF16)
    consts = (gmix, wall, gqa, wuq, gkva, wkv, gq, gk)
    side_specs = [_side_cast_spec(w.shape[0], w.shape[1], steps) for w in side_weights]
    outs = pl.pallas_call(
        _input_kernel,
        grid=(steps,),
        in_specs=[tok(D_MODEL), pl.BlockSpec((1, 1, tm), lambda i: (i, 0, 0))]
        + [_const_spec(c.shape) for c in consts] + side_specs,
        out_specs=[head_spec(), head_spec(), head_spec(),
                   tok(HG_WIDTH), tok(HG_WIDTH), tok(HG_WIDTH), tok(HG_WIDTH), tok(2 * D_MODEL)]
        + side_specs,
        out_shape=[head_shape, head_shape, head_shape,
                   jax.ShapeDtypeStruct((t, HG_WIDTH), BF16),
                   jax.ShapeDtypeStruct((t, HG_WIDTH), F32),
                   jax.ShapeDtypeStruct((t, HG_WIDTH), BF16),
                   jax.ShapeDtypeStruct((t, HG_WIDTH), BF16),
                   jax.ShapeDtypeStruct((t, 2 * D_MODEL), BF16)]
        + [jax.ShapeDtypeStruct(w.shape, BF16) for w in side_weights],
        compiler_params=pltpu.CompilerParams(
            dimension_semantics=("arbitrary",), vmem_limit_bytes=VMEM_LIMIT),
        name="input_stage",
    )(x2d, pos3d, *consts, *side_weights)
    return outs[:8], outs[8:]


def _attention_kernel(q_ref, k_ref, v_ref, o_ref, m_ref, acc_ref, s_ref):
    qi = pl.program_id(2)
    tq = ATT_TILE
    half = tq // 2
    m_ref[...] = jnp.full(m_ref.shape, -jnp.inf, F32)
    acc_ref[...] = jnp.zeros(acc_ref.shape, F32)

    def block(j, rows, n_keys, visible):
        start = pl.multiple_of(j * tq, tq)
        for hd in range(ATT_HEADS):
            s_ref[hd, rows, :n_keys] = _dot_nt(q_ref[0, hd, rows, :],
                                               k_ref[0, hd, pl.ds(start, n_keys), :])
        for hd in range(ATT_HEADS):
            sc = s_ref[hd, rows, :n_keys]
            if visible is not None:
                sc = jnp.where(visible, sc, -jnp.inf)
            m_old = m_ref[hd, rows, :]
            m_new = jnp.maximum(m_old, jnp.max(sc, axis=-1, keepdims=True))
            p = jnp.exp2(sc - jnp.concatenate([m_new] * (n_keys // LANES), axis=1)).astype(BF16)
            acc_ref[hd, rows, :] = (jnp.exp2(m_old - m_new) * acc_ref[hd, rows, :]
                                    + _dot(p, v_ref[0, hd, pl.ds(start, n_keys), :]))
            m_ref[hd, rows, :] = m_new

    def body(j, carry):
        block(j, slice(0, tq), tq, None)
        return carry

    lax.fori_loop(0, qi, body, 0)
    def lower(n_keys, shift):
        r_i = lax.broadcasted_iota(jnp.int32, (half, n_keys), 0)
        c_i = lax.broadcasted_iota(jnp.int32, (half, n_keys), 1)
        return c_i <= r_i + shift

    block(qi, slice(0, half), half, lower(half, 0))
    block(qi, slice(half, tq), tq, lower(tq, half))

    outs = []
    for hd in range(ATT_HEADS):
        acc = acc_ref[hd]
        outs.append(acc[:, :V_HEAD_DIM] / acc[:, DENOM_LANE:DENOM_LANE + 1])
    o_ref[0] = jnp.concatenate(outs, axis=-1).astype(BF16)


def _attention_stage(q, k, v):
    b, nh, s, _ = q.shape
    tq = ATT_TILE
    g = ATT_HEADS
    return pl.pallas_call(
        _attention_kernel,
        grid=(b, nh // g, s // tq),
        in_specs=[pl.BlockSpec((1, g, tq, HEAD_PAD), lambda bi, hp, i: (bi, hp, i, 0)),
                  pl.BlockSpec((1, g, s, HEAD_PAD), lambda bi, hp, i: (bi, hp, 0, 0)),
                  pl.BlockSpec((1, g, s, HEAD_PAD), lambda bi, hp, i: (bi, hp, 0, 0))],
        out_specs=pl.BlockSpec((1, tq, g * V_HEAD_DIM), lambda bi, hp, i: (bi, i, hp)),
        out_shape=jax.ShapeDtypeStruct((b, s, BRANCH_WIDTH), BF16),
        scratch_shapes=[pltpu.VMEM((g, tq, LANES), F32), pltpu.VMEM((g, tq, HEAD_PAD), F32),
                        pltpu.VMEM((g, tq, tq), F32)],
        compiler_params=pltpu.CompilerParams(
            dimension_semantics=("arbitrary", "arbitrary", "arbitrary"),
            vmem_limit_bytes=VMEM_LIMIT),
        name="causal_attention",
    )(q, k, v)


def _hgrn_span_matrix():
    c = HG_CHUNK
    x = np.arange(c)[:, None]
    y = np.arange(c)[None, :]
    mats = [(y <= x)]
    for b in HG_LEVELS:
        r = (x // (2 * b)) * (2 * b) + b
        mats.append((y > np.minimum(x, r)) & (y <= np.maximum(x, r)))
    m = np.concatenate(mats, axis=0).astype(np.float32)
    return np.concatenate([m, m], axis=1)


def _hgrn_chunk_local(q, hf, v_bf, lb, span, right_rows, pair_masks):
    c = HG_CHUNK
    f = lb + (1.0 - lb) * jax.nn.sigmoid(hf)
    lf = jnp.log2(f)
    kk = 1.0 - f
    lf_hi = lf.astype(BF16)
    lf_lo = (lf - lf_hi.astype(F32)).astype(BF16)
    spans = _dot(span, jnp.concatenate([lf_hi, lf_lo], axis=0))
    cum = spans[:c]
    last = cum[c - 1:c, :]
    q_dec = (q * jnp.exp2(cum)).astype(BF16)
    k_dec = (kk * jnp.exp2(last - cum)).astype(BF16)
    q_bf = q.astype(BF16)
    k_bf = kk.astype(BF16)
    zs = [(jnp.where(right_rows[lvl], q, kk)
           * jnp.exp2(spans[(lvl + 1) * c:(lvl + 2) * c])).astype(BF16)
          for lvl in range(len(HG_LEVELS))]

    o_intra, incr = [], []
    for hd in range(HG_HEADS):
        sl = slice(hd * HG_DIM, (hd + 1) * HG_DIM)
        att = jnp.where(pair_masks[0], _dot_nt(q_bf[:, sl], k_bf[:, sl]), 0.0)
        for lvl in range(len(HG_LEVELS)):
            z = zs[lvl][:, sl]
            att = jnp.where(pair_masks[lvl + 1], _dot_nt(z, z), att)
        v_h = v_bf[:, sl]
        o_intra.append(_dot(att.astype(BF16), v_h))
        incr.append(_dot(v_h.astype(F32).T.astype(BF16), k_dec[:, sl]))
    return o_intra, q_dec, incr, jnp.exp2(last)


def _hgrn_kernel(hq_ref, hf_ref, hi_ref, hg_ref, lb_ref, gout_ref, span_ref, o_ref, state_ref):
    @pl.when(pl.program_id(1) == 0)
    def _():
        state_ref[...] = jnp.zeros(state_ref.shape, F32)

    c = HG_CHUNK
    t_i = lax.broadcasted_iota(jnp.int32, (c, c), 0)
    s_i = lax.broadcasted_iota(jnp.int32, (c, c), 1)
    row = lax.broadcasted_iota(jnp.int32, (c, HG_WIDTH), 0)
    pair_masks = [t_i == s_i] + [((t_i // b) ^ (s_i // b) == 1) & (s_i < t_i) for b in HG_LEVELS]
    right_rows = [(row // b) % 2 == 1 for b in HG_LEVELS]
    span = span_ref[...]
    lb = lb_ref[...]
    n_chunks = HG_BLOCK // c

    local = []
    for ci in range(n_chunks):
        rows = slice(ci * c, (ci + 1) * c)
        local.append(_hgrn_chunk_local(hq_ref[0, rows, :].astype(F32), hf_ref[0, rows, :],
                                       hi_ref[0, rows, :], lb, span, right_rows, pair_masks))

    gout = gout_ref[...]
    for hd in range(HG_HEADS):
        sl = slice(hd * HG_DIM, (hd + 1) * HG_DIM)
        state_t = state_ref[hd]
        for ci in range(n_chunks):
            rows = slice(ci * c, (ci + 1) * c)
            o_intra, q_dec, incr, decay = local[ci]
            o = o_intra[hd] + _dot_nt(q_dec[:, sl], state_t.astype(BF16))
            state_t = state_t * decay[:, sl] + incr[hd]
            gate = hg_ref[0, rows, sl].astype(F32)
            o_ref[0, rows, sl] = (_rms(o, gout) * (gate * jax.nn.sigmoid(gate))).astype(BF16)
        state_ref[hd] = state_t


def _hgrn_stage(hq, hf, hi, hg, lb, gout):
    span = jnp.asarray(_hgrn_span_matrix(), BF16)
    b, s, _ = hq.shape
    tb = HG_BLOCK
    tok = pl.BlockSpec((1, tb, HG_WIDTH), lambda bi, i: (bi, i, 0))
    return pl.pallas_call(
        _hgrn_kernel,
        grid=(b, s // tb),
        in_specs=[tok, tok, tok, tok, _const_spec(lb.shape), _const_spec(gout.shape),
                  _const_spec(span.shape)],
        out_specs=tok,
        out_shape=jax.ShapeDtypeStruct((b, s, HG_WIDTH), BF16),
        scratch_shapes=[pltpu.VMEM((HG_HEADS, HG_DIM, HG_DIM), F32)],
        compiler_params=pltpu.CompilerParams(
            dimension_semantics=("arbitrary", "arbitrary"), vmem_limit_bytes=VMEM_LIMIT),
        name="hgrn2_recurrence",
    )(hq, hf, hi, hg, lb, gout, span)


def _output_kernel(x_ref, attn_ref, rec_ref, gates_ref, p_ref,
                   wb_ref, wout_ref, gffn_ref, wgate_ref, wup_ref, wdown_ref,
                   gpg_ref, wpg_ref, wpp_ref, gpost_ref, o_ref):
    y0 = _dot(attn_ref[...], wb_ref[:BRANCH_WIDTH, :])
    y1 = _dot(rec_ref[...], wb_ref[BRANCH_WIDTH:, :])
    g0 = gates_ref[:, :D_MODEL].astype(F32)
    g1 = gates_ref[:, D_MODEL:].astype(F32)
    merged = (g0 * y0 + g1 * y1).astype(BF16)
    x1 = x_ref[...] + _dot(merged, wout_ref[...])

    h2 = _rms(x1, gffn_ref[...]).astype(BF16)
    x2 = x1
    lo = 0
    for width in FFN_SPLITS:
        a = _dot(h2, wgate_ref[:, lo:lo + width])
        u = _dot(h2, wup_ref[:, lo:lo + width])
        z = (a * jax.nn.sigmoid(a) * u).astype(BF16)
        x2 = x2 + _dot(z, wdown_ref[lo:lo + width, :])
        lo += width

    e = _rms(_dot(p_ref[...].astype(BF16), wpp_ref[...]), gpost_ref[...])
    g = jax.nn.sigmoid(_dot(_rms(x2, gpg_ref[...]).astype(BF16), wpg_ref[...]))
    o_ref[...] = x2 + g * e


def _output_stage(x2d, attn, rec, gates, p2d, wb, wout, gffn, wgate, wup, wdown,
                  gpg, wpg, wpp, gpost):
    t = x2d.shape[0]
    tm = TOK_TILE

    def tok(width):
        return pl.BlockSpec((tm, width), lambda i: (i, 0))

    consts = (wb, wout, gffn, wgate, wup, wdown, gpg, wpg, wpp, gpost)
    return pl.pallas_call(
        _output_kernel,
        grid=(t // tm,),
        in_specs=[tok(D_MODEL), tok(BRANCH_WIDTH), tok(HG_WIDTH), tok(2 * D_MODEL), tok(PLE_DIM)]
        + [_const_spec(c.shape) for c in consts],
        out_specs=tok(D_MODEL),
        out_shape=jax.ShapeDtypeStruct((t, D_MODEL), F32),
        compiler_params=pltpu.CompilerParams(
            dimension_semantics=("arbitrary",), vmem_limit_bytes=VMEM_LIMIT),
        name="output_stage",
    )(x2d, attn, rec, gates, p2d, *consts)


def _pad_heads(w, heads, width):
    rows = w.shape[0]
    w = w.reshape(rows, heads, width)
    w = jnp.pad(w, ((0, 0), (0, 0), (0, HEAD_PAD - width)))
    return w.reshape(rows, heads * HEAD_PAD)


def _qk_head_layout(nope, rope):
    like = nope if nope is not None else rope

    def z(width):
        return jnp.zeros(like.shape[:-1] + (width,), like.dtype)

    n0 = nope[..., :HEAD_HALF_NOPE] if nope is not None else z(HEAD_HALF_NOPE)
    n1 = nope[..., HEAD_HALF_NOPE:] if nope is not None else z(HEAD_HALF_NOPE)
    x1 = rope[..., :ROPE_HALF] if rope is not None else z(ROPE_HALF)
    x2 = rope[..., ROPE_HALF:] if rope is not None else z(ROPE_HALF)
    return jnp.concatenate([n0, x1, z(ROPE_HALF), n1, x2, z(ROPE_HALF)], axis=-1)


def _row(v):
    return v.reshape(1, -1).astype(F32)


def kernel(x, p, positions, mix_norm_g, w_in, q_a_norm_g, w_uq, kv_a_norm_g, w_ukv, q_norm_g, k_norm_g, hg_lb_logits, hg_out_norm_g, w_branch, w_out, ffn_norm_g, w_ffn_gate, w_ffn_up, w_ffn_down, ple_gate_norm_g, w_ple_gate, w_ple_proj, ple_post_norm_g):
    b, s, d = x.shape
    depth = w_in.shape[0]
    lower_bounds = jnp.cumsum(jax.nn.softmax(hg_lb_logits.astype(F32), axis=0), axis=0)
    pos3d = positions.reshape(b * s // TOK_TILE, 1, TOK_TILE)
    x2d = x.reshape(b * s, d)
    q_scale = QK_HEAD_DIM ** -0.5 * math.log2(math.e)

    for layer in range(depth):
        wi = w_in[layer]
        wall = _input_weight_layout(wi.T)
        wq3 = w_uq[layer].reshape(Q_LORA_RANK, MLA_HEADS, QK_HEAD_DIM)
        wuq = _qk_head_layout(wq3[..., :QK_NOPE_DIM], wq3[..., QK_NOPE_DIM:])
        wuq = wuq.reshape(Q_LORA_RANK, MLA_HEADS * HEAD_PAD).astype(BF16)
        wukv = w_ukv[layer].reshape(KV_LORA_RANK, MLA_HEADS, QK_NOPE_DIM + V_HEAD_DIM)
        wk = _qk_head_layout(wukv[..., :QK_NOPE_DIM], None).reshape(KV_LORA_RANK, -1)
        wv = _pad_heads(wukv[:, :, QK_NOPE_DIM:].reshape(KV_LORA_RANK, -1), MLA_HEADS, V_HEAD_DIM)
        wkv = jnp.concatenate([wk, wv], axis=1).astype(BF16)
        gq = q_norm_g[layer].astype(F32) * q_scale
        gk = k_norm_g[layer].astype(F32)

        side = (w_branch[layer].reshape(2 * BRANCH_WIDTH, d), w_out[layer], w_ffn_gate[layer],
                w_ffn_up[layer], w_ffn_down[layer], w_ple_gate[layer], w_ple_proj[layer])
        (q, k, v, hq, hf, hi, hg, gates), side_bf = _input_stage(
            x2d, pos3d, b, s, _row(mix_norm_g[layer]), wall, _row(q_a_norm_g[layer]), wuq,
            _row(kv_a_norm_g[layer]), wkv,
            _row(_qk_head_layout(gq[:QK_NOPE_DIM], gq[QK_NOPE_DIM:])),
            _row(_qk_head_layout(gk[:QK_NOPE_DIM], gk[QK_NOPE_DIM:])), side)
        wb, wout, wgate, wup, wdown, wpg, wpp = side_bf

        attn = _attention_stage(q, k, v)
        rec = _hgrn_stage(hq.reshape(b, s, HG_WIDTH), hf.reshape(b, s, HG_WIDTH),
                          hi.reshape(b, s, HG_WIDTH), hg.reshape(b, s, HG_WIDTH),
                          _row(lower_bounds[layer]), _row(hg_out_norm_g[layer]))

        x2d = _output_stage(
            x2d, attn.reshape(b * s, BRANCH_WIDTH), rec.reshape(b * s, HG_WIDTH), gates,
            p[layer].reshape(b * s, PLE_DIM),
            wb, wout, _row(ffn_norm_g[layer]),
            wgate, wup, wdown, _row(ple_gate_norm_g[layer]), wpg, wpp,
            _row(ple_post_norm_g[layer]))
    return x2d.reshape(b, s, d)
```

```python
import functools
import math

import numpy as np

import jax
import jax.numpy as jnp
from jax import lax
from jax.experimental import pallas as pl
from jax.experimental.pallas import tpu as pltpu

D_MODEL = 1024
MLA_HEADS = 8
QK_NOPE_DIM = 64
QK_ROPE_DIM = 32
ROPE_HALF = QK_ROPE_DIM // 2
QK_HEAD_DIM = QK_NOPE_DIM + QK_ROPE_DIM
V_HEAD_DIM = 64
Q_LORA_RANK = 384
KV_LORA_RANK = 256
ROPE_BASE = 10000.0
HG_HEADS = 4
HG_DIM = 128
HG_WIDTH = HG_HEADS * HG_DIM
BRANCH_WIDTH = MLA_HEADS * V_HEAD_DIM
FFN_HIDDEN = 2816
PLE_DIM = 256
EPS = 1e-6

LANES = 128
BF16_ROWS = 16
HEAD_PAD = LANES
HEAD_HALF_NOPE = QK_NOPE_DIM // 2
W_LAT = Q_LORA_RANK + KV_LORA_RANK + HEAD_PAD
DENOM_LANE = V_HEAD_DIM

VMEM_LIMIT = 60 * 1024 * 1024

TOK_TILE = 512
ATT_TILE = 512
ATT_HEADS = 8
HG_BLOCK = 512
HG_CHUNK = 64
HG_LEVELS = tuple(HG_CHUNK >> (i + 1) for i in range(HG_CHUNK.bit_length() - 1))
FFN_SPLITS = (1024, 1024, 768)

BF16 = jnp.bfloat16
F32 = jnp.float32


def _const_spec(shape):
    nd = len(shape)
    return pl.BlockSpec(shape, lambda *_: (0,) * nd, pipeline_mode=pl.Buffered(1))


def _rms(v, gain, width=None):
    n = v.shape[-1] if width is None else width
    ms = jnp.sum(v * v, axis=-1, keepdims=True) * (1.0 / n)
    return v * lax.rsqrt(ms + EPS) * gain


def _dot(a, b):
    return jnp.dot(a, b, preferred_element_type=F32)


def _dot_nt(a, b):
    return lax.dot_general(a, b, (((1,), (1,)), ((), ())), preferred_element_type=F32)


def _rope_tables(pos_row):
    tm = pos_row.shape[1]
    fidx = lax.broadcasted_iota(jnp.int32, (ROPE_HALF, 1), 0).astype(F32)
    inv_freq = jnp.exp(fidx * (-math.log(ROPE_BASE) * 2.0 / QK_ROPE_DIM))
    ang = pos_row.astype(F32) * inv_freq
    cos = jnp.cos(ang)
    sin = jnp.sin(ang)
    ones = jnp.ones((HEAD_HALF_NOPE, tm), F32)
    zeros_n = jnp.zeros((HEAD_HALF_NOPE, tm), F32)
    zeros_p = jnp.zeros((ROPE_HALF, tm), F32)
    cos_t = jnp.concatenate([ones, cos, zeros_p, ones, cos, zeros_p], axis=0)
    sin_t = jnp.concatenate([zeros_n, -sin, zeros_p, zeros_n, sin, zeros_p], axis=0)
    return cos_t.T, sin_t.T


def _input_kernel(x_ref, pos_ref, gmix_ref, wall_ref, gqa_ref, wuq_ref, gkva_ref, wkv_ref,
                  gq_ref, gk_ref, *rest):
    n_side = (len(rest) - 8) // 2
    side_in = rest[:n_side]
    q_ref, k_ref, v_ref, hq_ref, hf_ref, hi_ref, hg_ref, gates_ref = rest[n_side:n_side + 8]
    side_out = rest[n_side + 8:]
    for w_ref, wb_ref in zip(side_in, side_out):
        wb_ref[...] = w_ref[...].astype(BF16)

    x = x_ref[...]
    h = _rms(x, gmix_ref[...]).astype(BF16)

    lat = _dot(h, wall_ref[:, :W_LAT])
    cq = _rms(lat[:, :Q_LORA_RANK], gqa_ref[...]).astype(BF16)
    ckv = _rms(lat[:, Q_LORA_RANK:Q_LORA_RANK + KV_LORA_RANK], gkva_ref[...]).astype(BF16)
    k_rope = lat[:, Q_LORA_RANK + KV_LORA_RANK:]

    cos_tab, sin_tab = _rope_tables(pos_ref[0])
    gq = gq_ref[...]
    gk = gk_ref[...]
    q_cos = cos_tab * gq
    q_sin = sin_tab * pltpu.roll(gq, LANES // 2, 1)
    kr = k_rope * gk
    kr = kr * cos_tab + pltpu.roll(kr, LANES // 2, 1) * sin_tab
    kr_ss = jnp.sum(k_rope * k_rope, axis=-1, keepdims=True)

    q_all = _dot(cq, wuq_ref[...])
    kv_all = _dot(ckv, wkv_ref[...])
    lane = lax.broadcasted_iota(jnp.int32, (1, LANES), 1)
    one_lane = (lane == DENOM_LANE).astype(F32)
    inv_width = 1.0 / QK_HEAD_DIM
    for hd in range(MLA_HEADS):
        sl = slice(hd * HEAD_PAD, (hd + 1) * HEAD_PAD)
        t = q_all[:, sl]
        r = lax.rsqrt(jnp.sum(t * t, axis=-1, keepdims=True) * inv_width + EPS)
        q_ref[0, hd] = ((t * q_cos + pltpu.roll(t, LANES // 2, 1) * q_sin) * r).astype(BF16)
        t = kv_all[:, sl]
        r = lax.rsqrt((jnp.sum(t * t, axis=-1, keepdims=True) + kr_ss) * inv_width + EPS)
        k_ref[0, hd] = ((t * gk + kr) * r).astype(BF16)
        vsl = slice(MLA_HEADS * HEAD_PAD + hd * HEAD_PAD, MLA_HEADS * HEAD_PAD + (hd + 1) * HEAD_PAD)
        v_ref[0, hd] = (kv_all[:, vsl] + one_lane).astype(BF16)

    hh = _dot(h, wall_ref[:, W_LAT:W_LAT + 4 * HG_WIDTH])
    hq_ref[...] = hh[:, :HG_WIDTH].astype(BF16)
    hf_ref[...] = hh[:, HG_WIDTH:2 * HG_WIDTH]
    hi_ref[...] = hh[:, 2 * HG_WIDTH:3 * HG_WIDTH].astype(BF16)
    hg_ref[...] = hh[:, 3 * HG_WIDTH:].astype(BF16)

    gates_ref[...] = jax.nn.sigmoid(_dot(h, wall_ref[:, W_LAT + 4 * HG_WIDTH:])).astype(BF16)


W_PREP_COLS = 256


def _input_weight_kernel(wt_ref, o_ref):
    c0 = Q_LORA_RANK + KV_LORA_RANK
    j = pl.program_id(0)
    n = W_PREP_COLS
    lat_full = c0 // n
    lat_rest = c0 - lat_full * n

    @pl.when(j != lat_full)
    def _():
        o_ref[...] = wt_ref[...].T.astype(BF16)

    @pl.when(j == lat_full)
    def _():
        d = wt_ref.shape[1]
        z_n = jnp.zeros((HEAD_HALF_NOPE, d), F32)
        z_p = jnp.zeros((ROPE_HALF, d), F32)
        blk = jnp.concatenate([wt_ref[:lat_rest, :], z_n, wt_ref[lat_rest:lat_rest + ROPE_HALF, :], z_p,
                               z_n, wt_ref[lat_rest + ROPE_HALF:lat_rest + QK_ROPE_DIM, :], z_p],
                              axis=0)
        o_ref[...] = blk.T.astype(BF16)


def _input_weight_layout(wt):
    cols, rows = wt.shape
    out_cols = cols - QK_ROPE_DIM + HEAD_PAD
    n = W_PREP_COLS
    lat_full = (Q_LORA_RANK + KV_LORA_RANK) // n
    pad = HEAD_PAD - QK_ROPE_DIM

    def in_rows(j):
        return (pl.multiple_of(jnp.where(j <= lat_full, j * n, j * n - pad), QK_ROPE_DIM), 0)

    return pl.pallas_call(
        _input_weight_kernel,
        grid=(out_cols // n,),
        in_specs=[pl.BlockSpec((pl.Element(n), pl.Element(rows)), in_rows)],
        out_specs=pl.BlockSpec((rows, n), lambda j: (0, j)),
        out_shape=jax.ShapeDtypeStruct((rows, out_cols), BF16),
        compiler_params=pltpu.CompilerParams(
            dimension_semantics=("arbitrary",), vmem_limit_bytes=VMEM_LIMIT),
        name="input_weight_layout",
    )(wt)


def _side_cast_spec(rows, cols, steps):
    for share in (1, 2):
        blk, rem = divmod(rows * share, steps)
        if rem == 0 and blk % BF16_ROWS == 0:
            return pl.BlockSpec((blk, cols), lambda i: (i // share, 0))
    raise ValueError(f"no row block for a ({rows},{cols}) weight over {steps} steps")


def _input_stage(x2d, pos3d, b, s, gmix, wall, gqa, wuq, gkva, wkv, gq, gk, side_weights):
    t = x2d.shape[0]
    tm = TOK_TILE
    nt = s // tm
    steps = t // tm

    def tok(width):
        return pl.BlockSpec((tm, width), lambda i: (i, 0))

    def head_spec():
        return pl.BlockSpec((1, MLA_HEADS, tm, HEAD_PAD), lambda i: (i // nt, 0, i % nt, 0))

    head_shape = jax.ShapeDtypeStruct((b, MLA_HEADS, s, HEAD_PAD), BF16)
    consts = (gmix, wall, gqa, wuq, gkva, wkv, gq, gk)
    side_specs = [_side_cast_spec(w.shape[0], w.shape[1], steps) for w in side_weights]
    outs = pl.pallas_call(
        _input_kernel,
        grid=(steps,),
        in_specs=[tok(D_MODEL), pl.BlockSpec((1, 1, tm), lambda i: (i, 0, 0))]
        + [_const_spec(c.shape) for c in consts] + side_specs,
        out_specs=[head_spec(), head_spec(), head_spec(),
                   tok(HG_WIDTH), tok(HG_WIDTH), tok(HG_WIDTH), tok(HG_WIDTH), tok(2 * D_MODEL)]
        + side_specs,
        out_shape=[head_shape, head_shape, head_shape,
                   jax.ShapeDtypeStruct((t, HG_WIDTH), BF16),
                   jax.ShapeDtypeStruct((t, HG_WIDTH), F32),
                   jax.ShapeDtypeStruct((t, HG_WIDTH), BF16),
                   jax.ShapeDtypeStruct((t, HG_WIDTH), BF16),
                   jax.ShapeDtypeStruct((t, 2 * D_MODEL), BF16)]
        + [jax.ShapeDtypeStruct(w.shape, BF16) for w in side_weights],
        compiler_params=pltpu.CompilerParams(
            dimension_semantics=("arbitrary",), vmem_limit_bytes=VMEM_LIMIT),
        name="input_stage",
    )(x2d, pos3d, *consts, *side_weights)
    return outs[:8], outs[8:]


def _attention_kernel(q_ref, k_ref, v_ref, o_ref, m_ref, acc_ref, s_ref):
    qi = pl.program_id(2)
    tq = ATT_TILE
    half = tq // 2
    m_ref[...] = jnp.full(m_ref.shape, -jnp.inf, F32)
    acc_ref[...] = jnp.zeros(acc_ref.shape, F32)

    def block(j, rows, n_keys, visible):
        start = pl.multiple_of(j * tq, tq)
        for hd in range(ATT_HEADS):
            s_ref[hd, rows, :n_keys] = _dot_nt(q_ref[0, hd, rows, :],
                                               k_ref[0, hd, pl.ds(start, n_keys), :])
        for hd in range(ATT_HEADS):
            sc = s_ref[hd, rows, :n_keys]
            if visible is not None:
                sc = jnp.where(visible, sc, -jnp.inf)
            m_old = m_ref[hd, rows, :]
            m_new = jnp.maximum(m_old, jnp.max(sc, axis=-1, keepdims=True))
            p = jnp.exp2(sc - jnp.concatenate([m_new] * (n_keys // LANES), axis=1)).astype(BF16)
            acc_ref[hd, rows, :] = (jnp.exp2(m_old - m_new) * acc_ref[hd, rows, :]
                                    + _dot(p, v_ref[0, hd, pl.ds(start, n_keys), :]))
            m_ref[hd, rows, :] = m_new

    def body(j, carry):
        block(j, slice(0, tq), tq, None)
        return carry

    lax.fori_loop(0, qi, body, 0)
    def lower(n_keys, shift):
        r_i = lax.broadcasted_iota(jnp.int32, (half, n_keys), 0)
        c_i = lax.broadcasted_iota(jnp.int32, (half, n_keys), 1)
        return c_i <= r_i + shift

    block(qi, slice(0, half), half, lower(half, 0))
    block(qi, slice(half, tq), tq, lower(tq, half))

    outs = []
    for hd in range(ATT_HEADS):
        acc = acc_ref[hd]
        outs.append(acc[:, :V_HEAD_DIM] / acc[:, DENOM_LANE:DENOM_LANE + 1])
    o_ref[0] = jnp.concatenate(outs, axis=-1).astype(BF16)


def _attention_stage(q, k, v):
    b, nh, s, _ = q.shape
    tq = ATT_TILE
    g = ATT_HEADS
    return pl.pallas_call(
        _attention_kernel,
        grid=(b, nh // g, s // tq),
        in_specs=[pl.BlockSpec((1, g, tq, HEAD_PAD), lambda bi, hp, i: (bi, hp, i, 0)),
                  pl.BlockSpec((1, g, s, HEAD_PAD), lambda bi, hp, i: (bi, hp, 0, 0)),
                  pl.BlockSpec((1, g, s, HEAD_PAD), lambda bi, hp, i: (bi, hp, 0, 0))],
        out_specs=pl.BlockSpec((1, tq, g * V_HEAD_DIM), lambda bi, hp, i: (bi, i, hp)),
        out_shape=jax.ShapeDtypeStruct((b, s, BRANCH_WIDTH), BF16),
        scratch_shapes=[pltpu.VMEM((g, tq, LANES), F32), pltpu.VMEM((g, tq, HEAD_PAD), F32),
                        pltpu.VMEM((g, tq, tq), F32)],
        compiler_params=pltpu.CompilerParams(
            dimension_semantics=("arbitrary", "arbitrary", "arbitrary"),
            vmem_limit_bytes=VMEM_LIMIT),
        name="causal_attention",
    )(q, k, v)


def _hgrn_span_matrix():
    c = HG_CHUNK
    x = np.arange(c)[:, None]
    y = np.arange(c)[None, :]
    mats = [(y <= x)]
    for b in HG_LEVELS:
        r = (x // (2 * b)) * (2 * b) + b
        mats.append((y > np.minimum(x, r)) & (y <= np.maximum(x, r)))
    m = np.concatenate(mats, axis=0).astype(np.float32)
    return np.concatenate([m, m], axis=1)


def _hgrn_chunk_local(q, hf, v_bf, lb, span, right_rows, pair_masks):
    c = HG_CHUNK
    f = lb + (1.0 - lb) * jax.nn.sigmoid(hf)
    lf = jnp.log2(f)
    kk = 1.0 - f
    lf_hi = lf.astype(BF16)
    lf_lo = (lf - lf_hi.astype(F32)).astype(BF16)
    spans = _dot(span, jnp.concatenate([lf_hi, lf_lo], axis=0))
    cum = spans[:c]
    last = cum[c - 1:c, :]
    q_dec = (q * jnp.exp2(cum)).astype(BF16)
    k_dec = (kk * jnp.exp2(last - cum)).astype(BF16)
    q_bf = q.astype(BF16)
    k_bf = kk.astype(BF16)
    zs = [(jnp.where(right_rows[lvl], q, kk)
           * jnp.exp2(spans[(lvl + 1) * c:(lvl + 2) * c])).astype(BF16)
          for lvl in range(len(HG_LEVELS))]

    o_intra, incr = [], []
    for hd in range(HG_HEADS):
        sl = slice(hd * HG_DIM, (hd + 1) * HG_DIM)
        att = jnp.where(pair_masks[0], _dot_nt(q_bf[:, sl], k_bf[:, sl]), 0.0)
        for lvl in range(len(HG_LEVELS)):
            z = zs[lvl][:, sl]
            att = jnp.where(pair_masks[lvl + 1], _dot_nt(z, z), att)
        v_h = v_bf[:, sl]
        o_intra.append(_dot(att.astype(BF16), v_h))
        incr.append(_dot(v_h.astype(F32).T.astype(BF16), k_dec[:, sl]))
    return o_intra, q_dec, incr, jnp.exp2(last)


def _hgrn_kernel(hq_ref, hf_ref, hi_ref, hg_ref, lb_ref, gout_ref, span_ref, o_ref, state_ref):
    @pl.when(pl.program_id(1) == 0)
    def _():
        state_ref[...] = jnp.zeros(state_ref.shape, F32)

    c = HG_CHUNK
    t_i = lax.broadcasted_iota(jnp.int32, (c, c), 0)
    s_i = lax.broadcasted_iota(jnp.int32, (c, c), 1)
    row = lax.broadcasted_iota(jnp.int32, (c, HG_WIDTH), 0)
    pair_masks = [t_i == s_i] + [((t_i // b) ^ (s_i // b) == 1) & (s_i < t_i) for b in HG_LEVELS]
    right_rows = [(row // b) % 2 == 1 for b in HG_LEVELS]
    span = span_ref[...]
    lb = lb_ref[...]
    n_chunks = HG_BLOCK // c

    local = []
    for ci in range(n_chunks):
        rows = slice(ci * c, (ci + 1) * c)
        local.append(_hgrn_chunk_local(hq_ref[0, rows, :].astype(F32), hf_ref[0, rows, :],
                                       hi_ref[0, rows, :], lb, span, right_rows, pair_masks))

    gout = gout_ref[...]
    for hd in range(HG_HEADS):
        sl = slice(hd * HG_DIM, (hd + 1) * HG_DIM)
        state_t = state_ref[hd]
        for ci in range(n_chunks):
            rows = slice(ci * c, (ci + 1) * c)
            o_intra, q_dec, incr, decay = local[ci]
            o = o_intra[hd] + _dot_nt(q_dec[:, sl], state_t.astype(BF16))
            state_t = state_t * decay[:, sl] + incr[hd]
            gate = hg_ref[0, rows, sl].astype(F32)
            o_ref[0, rows, sl] = (_rms(o, gout) * (gate * jax.nn.sigmoid(gate))).astype(BF16)
        state_ref[hd] = state_t


def _hgrn_stage(hq, hf, hi, hg, lb, gout):
    span = jnp.asarray(_hgrn_span_matrix(), BF16)
    b, s, _ = hq.shape
    tb = HG_BLOCK
    tok = pl.BlockSpec((1, tb, HG_WIDTH), lambda bi, i: (bi, i, 0))
    return pl.pallas_call(
        _hgrn_kernel,
        grid=(b, s // tb),
        in_specs=[tok, tok, tok, tok, _const_spec(lb.shape), _const_spec(gout.shape),
                  _const_spec(span.shape)],
        out_specs=tok,
        out_shape=jax.ShapeDtypeStruct((b, s, HG_WIDTH), BF16),
        scratch_shapes=[pltpu.VMEM((HG_HEADS, HG_DIM, HG_DIM), F32)],
        compiler_params=pltpu.CompilerParams(
            dimension_semantics=("arbitrary", "arbitrary"), vmem_limit_bytes=VMEM_LIMIT),
        name="hgrn2_recurrence",
    )(hq, hf, hi, hg, lb, gout, span)


def _output_kernel(x_ref, attn_ref, rec_ref, gates_ref, p_ref,
                   wb_ref, wout_ref, gffn_ref, wgate_ref, wup_ref, wdown_ref,
                   gpg_ref, wpg_ref, wpp_ref, gpost_ref, o_ref):
    y0 = _dot(attn_ref[...], wb_ref[:BRANCH_WIDTH, :])
    y1 = _dot(rec_ref[...], wb_ref[BRANCH_WIDTH:, :])
    g0 = gates_ref[:, :D_MODEL].astype(F32)
    g1 = gates_ref[:, D_MODEL:].astype(F32)
    merged = (g0 * y0 + g1 * y1).astype(BF16)
    x1 = x_ref[...] + _dot(merged, wout_ref[...])

    h2 = _rms(x1, gffn_ref[...]).astype(BF16)
    x2 = x1
    lo = 0
    for width in FFN_SPLITS:
        a = _dot(h2, wgate_ref[:, lo:lo + width])
        u = _dot(h2, wup_ref[:, lo:lo + width])
        z = (a * jax.nn.sigmoid(a) * u).astype(BF16)
        x2 = x2 + _dot(z, wdown_ref[lo:lo + width, :])
        lo += width

    e = _rms(_dot(p_ref[...].astype(BF16), wpp_ref[...]), gpost_ref[...])
    g = jax.nn.sigmoid(_dot(_rms(x2, gpg_ref[...]).astype(BF16), wpg_ref[...]))
    o_ref[...] = x2 + g * e


def _output_stage(x2d, attn, rec, gates, p2d, wb, wout, gffn, wgate, wup, wdown,
                  gpg, wpg, wpp, gpost):
    t = x2d.shape[0]
    tm = TOK_TILE

    def tok(width):
        return pl.BlockSpec((tm, width), lambda i: (i, 0))

    consts = (wb, wout, gffn, wgate, wup, wdown, gpg, wpg, wpp, gpost)
    return pl.pallas_call(
        _output_kernel,
        grid=(t // tm,),
        in_specs=[tok(D_MODEL), tok(BRANCH_WIDTH), tok(HG_WIDTH), tok(2 * D_MODEL), tok(PLE_DIM)]
        + [_const_spec(c.shape) for c in consts],
        out_specs=tok(D_MODEL),
        out_shape=jax.ShapeDtypeStruct((t, D_MODEL), F32),
        compiler_params=pltpu.CompilerParams(
            dimension_semantics=("arbitrary",), vmem_limit_bytes=VMEM_LIMIT),
        name="output_stage",
    )(x2d, attn, rec, gates, p2d, *consts)


def _pad_heads(w, heads, width):
    rows = w.shape[0]
    w = w.reshape(rows, heads, width)
    w = jnp.pad(w, ((0, 0), (0, 0), (0, HEAD_PAD - width)))
    return w.reshape(rows, heads * HEAD_PAD)


def _qk_head_layout(nope, rope):
    like = nope if nope is not None else rope

    def z(width):
        return jnp.zeros(like.shape[:-1] + (width,), like.dtype)

    n0 = nope[..., :HEAD_HALF_NOPE] if nope is not None else z(HEAD_HALF_NOPE)
    n1 = nope[..., HEAD_HALF_NOPE:] if nope is not None else z(HEAD_HALF_NOPE)
    x1 = rope[..., :ROPE_HALF] if rope is not None else z(ROPE_HALF)
    x2 = rope[..., ROPE_HALF:] if rope is not None else z(ROPE_HALF)
    return jnp.concatenate([n0, x1, z(ROPE_HALF), n1, x2, z(ROPE_HALF)], axis=-1)


def _row(v):
    return v.reshape(1, -1).astype(F32)


def kernel(x, p, positions, mix_norm_g, w_in, q_a_norm_g, w_uq, kv_a_norm_g, w_ukv, q_norm_g, k_norm_g, hg_lb_logits, hg_out_norm_g, w_branch, w_out, ffn_norm_g, w_ffn_gate, w_ffn_up, w_ffn_down, ple_gate_norm_g, w_ple_gate, w_ple_proj, ple_post_norm_g):
    b, s, d = x.shape
    depth = w_in.shape[0]
    lower_bounds = jnp.cumsum(jax.nn.softmax(hg_lb_logits.astype(F32), axis=0), axis=0)
    pos3d = positions.reshape(b * s // TOK_TILE, 1, TOK_TILE)
    x2d = x.reshape(b * s, d)
    q_scale = QK_HEAD_DIM ** -0.5 * math.log2(math.e)

    for layer in range(depth):
        wi = w_in[layer]
        wall = _input_weight_layout(wi.T)
        wq3 = w_uq[layer].reshape(Q_LORA_RANK, MLA_HEADS, QK_HEAD_DIM)
        wuq = _qk_head_layout(wq3[..., :QK_NOPE_DIM], wq3[..., QK_NOPE_DIM:])
        wuq = wuq.reshape(Q_LORA_RANK, MLA_HEADS * HEAD_PAD).astype(BF16)
        wukv = w_ukv[layer].reshape(KV_LORA_RANK, MLA_HEADS, QK_NOPE_DIM + V_HEAD_DIM)
        wk = _qk_head_layout(wukv[..., :QK_NOPE_DIM], None).reshape(KV_LORA_RANK, -1)
        wv = _pad_heads(wukv[:, :, QK_NOPE_DIM:].reshape(KV_LORA_RANK, -1), MLA_HEADS, V_HEAD_DIM)
        wkv = jnp.concatenate([wk, wv], axis=1).astype(BF16)
        gq = q_norm_g[layer].astype(F32) * q_scale
        gk = k_norm_g[layer].astype(F32)

        side = (w_branch[layer].reshape(2 * BRANCH_WIDTH, d), w_out[layer], w_ffn_gate[layer],
                w_ffn_up[layer], w_ffn_down[layer], w_ple_gate[layer], w_ple_proj[layer])
        (q, k, v, hq, hf, hi, hg, gates), side_bf = _input_stage(
            x2d, pos3d, b, s, _row(mix_norm_g[layer]), wall, _row(q_a_norm_g[layer]), wuq,
            _row(kv_a_norm_g[layer]), wkv,
            _row(_qk_head_layout(gq[:QK_NOPE_DIM], gq[QK_NOPE_DIM:])),
            _row(_qk_head_layout(gk[:QK_NOPE_DIM], gk[QK_NOPE_DIM:])), side)
        wb, wout, wgate, wup, wdown, wpg, wpp = side_bf

        attn = _attention_stage(q, k, v)
        rec = _hgrn_stage(hq.reshape(b, s, HG_WIDTH), hf.reshape(b, s, HG_WIDTH),
                          hi.reshape(b, s, HG_WIDTH), hg.reshape(b, s, HG_WIDTH),
                          _row(lower_bounds[layer]), _row(hg_out_norm_g[layer]))

        x2d = _output_stage(
            x2d, attn.reshape(b * s, BRANCH_WIDTH), rec.reshape(b * s, HG_WIDTH), gates,
            p[layer].reshape(b * s, PLE_DIM),
            wb, wout, _row(ffn_norm_g[layer]),
            wgate, wup, wdown, _row(ple_gate_norm_g[layer]), wpg, wpp,
            _row(ple_post_norm_g[layer]))
    return x2d.reshape(b, s, d)
```

```python
import functools
import math

import numpy as np

import jax
import jax.numpy as jnp
from jax import lax
from jax.experimental import pallas as pl
from jax.experimental.pallas import tpu as pltpu

D_MODEL = 1024
MLA_HEADS = 8
QK_NOPE_DIM = 64
QK_ROPE_DIM = 32
ROPE_HALF = QK_ROPE_DIM // 2
QK_HEAD_DIM = QK_NOPE_DIM + QK_ROPE_DIM
V_HEAD_DIM = 64
Q_LORA_RANK = 384
KV_LORA_RANK = 256
ROPE_BASE = 10000.0
HG_HEADS = 4
HG_DIM = 128
HG_WIDTH = HG_HEADS * HG_DIM
BRANCH_WIDTH = MLA_HEADS * V_HEAD_DIM
FFN_HIDDEN = 2816
PLE_DIM = 256
EPS = 1e-6

LANES = 128
BF16_ROWS = 16
HEAD_PAD = LANES
HEAD_HALF_NOPE = QK_NOPE_DIM // 2
W_LAT = Q_LORA_RANK + KV_LORA_RANK + HEAD_PAD
DENOM_LANE = V_HEAD_DIM

VMEM_LIMIT = 60 * 1024 * 1024

TOK_TILE = 512
ATT_TILE = 512
ATT_HEADS = 8
HG_BLOCK = 512
HG_CHUNK = 64
HG_LEVELS = tuple(HG_CHUNK >> (i + 1) for i in range(HG_CHUNK.bit_length() - 1))
FFN_SPLITS = (1024, 1024, 768)

BF16 = jnp.bfloat16
F32 = jnp.float32


def _const_spec(shape):
    nd = len(shape)
    return pl.BlockSpec(shape, lambda *_: (0,) * nd, pipeline_mode=pl.Buffered(1))


def _rms(v, gain, width=None):
    n = v.shape[-1] if width is None else width
    ms = jnp.sum(v * v, axis=-1, keepdims=True) * (1.0 / n)
    return v * lax.rsqrt(ms + EPS) * gain


def _dot(a, b):
    return jnp.dot(a, b, preferred_element_type=F32)


def _dot_nt(a, b):
    return lax.dot_general(a, b, (((1,), (1,)), ((), ())), preferred_element_type=F32)


def _rope_tables(pos_row):
    tm = pos_row.shape[1]
    fidx = lax.broadcasted_iota(jnp.int32, (ROPE_HALF, 1), 0).astype(F32)
    inv_freq = jnp.exp(fidx * (-math.log(ROPE_BASE) * 2.0 / QK_ROPE_DIM))
    ang = pos_row.astype(F32) * inv_freq
    cos = jnp.cos(ang)
    sin = jnp.sin(ang)
    ones = jnp.ones((HEAD_HALF_NOPE, tm), F32)
    zeros_n = jnp.zeros((HEAD_HALF_NOPE, tm), F32)
    zeros_p = jnp.zeros((ROPE_HALF, tm), F32)
    cos_t = jnp.concatenate([ones, cos, zeros_p, ones, cos, zeros_p], axis=0)
    sin_t = jnp.concatenate([zeros_n, -sin, zeros_p, zeros_n, sin, zeros_p], axis=0)
    return cos_t.T, sin_t.T


def _input_kernel(x_ref, pos_ref, gmix_ref, wall_ref, gqa_ref, wuq_ref, gkva_ref, wkv_ref,
                  gq_ref, gk_ref, *rest):
    n_side = (len(rest) - 8) // 2
    side_in = rest[:n_side]
    q_ref, k_ref, v_ref, hq_ref, hf_ref, hi_ref, hg_ref, gates_ref = rest[n_side:n_side + 8]
    side_out = rest[n_side + 8:]
    for w_ref, wb_ref in zip(side_in, side_out):
        wb_ref[...] = w_ref[...].astype(BF16)

    x = x_ref[...]
    h = _rms(x, gmix_ref[...]).astype(BF16)

    lat = _dot(h, wall_ref[:, :W_LAT])
    cq = _rms(lat[:, :Q_LORA_RANK], gqa_ref[...]).astype(BF16)
    ckv = _rms(lat[:, Q_LORA_RANK:Q_LORA_RANK + KV_LORA_RANK], gkva_ref[...]).astype(BF16)
    k_rope = lat[:, Q_LORA_RANK + KV_LORA_RANK:]

    cos_tab, sin_tab = _rope_tables(pos_ref[0])
    gq = gq_ref[...]
    gk = gk_ref[...]
    q_cos = cos_tab * gq
    q_sin = sin_tab * pltpu.roll(gq, LANES // 2, 1)
    kr = k_rope * gk
    kr = kr * cos_tab + pltpu.roll(kr, LANES // 2, 1) * sin_tab
    kr_ss = jnp.sum(k_rope * k_rope, axis=-1, keepdims=True)

    q_all = _dot(cq, wuq_ref[...])
    kv_all = _dot(ckv, wkv_ref[...])
    lane = lax.broadcasted_iota(jnp.int32, (1, LANES), 1)
    one_lane = (lane == DENOM_LANE).astype(F32)
    inv_width = 1.0 / QK_HEAD_DIM
    for hd in range(MLA_HEADS):
        sl = slice(hd * HEAD_PAD, (hd + 1) * HEAD_PAD)
        t = q_all[:, sl]
        r = lax.rsqrt(jnp.sum(t * t, axis=-1, keepdims=True) * inv_width + EPS)
        q_ref[0, hd] = ((t * q_cos + pltpu.roll(t, LANES // 2, 1) * q_sin) * r).astype(BF16)
        t = kv_all[:, sl]
        r = lax.rsqrt((jnp.sum(t * t, axis=-1, keepdims=True) + kr_ss) * inv_width + EPS)
        k_ref[0, hd] = ((t * gk + kr) * r).astype(BF16)
        vsl = slice(MLA_HEADS * HEAD_PAD + hd * HEAD_PAD, MLA_HEADS * HEAD_PAD + (hd + 1) * HEAD_PAD)
        v_ref[0, hd] = (kv_all[:, vsl] + one_lane).astype(BF16)

    hh = _dot(h, wall_ref[:, W_LAT:W_LAT + 4 * HG_WIDTH])
    hq_ref[...] = hh[:, :HG_WIDTH].astype(BF16)
    hf_ref[...] = hh[:, HG_WIDTH:2 * HG_WIDTH]
    hi_ref[...] = hh[:, 2 * HG_WIDTH:3 * HG_WIDTH].astype(BF16)
    hg_ref[...] = hh[:, 3 * HG_WIDTH:].astype(BF16)

    gates_ref[...] = jax.nn.sigmoid(_dot(h, wall_ref[:, W_LAT + 4 * HG_WIDTH:])).astype(BF16)


W_PREP_COLS = 256


def _input_weight_kernel(wt_ref, o_ref):
    c0 = Q_LORA_RANK + KV_LORA_RANK
    j = pl.program_id(0)
    n = W_PREP_COLS
    lat_full = c0 // n
    lat_rest = c0 - lat_full * n

    @pl.when(j != lat_full)
    def _():
        o_ref[...] = wt_ref[...].T.astype(BF16)

    @pl.when(j == lat_full)
    def _():
        d = wt_ref.shape[1]
        z_n = jnp.zeros((HEAD_HALF_NOPE, d), F32)
        z_p = jnp.zeros((ROPE_HALF, d), F32)
        blk = jnp.concatenate([wt_ref[:lat_rest, :], z_n, wt_ref[lat_rest:lat_rest + ROPE_HALF, :], z_p,
                               z_n, wt_ref[lat_rest + ROPE_HALF:lat_rest + QK_ROPE_DIM, :], z_p],
                              axis=0)
        o_ref[...] = blk.T.astype(BF16)


def _input_weight_layout(wt):
    cols, rows = wt.shape
    out_cols = cols - QK_ROPE_DIM + HEAD_PAD
    n = W_PREP_COLS
    lat_full = (Q_LORA_RANK + KV_LORA_RANK) // n
    pad = HEAD_PAD - QK_ROPE_DIM

    def in_rows(j):
        return (pl.multiple_of(jnp.where(j <= lat_full, j * n, j * n - pad), QK_ROPE_DIM), 0)

    return pl.pallas_call(
        _input_weight_kernel,
        grid=(out_cols // n,),
        in_specs=[pl.BlockSpec((pl.Element(n), pl.Element(rows)), in_rows)],
        out_specs=pl.BlockSpec((rows, n), lambda j: (0, j)),
        out_shape=jax.ShapeDtypeStruct((rows, out_cols), BF16),
        compiler_params=pltpu.CompilerParams(
            dimension_semantics=("arbitrary",), vmem_limit_bytes=VMEM_LIMIT),
        name="input_weight_layout",
    )(wt)


def _side_cast_spec(rows, cols, steps):
    for share in (1, 2):
        blk, rem = divmod(rows * share, steps)
        if rem == 0 and blk % BF16_ROWS == 0:
            return pl.BlockSpec((blk, cols), lambda i: (i // share, 0))
    raise ValueError(f"no row block for a ({rows},{cols}) weight over {steps} steps")


def _input_stage(x2d, pos3d, b, s, gmix, wall, gqa, wuq, gkva, wkv, gq, gk, side_weights):
    t = x2d.shape[0]
    tm = TOK_TILE
    nt = s // tm
    steps = t // tm

    def tok(width):
        return pl.BlockSpec((tm, width), lambda i: (i, 0))

    def head_spec():
        return pl.BlockSpec((1, MLA_HEADS, tm, HEAD_PAD), lambda i: (i // nt, 0, i % nt, 0))

    head_shape = jax.ShapeDtypeStruct((b, MLA_HEADS, s, HEAD_PAD), BF16)
    consts = (gmix, wall, gqa, wuq, gkva, wkv, gq, gk)
    side_specs = [_side_cast_spec(w.shape[0], w.shape[1], steps) for w in side_weights]
    outs = pl.pallas_call(
        _input_kernel,
        grid=(steps,),
        in_specs=[tok(D_MODEL), pl.BlockSpec((1, 1, tm), lambda i: (i, 0, 0))]
        + [_const_spec(c.shape) for c in consts] + side_specs,
        out_specs=[head_spec(), head_spec(), head_spec(),
                   tok(HG_WIDTH), tok(HG_WIDTH), tok(HG_WIDTH), tok(HG_WIDTH), tok(2 * D_MODEL)]
        + side_specs,
        out_shape=[head_shape, head_shape, head_shape,
                   jax.ShapeDtypeStruct((t, HG_WIDTH), BF16),
                   jax.ShapeDtypeStruct((t, HG_WIDTH), F32),
                   jax.ShapeDtypeStruct((t, HG_WIDTH), BF16),
                   jax.ShapeDtypeStruct((t, HG_WIDTH), BF16),
                   jax.ShapeDtypeStruct((t, 2 * D_MODEL), BF16)]
        + [jax.ShapeDtypeStruct(w.shape, BF16) for w in side_weights],
        compiler_params=pltpu.CompilerParams(
            dimension_semantics=("arbitrary",), vmem_limit_bytes=VMEM_LIMIT),
        name="input_stage",
    )(x2d, pos3d, *consts, *side_weights)
    return outs[:8], outs[8:]


def _attention_kernel(q_ref, k_ref, v_ref, o_ref, m_ref, acc_ref, s_ref):
    qi = pl.program_id(2)
    tq = ATT_TILE
    half = tq // 2
    m_ref[...] = jnp.full(m_ref.shape, -jnp.inf, F32)
    acc_ref[...] = jnp.zeros(acc_ref.shape, F32)

    def block(j, rows, n_keys, visible):
        start = pl.multiple_of(j * tq, tq)
        for hd in range(ATT_HEADS):
            s_ref[hd, rows, :n_keys] = _dot_nt(q_ref[0, hd, rows, :],
                                               k_ref[0, hd, pl.ds(start, n_keys), :])
        for hd in range(ATT_HEADS):
            sc = s_ref[hd, rows, :n_keys]
            if visible is not None:
                sc = jnp.where(visible, sc, -jnp.inf)
            m_old = m_ref[hd, rows, :]
            m_new = jnp.maximum(m_old, jnp.max(sc, axis=-1, keepdims=True))
            p = jnp.exp2(sc - jnp.concatenate([m_new] * (n_keys // LANES), axis=1)).astype(BF16)
            acc_ref[hd, rows, :] = (jnp.exp2(m_old - m_new) * acc_ref[hd, rows, :]
                                    + _dot(p, v_ref[0, hd, pl.ds(start, n_keys), :]))
            m_ref[hd, rows, :] = m_new

    def body(j, carry):
        block(j, slice(0, tq), tq, None)
        return carry

    lax.fori_loop(0, qi, body, 0)
    def lower(n_keys, shift):
        r_i = lax.broadcasted_iota(jnp.int32, (half, n_keys), 0)
        c_i = lax.broadcasted_iota(jnp.int32, (half, n_keys), 1)
        return c_i <= r_i + shift

    block(qi, slice(0, half), half, lower(half, 0))
    block(qi, slice(half, tq), tq, lower(tq, half))

    outs = []
    for hd in range(ATT_HEADS):
        acc = acc_ref[hd]
        outs.append(acc[:, :V_HEAD_DIM] / acc[:, DENOM_LANE:DENOM_LANE + 1])
    o_ref[0] = jnp.concatenate(outs, axis=-1).astype(BF16)


def _attention_stage(q, k, v):
    b, nh, s, _ = q.shape
    tq = ATT_TILE
    g = ATT_HEADS
    return pl.pallas_call(
        _attention_kernel,
        grid=(b, nh // g, s // tq),
        in_specs=[pl.BlockSpec((1, g, tq, HEAD_PAD), lambda bi, hp, i: (bi, hp, i, 0)),
                  pl.BlockSpec((1, g, s, HEAD_PAD), lambda bi, hp, i: (bi, hp, 0, 0)),
                  pl.BlockSpec((1, g, s, HEAD_PAD), lambda bi, hp, i: (bi, hp, 0, 0))],
        out_specs=pl.BlockSpec((1, tq, g * V_HEAD_DIM), lambda bi, hp, i: (bi, i, hp)),
        out_shape=jax.ShapeDtypeStruct((b, s, BRANCH_WIDTH), BF16),
        scratch_shapes=[pltpu.VMEM((g, tq, LANES), F32), pltpu.VMEM((g, tq, HEAD_PAD), F32),
                        pltpu.VMEM((g, tq, tq), F32)],
        compiler_params=pltpu.CompilerParams(
            dimension_semantics=("arbitrary", "arbitrary", "arbitrary"),
            vmem_limit_bytes=VMEM_LIMIT),
        name="causal_attention",
    )(q, k, v)


def _hgrn_span_matrix():
    c = HG_CHUNK
    x = np.arange(c)[:, None]
    y = np.arange(c)[None, :]
    mats = [(y <= x)]
    for b in HG_LEVELS:
        r = (x // (2 * b)) * (2 * b) + b
        mats.append((y > np.minimum(x, r)) & (y <= np.maximum(x, r)))
    m = np.concatenate(mats, axis=0).astype(np.float32)
    return np.concatenate([m, m], axis=1)


def _block_diag_rows(x):
    a, b = x[:, :HG_DIM], x[:, HG_DIM:]
    zero = jnp.zeros_like(a)
    return jnp.concatenate([jnp.concatenate([a, zero], axis=1),
                            jnp.concatenate([zero, b], axis=1)], axis=0)


def _hgrn_chunk_local(q, hf, v_bf, lb, span, right_rows, pair_masks):
    c = HG_CHUNK
    f = lb + (1.0 - lb) * jax.nn.sigmoid(hf)
    lf = jnp.log2(f)
    kk = 1.0 - f
    lf_hi = lf.astype(BF16)
    lf_lo = (lf - lf_hi.astype(F32)).astype(BF16)
    spans = _dot(span, jnp.concatenate([lf_hi, lf_lo], axis=0))
    cum = spans[:c]
    last = cum[c - 1:c, :]
    q_dec = (q * jnp.exp2(cum)).astype(BF16)
    k_dec = (kk * jnp.exp2(last - cum)).astype(BF16)
    q_bf = q.astype(BF16)
    k_bf = kk.astype(BF16)
    zs = [(jnp.where(right_rows[lvl], q, kk)
           * jnp.exp2(spans[(lvl + 1) * c:(lvl + 2) * c])).astype(BF16)
          for lvl in range(len(HG_LEVELS))]

    o_intra, incr = [], []
    for pair in range(HG_HEADS // 2):
        psl = slice(2 * pair * HG_DIM, (2 * pair + 2) * HG_DIM)
        att = jnp.where(pair_masks[0], _dot_nt(q_bf[:, psl], _block_diag_rows(k_bf[:, psl])), 0.0)
        for lvl in range(len(HG_LEVELS)):
            z = zs[lvl][:, psl]
            att = jnp.where(pair_masks[lvl + 1], _dot_nt(z, _block_diag_rows(z)), att)
        v_pair = v_bf[:, psl]
        o_intra.append(_dot(att.astype(BF16), _block_diag_rows(v_pair)))
        for hd in (2 * pair, 2 * pair + 1):
            sl = slice(hd * HG_DIM, (hd + 1) * HG_DIM)
            incr.append(_dot(v_bf[:, sl].astype(F32).T.astype(BF16), k_dec[:, sl]))
    return o_intra, q_dec, incr, jnp.exp2(last)


def _hgrn_kernel(hq_ref, hf_ref, hi_ref, hg_ref, lb_ref, gout_ref, span_ref, o_ref, state_ref):
    @pl.when(pl.program_id(1) == 0)
    def _():
        state_ref[...] = jnp.zeros(state_ref.shape, F32)

    c = HG_CHUNK
    t_i = lax.broadcasted_iota(jnp.int32, (c, 2 * c), 0)
    s_i = lax.broadcasted_iota(jnp.int32, (c, 2 * c), 1) % c
    row = lax.broadcasted_iota(jnp.int32, (c, HG_WIDTH), 0)
    pair_masks = [t_i == s_i] + [((t_i // b) ^ (s_i // b) == 1) & (s_i < t_i) for b in HG_LEVELS]
    right_rows = [(row // b) % 2 == 1 for b in HG_LEVELS]
    span = span_ref[...]
    lb = lb_ref[...]
    n_chunks = HG_BLOCK // c

    local = []
    for ci in range(n_chunks):
        rows = slice(ci * c, (ci + 1) * c)
        local.append(_hgrn_chunk_local(hq_ref[0, rows, :].astype(F32), hf_ref[0, rows, :],
                                       hi_ref[0, rows, :], lb, span, right_rows, pair_masks))

    gout = gout_ref[...]
    for pair in range(HG_HEADS // 2):
        hds = (2 * pair, 2 * pair + 1)
        psl = slice(2 * pair * HG_DIM, (2 * pair + 2) * HG_DIM)
        states = [state_ref[hd] for hd in hds]
        for ci in range(n_chunks):
            rows = slice(ci * c, (ci + 1) * c)
            o_intra, q_dec, incr, decay = local[ci]
            both = jnp.concatenate([st.astype(BF16) for st in states], axis=1)
            o_pair = o_intra[pair] + _dot_nt(q_dec[:, psl], _block_diag_rows(both))
            for n, hd in enumerate(hds):
                sl = slice(hd * HG_DIM, (hd + 1) * HG_DIM)
                states[n] = states[n] * decay[:, sl] + incr[hd]
                o = o_pair[:, n * HG_DIM:(n + 1) * HG_DIM]
                gate = hg_ref[0, rows, sl].astype(F32)
                o_ref[0, rows, sl] = (_rms(o, gout) * (gate * jax.nn.sigmoid(gate))).astype(BF16)
        for n, hd in enumerate(hds):
            state_ref[hd] = states[n]


def _hgrn_stage(hq, hf, hi, hg, lb, gout):
    span = jnp.asarray(_hgrn_span_matrix(), BF16)
    b, s, _ = hq.shape
    tb = HG_BLOCK
    tok = pl.BlockSpec((1, tb, HG_WIDTH), lambda bi, i: (bi, i, 0))
    return pl.pallas_call(
        _hgrn_kernel,
        grid=(b, s // tb),
        in_specs=[tok, tok, tok, tok, _const_spec(lb.shape), _const_spec(gout.shape),
                  _const_spec(span.shape)],
        out_specs=tok,
        out_shape=jax.ShapeDtypeStruct((b, s, HG_WIDTH), BF16),
        scratch_shapes=[pltpu.VMEM((HG_HEADS, HG_DIM, HG_DIM), F32)],
        compiler_params=pltpu.CompilerParams(
            dimension_semantics=("arbitrary", "arbitrary"), vmem_limit_bytes=VMEM_LIMIT),
        name="hgrn2_recurrence",
    )(hq, hf, hi, hg, lb, gout, span)


def _output_kernel(x_ref, attn_ref, rec_ref, gates_ref, p_ref,
                   wb_ref, wout_ref, gffn_ref, wgate_ref, wup_ref, wdown_ref,
                   gpg_ref, wpg_ref, wpp_ref, gpost_ref, o_ref):
    y0 = _dot(attn_ref[...], wb_ref[:BRANCH_WIDTH, :])
    y1 = _dot(rec_ref[...], wb_ref[BRANCH_WIDTH:, :])
    g0 = gates_ref[:, :D_MODEL].astype(F32)
    g1 = gates_ref[:, D_MODEL:].astype(F32)
    merged = (g0 * y0 + g1 * y1).astype(BF16)
    x1 = x_ref[...] + _dot(merged, wout_ref[...])

    e = _rms(_dot(p_ref[...].astype(BF16), wpp_ref[...]), gpost_ref[...])
    h2 = _rms(x1, gffn_ref[...]).astype(BF16)
    x2 = x1
    lo = 0
    for width in FFN_SPLITS:
        a = _dot(h2, wgate_ref[:, lo:lo + width])
        u = _dot(h2, wup_ref[:, lo:lo + width])
        z = (a * jax.nn.sigmoid(a) * u).astype(BF16)
        x2 = x2 + _dot(z, wdown_ref[lo:lo + width, :])
        lo += width

    g = jax.nn.sigmoid(_dot(_rms(x2, gpg_ref[...]).astype(BF16), wpg_ref[...]))
    o_ref[...] = x2 + g * e


def _output_stage(x2d, attn, rec, gates, p2d, wb, wout, gffn, wgate, wup, wdown,
                  gpg, wpg, wpp, gpost):
    t = x2d.shape[0]
    tm = TOK_TILE

    def tok(width):
        return pl.BlockSpec((tm, width), lambda i: (i, 0))

    consts = (wb, wout, gffn, wgate, wup, wdown, gpg, wpg, wpp, gpost)
    return pl.pallas_call(
        _output_kernel,
        grid=(t // tm,),
        in_specs=[tok(D_MODEL), tok(BRANCH_WIDTH), tok(HG_WIDTH), tok(2 * D_MODEL), tok(PLE_DIM)]
        + [_const_spec(c.shape) for c in consts],
        out_specs=tok(D_MODEL),
        out_shape=jax.ShapeDtypeStruct((t, D_MODEL), F32),
        compiler_params=pltpu.CompilerParams(
            dimension_semantics=("arbitrary",), vmem_limit_bytes=VMEM_LIMIT),
        name="output_stage",
    )(x2d, attn, rec, gates, p2d, *consts)


def _pad_heads(w, heads, width):
    rows = w.shape[0]
    w = w.reshape(rows, heads, width)
    w = jnp.pad(w, ((0, 0), (0, 0), (0, HEAD_PAD - width)))
    return w.reshape(rows, heads * HEAD_PAD)


def _qk_head_layout(nope, rope):
    like = nope if nope is not None else rope

    def z(width):
        return jnp.zeros(like.shape[:-1] + (width,), like.dtype)

    n0 = nope[..., :HEAD_HALF_NOPE] if nope is not None else z(HEAD_HALF_NOPE)
    n1 = nope[..., HEAD_HALF_NOPE:] if nope is not None else z(HEAD_HALF_NOPE)
    x1 = rope[..., :ROPE_HALF] if rope is not None else z(ROPE_HALF)
    x2 = rope[..., ROPE_HALF:] if rope is not None else z(ROPE_HALF)
    return jnp.concatenate([n0, x1, z(ROPE_HALF), n1, x2, z(ROPE_HALF)], axis=-1)


def _row(v):
    return v.reshape(1, -1).astype(F32)


def kernel(x, p, positions, mix_norm_g, w_in, q_a_norm_g, w_uq, kv_a_norm_g, w_ukv, q_norm_g, k_norm_g, hg_lb_logits, hg_out_norm_g, w_branch, w_out, ffn_norm_g, w_ffn_gate, w_ffn_up, w_ffn_down, ple_gate_norm_g, w_ple_gate, w_ple_proj, ple_post_norm_g):
    b, s, d = x.shape
    depth = w_in.shape[0]
    lower_bounds = jnp.cumsum(jax.nn.softmax(hg_lb_logits.astype(F32), axis=0), axis=0)
    pos3d = positions.reshape(b * s // TOK_TILE, 1, TOK_TILE)
    x2d = x.reshape(b * s, d)
    q_scale = QK_HEAD_DIM ** -0.5 * math.log2(math.e)

    for layer in range(depth):
        wi = w_in[layer]
        wall = _input_weight_layout(wi.T)
        wq3 = w_uq[layer].reshape(Q_LORA_RANK, MLA_HEADS, QK_HEAD_DIM)
        wuq = _qk_head_layout(wq3[..., :QK_NOPE_DIM], wq3[..., QK_NOPE_DIM:])
        wuq = wuq.reshape(Q_LORA_RANK, MLA_HEADS * HEAD_PAD).astype(BF16)
        wukv = w_ukv[layer].reshape(KV_LORA_RANK, MLA_HEADS, QK_NOPE_DIM + V_HEAD_DIM)
        wk = _qk_head_layout(wukv[..., :QK_NOPE_DIM], None).reshape(KV_LORA_RANK, -1)
        wv = _pad_heads(wukv[:, :, QK_NOPE_DIM:].reshape(KV_LORA_RANK, -1), MLA_HEADS, V_HEAD_DIM)
        wkv = jnp.concatenate([wk, wv], axis=1).astype(BF16)
        gq = q_norm_g[layer].astype(F32) * q_scale
        gk = k_norm_g[layer].astype(F32)

        side = (w_branch[layer].reshape(2 * BRANCH_WIDTH, d), w_out[layer], w_ffn_gate[layer],
                w_ffn_up[layer], w_ffn_down[layer], w_ple_gate[layer], w_ple_proj[layer])
        (q, k, v, hq, hf, hi, hg, gates), side_bf = _input_stage(
            x2d, pos3d, b, s, _row(mix_norm_g[layer]), wall, _row(q_a_norm_g[layer]), wuq,
            _row(kv_a_norm_g[layer]), wkv,
            _row(_qk_head_layout(gq[:QK_NOPE_DIM], gq[QK_NOPE_DIM:])),
            _row(_qk_head_layout(gk[:QK_NOPE_DIM], gk[QK_NOPE_DIM:])), side)
        wb, wout, wgate, wup, wdown, wpg, wpp = side_bf

        attn = _attention_stage(q, k, v)
        rec = _hgrn_stage(hq.reshape(b, s, HG_WIDTH), hf.reshape(b, s, HG_WIDTH),
                          hi.reshape(b, s, HG_WIDTH), hg.reshape(b, s, HG_WIDTH),
                          _row(lower_bounds[layer]), _row(hg_out_norm_g[layer]))

        x2d = _output_stage(
            x2d, attn.reshape(b * s, BRANCH_WIDTH), rec.reshape(b * s, HG_WIDTH), gates,
            p[layer].reshape(b * s, PLE_DIM),
            wb, wout, _row(ffn_norm_g[layer]),
            wgate, wup, wdown, _row(ple_gate_norm_g[layer]), wpg, wpp,
            _row(ple_post_norm_g[layer]))
    return x2d.reshape(b, s, d)
```

```python
import functools
import math

import numpy as np

import jax
import jax.numpy as jnp
from jax import lax
from jax.experimental import pallas as pl
from jax.experimental.pallas import tpu as pltpu

D_MODEL = 1024
MLA_HEADS = 8
QK_NOPE_DIM = 64
QK_ROPE_DIM = 32
ROPE_HALF = QK_ROPE_DIM // 2
QK_HEAD_DIM = QK_NOPE_DIM + QK_ROPE_DIM
V_HEAD_DIM = 64
Q_LORA_RANK = 384
KV_LORA_RANK = 256
ROPE_BASE = 10000.0
HG_HEADS = 4
HG_DIM = 128
HG_WIDTH = HG_HEADS * HG_DIM
BRANCH_WIDTH = MLA_HEADS * V_HEAD_DIM
FFN_HIDDEN = 2816
PLE_DIM = 256
EPS = 1e-6

LANES = 128
BF16_ROWS = 16
HEAD_PAD = LANES
HEAD_HALF_NOPE = QK_NOPE_DIM // 2
W_LAT = Q_LORA_RANK + KV_LORA_RANK + HEAD_PAD
DENOM_LANE = V_HEAD_DIM

VMEM_LIMIT = 60 * 1024 * 1024

TOK_TILE = 512
ATT_TILE = 512
ATT_HEADS = 8
HG_BLOCK = 1024
HG_CHUNK = 64
HG_LEVELS = tuple(HG_CHUNK >> (i + 1) for i in range(HG_CHUNK.bit_length() - 1))
FFN_SPLITS = (1024, 1024, 768)

BF16 = jnp.bfloat16
F32 = jnp.float32


def _const_spec(shape):
    nd = len(shape)
    return pl.BlockSpec(shape, lambda *_: (0,) * nd, pipeline_mode=pl.Buffered(1))


def _rms(v, gain, width=None):
    n = v.shape[-1] if width is None else width
    ms = jnp.sum(v * v, axis=-1, keepdims=True) * (1.0 / n)
    return v * lax.rsqrt(ms + EPS) * gain


def _dot(a, b):
    return jnp.dot(a, b, preferred_element_type=F32)


def _dot_nt(a, b):
    return lax.dot_general(a, b, (((1,), (1,)), ((), ())), preferred_element_type=F32)


def _rope_tables(pos_row):
    tm = pos_row.shape[1]
    fidx = lax.broadcasted_iota(jnp.int32, (ROPE_HALF, 1), 0).astype(F32)
    inv_freq = jnp.exp(fidx * (-math.log(ROPE_BASE) * 2.0 / QK_ROPE_DIM))
    ang = pos_row.astype(F32) * inv_freq
    cos = jnp.cos(ang)
    sin = jnp.sin(ang)
    ones = jnp.ones((HEAD_HALF_NOPE, tm), F32)
    zeros_n = jnp.zeros((HEAD_HALF_NOPE, tm), F32)
    zeros_p = jnp.zeros((ROPE_HALF, tm), F32)
    cos_t = jnp.concatenate([ones, cos, zeros_p, ones, cos, zeros_p], axis=0)
    sin_t = jnp.concatenate([zeros_n, -sin, zeros_p, zeros_n, sin, zeros_p], axis=0)
    return cos_t.T, sin_t.T


def _input_kernel(x_ref, pos_ref, gmix_ref, wall_ref, gqa_ref, wuq_ref, gkva_ref, wkv_ref,
                  gq_ref, gk_ref, *rest):
    n_side = (len(rest) - 8) // 2
    side_in = rest[:n_side]
    q_ref, k_ref, v_ref, hq_ref, hf_ref, hi_ref, hg_ref, gates_ref = rest[n_side:n_side + 8]
    side_out = rest[n_side + 8:]
    for w_ref, wb_ref in zip(side_in, side_out):
        wb_ref[...] = w_ref[...].astype(BF16)

    x = x_ref[...]
    h = _rms(x, gmix_ref[...]).astype(BF16)

    lat = _dot(h, wall_ref[:, :W_LAT])
    cq = _rms(lat[:, :Q_LORA_RANK], gqa_ref[...]).astype(BF16)
    ckv = _rms(lat[:, Q_LORA_RANK:Q_LORA_RANK + KV_LORA_RANK], gkva_ref[...]).astype(BF16)
    k_rope = lat[:, Q_LORA_RANK + KV_LORA_RANK:]

    cos_tab, sin_tab = _rope_tables(pos_ref[0])
    gq = gq_ref[...]
    gk = gk_ref[...]
    q_cos = cos_tab * gq
    q_sin = sin_tab * pltpu.roll(gq, LANES // 2, 1)
    kr = k_rope * gk
    kr = kr * cos_tab + pltpu.roll(kr, LANES // 2, 1) * sin_tab
    kr_ss = jnp.sum(k_rope * k_rope, axis=-1, keepdims=True)

    q_all = _dot(cq, wuq_ref[...])
    kv_all = _dot(ckv, wkv_ref[...])
    lane = lax.broadcasted_iota(jnp.int32, (1, LANES), 1)
    one_lane = (lane == DENOM_LANE).astype(F32)
    inv_width = 1.0 / QK_HEAD_DIM
    for hd in range(MLA_HEADS):
        sl = slice(hd * HEAD_PAD, (hd + 1) * HEAD_PAD)
        t = q_all[:, sl]
        r = lax.rsqrt(jnp.sum(t * t, axis=-1, keepdims=True) * inv_width + EPS)
        q_ref[0, hd] = ((t * q_cos + pltpu.roll(t, LANES // 2, 1) * q_sin) * r).astype(BF16)
        t = kv_all[:, sl]
        r = lax.rsqrt((jnp.sum(t * t, axis=-1, keepdims=True) + kr_ss) * inv_width + EPS)
        k_ref[0, hd] = ((t * gk + kr) * r).astype(BF16)
        vsl = slice(MLA_HEADS * HEAD_PAD + hd * HEAD_PAD, MLA_HEADS * HEAD_PAD + (hd + 1) * HEAD_PAD)
        v_ref[0, hd] = (kv_all[:, vsl] + one_lane).astype(BF16)

    hh = _dot(h, wall_ref[:, W_LAT:W_LAT + 4 * HG_WIDTH])
    hq_ref[...] = hh[:, :HG_WIDTH].astype(BF16)
    hf_ref[...] = hh[:, HG_WIDTH:2 * HG_WIDTH]
    hi_ref[...] = hh[:, 2 * HG_WIDTH:3 * HG_WIDTH].astype(BF16)
    hg_ref[...] = hh[:, 3 * HG_WIDTH:].astype(BF16)

    gates_ref[...] = jax.nn.sigmoid(_dot(h, wall_ref[:, W_LAT + 4 * HG_WIDTH:])).astype(BF16)


W_PREP_COLS = 2432


def _input_weight_kernel(wt_ref, o_ref):
    c0 = Q_LORA_RANK + KV_LORA_RANK
    j = pl.program_id(0)
    n = W_PREP_COLS
    lat_full = c0 // n
    lat_rest = c0 - lat_full * n

    @pl.when(j != lat_full)
    def _():
        o_ref[...] = wt_ref[...].T.astype(BF16)

    @pl.when(j == lat_full)
    def _():
        d = wt_ref.shape[1]
        z_n = jnp.zeros((HEAD_HALF_NOPE, d), F32)
        z_p = jnp.zeros((ROPE_HALF, d), F32)
        parts = [wt_ref[:lat_rest, :], z_n, wt_ref[lat_rest:lat_rest + ROPE_HALF, :], z_p,
                 z_n, wt_ref[lat_rest + ROPE_HALF:lat_rest + QK_ROPE_DIM, :], z_p]
        tail = n - lat_rest - HEAD_PAD
        if tail:
            parts.append(wt_ref[lat_rest + QK_ROPE_DIM:lat_rest + QK_ROPE_DIM + tail, :])
        o_ref[...] = jnp.concatenate(parts, axis=0).T.astype(BF16)


def _input_weight_layout(wt):
    cols, rows = wt.shape
    out_cols = cols - QK_ROPE_DIM + HEAD_PAD
    n = W_PREP_COLS
    lat_full = (Q_LORA_RANK + KV_LORA_RANK) // n
    pad = HEAD_PAD - QK_ROPE_DIM

    def in_rows(j):
        return (pl.multiple_of(jnp.where(j <= lat_full, j * n, j * n - pad), QK_ROPE_DIM), 0)

    return pl.pallas_call(
        _input_weight_kernel,
        grid=(out_cols // n,),
        in_specs=[pl.BlockSpec((pl.Element(n), pl.Element(rows)), in_rows)],
        out_specs=pl.BlockSpec((rows, n), lambda j: (0, j)),
        out_shape=jax.ShapeDtypeStruct((rows, out_cols), BF16),
        compiler_params=pltpu.CompilerParams(
            dimension_semantics=("arbitrary",), vmem_limit_bytes=VMEM_LIMIT),
        name="input_weight_layout",
    )(wt)


def _side_cast_spec(rows, cols, steps):
    for share in (1, 2):
        blk, rem = divmod(rows * share, steps)
        if rem == 0 and blk % BF16_ROWS == 0:
            return pl.BlockSpec((blk, cols), lambda i: (i // share, 0))
    raise ValueError(f"no row block for a ({rows},{cols}) weight over {steps} steps")


def _input_stage(x2d, pos3d, b, s, gmix, wall, gqa, wuq, gkva, wkv, gq, gk, side_weights):
    t = x2d.shape[0]
    tm = TOK_TILE
    nt = s // tm
    steps = t // tm

    def tok(width):
        return pl.BlockSpec((tm, width), lambda i: (i, 0))

    def head_spec():
        return pl.BlockSpec((1, MLA_HEADS, tm, HEAD_PAD), lambda i: (i // nt, 0, i % nt, 0))

    head_shape = jax.ShapeDtypeStruct((b, MLA_HEADS, s, HEAD_PAD), BF16)
    consts = (gmix, wall, gqa, wuq, gkva, wkv, gq, gk)
    side_specs = [_side_cast_spec(w.shape[0], w.shape[1], steps) for w in side_weights]
    outs = pl.pallas_call(
        _input_kernel,
        grid=(steps,),
        in_specs=[tok(D_MODEL), pl.BlockSpec((1, 1, tm), lambda i: (i, 0, 0))]
        + [_const_spec(c.shape) for c in consts] + side_specs,
        out_specs=[head_spec(), head_spec(), head_spec(),
                   tok(HG_WIDTH), tok(HG_WIDTH), tok(HG_WIDTH), tok(HG_WIDTH), tok(2 * D_MODEL)]
        + side_specs,
        out_shape=[head_shape, head_shape, head_shape,
                   jax.ShapeDtypeStruct((t, HG_WIDTH), BF16),
                   jax.ShapeDtypeStruct((t, HG_WIDTH), F32),
                   jax.ShapeDtypeStruct((t, HG_WIDTH), BF16),
                   jax.ShapeDtypeStruct((t, HG_WIDTH), BF16),
                   jax.ShapeDtypeStruct((t, 2 * D_MODEL), BF16)]
        + [jax.ShapeDtypeStruct(w.shape, BF16) for w in side_weights],
        compiler_params=pltpu.CompilerParams(
            dimension_semantics=("arbitrary",), vmem_limit_bytes=VMEM_LIMIT),
        name="input_stage",
    )(x2d, pos3d, *consts, *side_weights)
    return outs[:8], outs[8:]


def _attention_kernel(q_ref, k_ref, v_ref, o_ref, m_ref, acc_ref, s_ref):
    qi = pl.program_id(2)
    tq = ATT_TILE
    half = tq // 2
    m_ref[...] = jnp.full(m_ref.shape, -jnp.inf, F32)
    acc_ref[...] = jnp.zeros(acc_ref.shape, F32)

    def block(j, rows, n_keys, visible):
        start = pl.multiple_of(j * tq, tq)
        for hd in range(ATT_HEADS):
            s_ref[hd, rows, :n_keys] = _dot_nt(q_ref[0, hd, rows, :],
                                               k_ref[0, hd, pl.ds(start, n_keys), :])
        for hd in range(ATT_HEADS):
            sc = s_ref[hd, rows, :n_keys]
            if visible is not None:
                sc = jnp.where(visible, sc, -jnp.inf)
            m_old = m_ref[hd, rows, :]
            m_new = jnp.maximum(m_old, jnp.max(sc, axis=-1, keepdims=True))
            p = jnp.exp2(sc - jnp.concatenate([m_new] * (n_keys // LANES), axis=1)).astype(BF16)
            acc_ref[hd, rows, :] = (jnp.exp2(m_old - m_new) * acc_ref[hd, rows, :]
                                    + _dot(p, v_ref[0, hd, pl.ds(start, n_keys), :]))
            m_ref[hd, rows, :] = m_new

    def body(j, carry):
        block(j, slice(0, tq), tq, None)
        return carry

    lax.fori_loop(0, qi, body, 0)
    def lower(n_keys, shift):
        r_i = lax.broadcasted_iota(jnp.int32, (half, n_keys), 0)
        c_i = lax.broadcasted_iota(jnp.int32, (half, n_keys), 1)
        return c_i <= r_i + shift

    block(qi, slice(0, half), half, lower(half, 0))
    block(qi, slice(half, tq), tq, lower(tq, half))

    outs = []
    for hd in range(ATT_HEADS):
        acc = acc_ref[hd]
        outs.append(acc[:, :V_HEAD_DIM] / acc[:, DENOM_LANE:DENOM_LANE + 1])
    o_ref[0] = jnp.concatenate(outs, axis=-1).astype(BF16)


def _attention_stage(q, k, v):
    b, nh, s, _ = q.shape
    tq = ATT_TILE
    g = ATT_HEADS
    return pl.pallas_call(
        _attention_kernel,
        grid=(b, nh // g, s // tq),
        in_specs=[pl.BlockSpec((1, g, tq, HEAD_PAD), lambda bi, hp, i: (bi, hp, i, 0)),
                  pl.BlockSpec((1, g, s, HEAD_PAD), lambda bi, hp, i: (bi, hp, 0, 0)),
                  pl.BlockSpec((1, g, s, HEAD_PAD), lambda bi, hp, i: (bi, hp, 0, 0))],
        out_specs=pl.BlockSpec((1, tq, g * V_HEAD_DIM), lambda bi, hp, i: (bi, i, hp)),
        out_shape=jax.ShapeDtypeStruct((b, s, BRANCH_WIDTH), BF16),
        scratch_shapes=[pltpu.VMEM((g, tq, LANES), F32), pltpu.VMEM((g, tq, HEAD_PAD), F32),
                        pltpu.VMEM((g, tq, tq), F32)],
        compiler_params=pltpu.CompilerParams(
            dimension_semantics=("arbitrary", "arbitrary", "arbitrary"),
            vmem_limit_bytes=VMEM_LIMIT),
        name="causal_attention",
    )(q, k, v)


def _hgrn_span_matrix():
    c = HG_CHUNK
    x = np.arange(c)[:, None]
    y = np.arange(c)[None, :]
    mats = [(y <= x)]
    for b in HG_LEVELS:
        r = (x // (2 * b)) * (2 * b) + b
        mats.append((y > np.minimum(x, r)) & (y <= np.maximum(x, r)))
    m = np.concatenate(mats, axis=0).astype(np.float32)
    return np.concatenate([m, m], axis=1)


def _block_diag_rows(x):
    a, b = x[:, :HG_DIM], x[:, HG_DIM:]
    zero = jnp.zeros_like(a)
    return jnp.concatenate([jnp.concatenate([a, zero], axis=1),
                            jnp.concatenate([zero, b], axis=1)], axis=0)


def _hgrn_chunk_local(q, hf, v_bf, lb, span, right_rows, pair_masks):
    c = HG_CHUNK
    f = lb + (1.0 - lb) * jax.nn.sigmoid(hf)
    lf = jnp.log2(f)
    kk = 1.0 - f
    lf_hi = lf.astype(BF16)
    lf_lo = (lf - lf_hi.astype(F32)).astype(BF16)
    spans = _dot(span, jnp.concatenate([lf_hi, lf_lo], axis=0))
    cum = spans[:c]
    last = cum[c - 1:c, :]
    q_dec = (q * jnp.exp2(cum)).astype(BF16)
    k_dec = (kk * jnp.exp2(last - cum)).astype(BF16)
    q_bf = q.astype(BF16)
    k_bf = kk.astype(BF16)
    zs = [(jnp.where(right_rows[lvl], q, kk)
           * jnp.exp2(spans[(lvl + 1) * c:(lvl + 2) * c])).astype(BF16)
          for lvl in range(len(HG_LEVELS))]

    o_intra, incr = [], []
    for pair in range(HG_HEADS // 2):
        psl = slice(2 * pair * HG_DIM, (2 * pair + 2) * HG_DIM)
        att = jnp.where(pair_masks[0], _dot_nt(q_bf[:, psl], _block_diag_rows(k_bf[:, psl])), 0.0)
        for lvl in range(len(HG_LEVELS)):
            z = zs[lvl][:, psl]
            att = jnp.where(pair_masks[lvl + 1], _dot_nt(z, _block_diag_rows(z)), att)
        v_pair = v_bf[:, psl]
        o_intra.append(_dot(att.astype(BF16), _block_diag_rows(v_pair)))
        for hd in (2 * pair, 2 * pair + 1):
            sl = slice(hd * HG_DIM, (hd + 1) * HG_DIM)
            incr.append(_dot(v_bf[:, sl].astype(F32).T.astype(BF16), k_dec[:, sl]))
    return o_intra, q_dec, incr, jnp.exp2(last)


def _hgrn_kernel(hq_ref, hf_ref, hi_ref, hg_ref, lb_ref, gout_ref, span_ref, o_ref, state_ref):
    @pl.when(pl.program_id(1) == 0)
    def _():
        state_ref[...] = jnp.zeros(state_ref.shape, F32)

    c = HG_CHUNK
    t_i = lax.broadcasted_iota(jnp.int32, (c, 2 * c), 0)
    s_i = lax.broadcasted_iota(jnp.int32, (c, 2 * c), 1) % c
    row = lax.broadcasted_iota(jnp.int32, (c, HG_WIDTH), 0)
    pair_masks = [t_i == s_i] + [((t_i // b) ^ (s_i // b) == 1) & (s_i < t_i) for b in HG_LEVELS]
    right_rows = [(row // b) % 2 == 1 for b in HG_LEVELS]
    span = span_ref[...]
    lb = lb_ref[...]
    n_chunks = HG_BLOCK // c

    local = []
    for ci in range(n_chunks):
        rows = slice(ci * c, (ci + 1) * c)
        local.append(_hgrn_chunk_local(hq_ref[0, rows, :].astype(F32), hf_ref[0, rows, :],
                                       hi_ref[0, rows, :], lb, span, right_rows, pair_masks))

    gout = gout_ref[...]
    for pair in range(HG_HEADS // 2):
        hds = (2 * pair, 2 * pair + 1)
        psl = slice(2 * pair * HG_DIM, (2 * pair + 2) * HG_DIM)
        states = [state_ref[hd] for hd in hds]
        for ci in range(n_chunks):
            rows = slice(ci * c, (ci + 1) * c)
            o_intra, q_dec, incr, decay = local[ci]
            both = jnp.concatenate([st.astype(BF16) for st in states], axis=1)
            o_pair = o_intra[pair] + _dot_nt(q_dec[:, psl], _block_diag_rows(both))
            for n, hd in enumerate(hds):
                sl = slice(hd * HG_DIM, (hd + 1) * HG_DIM)
                states[n] = states[n] * decay[:, sl] + incr[hd]
                o = o_pair[:, n * HG_DIM:(n + 1) * HG_DIM]
                gate = hg_ref[0, rows, sl].astype(F32)
                o_ref[0, rows, sl] = (_rms(o, gout) * (gate * jax.nn.sigmoid(gate))).astype(BF16)
        for n, hd in enumerate(hds):
            state_ref[hd] = states[n]


def _hgrn_stage(hq, hf, hi, hg, lb, gout):
    span = jnp.asarray(_hgrn_span_matrix(), BF16)
    b, s, _ = hq.shape
    tb = HG_BLOCK
    tok = pl.BlockSpec((1, tb, HG_WIDTH), lambda bi, i: (bi, i, 0))
    return pl.pallas_call(
        _hgrn_kernel,
        grid=(b, s // tb),
        in_specs=[tok, tok, tok, tok, _const_spec(lb.shape), _const_spec(gout.shape),
                  _const_spec(span.shape)],
        out_specs=tok,
        out_shape=jax.ShapeDtypeStruct((b, s, HG_WIDTH), BF16),
        scratch_shapes=[pltpu.VMEM((HG_HEADS, HG_DIM, HG_DIM), F32)],
        compiler_params=pltpu.CompilerParams(
            dimension_semantics=("arbitrary", "arbitrary"), vmem_limit_bytes=VMEM_LIMIT),
        name="hgrn2_recurrence",
    )(hq, hf, hi, hg, lb, gout, span)


def _output_kernel(x_ref, attn_ref, rec_ref, gates_ref, p_ref,
                   wb_ref, wout_ref, gffn_ref, wgate_ref, wup_ref, wdown_ref,
                   gpg_ref, wpg_ref, wpp_ref, gpost_ref, o_ref):
    y0 = _dot(attn_ref[...], wb_ref[:BRANCH_WIDTH, :])
    y1 = _dot(rec_ref[...], wb_ref[BRANCH_WIDTH:, :])
    g0 = gates_ref[:, :D_MODEL].astype(F32)
    g1 = gates_ref[:, D_MODEL:].astype(F32)
    merged = (g0 * y0 + g1 * y1).astype(BF16)
    x1 = x_ref[...] + _dot(merged, wout_ref[...])

    e = _rms(_dot(p_ref[...].astype(BF16), wpp_ref[...]), gpost_ref[...])
    h2 = _rms(x1, gffn_ref[...]).astype(BF16)
    x2 = x1
    lo = 0
    for width in FFN_SPLITS:
        a = _dot(h2, wgate_ref[:, lo:lo + width])
        u = _dot(h2, wup_ref[:, lo:lo + width])
        z = (a * jax.nn.sigmoid(a) * u).astype(BF16)
        x2 = x2 + _dot(z, wdown_ref[lo:lo + width, :])
        lo += width

    g = jax.nn.sigmoid(_dot(_rms(x2, gpg_ref[...]).astype(BF16), wpg_ref[...]))
    o_ref[...] = x2 + g * e


def _output_stage(x2d, attn, rec, gates, p2d, wb, wout, gffn, wgate, wup, wdown,
                  gpg, wpg, wpp, gpost):
    t = x2d.shape[0]
    tm = TOK_TILE

    def tok(width):
        return pl.BlockSpec((tm, width), lambda i: (i, 0))

    consts = (wb, wout, gffn, wgate, wup, wdown, gpg, wpg, wpp, gpost)
    return pl.pallas_call(
        _output_kernel,
        grid=(t // tm,),
        in_specs=[tok(D_MODEL), tok(BRANCH_WIDTH), tok(HG_WIDTH), tok(2 * D_MODEL), tok(PLE_DIM)]
        + [_const_spec(c.shape) for c in consts],
        out_specs=tok(D_MODEL),
        out_shape=jax.ShapeDtypeStruct((t, D_MODEL), F32),
        compiler_params=pltpu.CompilerParams(
            dimension_semantics=("arbitrary",), vmem_limit_bytes=VMEM_LIMIT),
        name="output_stage",
    )(x2d, attn, rec, gates, p2d, *consts)


def _pad_heads(w, heads, width):
    rows = w.shape[0]
    w = w.reshape(rows, heads, width)
    w = jnp.pad(w, ((0, 0), (0, 0), (0, HEAD_PAD - width)))
    return w.reshape(rows, heads * HEAD_PAD)


def _qk_head_layout(nope, rope):
    like = nope if nope is not None else rope

    def z(width):
        return jnp.zeros(like.shape[:-1] + (width,), like.dtype)

    n0 = nope[..., :HEAD_HALF_NOPE] if nope is not None else z(HEAD_HALF_NOPE)
    n1 = nope[..., HEAD_HALF_NOPE:] if nope is not None else z(HEAD_HALF_NOPE)
    x1 = rope[..., :ROPE_HALF] if rope is not None else z(ROPE_HALF)
    x2 = rope[..., ROPE_HALF:] if rope is not None else z(ROPE_HALF)
    return jnp.concatenate([n0, x1, z(ROPE_HALF), n1, x2, z(ROPE_HALF)], axis=-1)


def _row(v):
    return v.reshape(1, -1).astype(F32)


def kernel(x, p, positions, mix_norm_g, w_in, q_a_norm_g, w_uq, kv_a_norm_g, w_ukv, q_norm_g, k_norm_g, hg_lb_logits, hg_out_norm_g, w_branch, w_out, ffn_norm_g, w_ffn_gate, w_ffn_up, w_ffn_down, ple_gate_norm_g, w_ple_gate, w_ple_proj, ple_post_norm_g):
    b, s, d = x.shape
    depth = w_in.shape[0]
    lower_bounds = jnp.cumsum(jax.nn.softmax(hg_lb_logits.astype(F32), axis=0), axis=0)
    pos3d = positions.reshape(b * s // TOK_TILE, 1, TOK_TILE)
    x2d = x.reshape(b * s, d)
    q_scale = QK_HEAD_DIM ** -0.5 * math.log2(math.e)

    for layer in range(depth):
        wi = w_in[layer]
        wall = _input_weight_layout(wi.T)
        wq3 = w_uq[layer].reshape(Q_LORA_RANK, MLA_HEADS, QK_HEAD_DIM)
        wuq = _qk_head_layout(wq3[..., :QK_NOPE_DIM], wq3[..., QK_NOPE_DIM:])
        wuq = wuq.reshape(Q_LORA_RANK, MLA_HEADS * HEAD_PAD).astype(BF16)
        wukv = w_ukv[layer].reshape(KV_LORA_RANK, MLA_HEADS, QK_NOPE_DIM + V_HEAD_DIM)
        wk = _qk_head_layout(wukv[..., :QK_NOPE_DIM], None).reshape(KV_LORA_RANK, -1)
        wv = _pad_heads(wukv[:, :, QK_NOPE_DIM:].reshape(KV_LORA_RANK, -1), MLA_HEADS, V_HEAD_DIM)
        wkv = jnp.concatenate([wk, wv], axis=1).astype(BF16)
        gq = q_norm_g[layer].astype(F32) * q_scale
        gk = k_norm_g[layer].astype(F32)

        side = (w_branch[layer].reshape(2 * BRANCH_WIDTH, d), w_out[layer], w_ffn_gate[layer],
                w_ffn_up[layer], w_ffn_down[layer], w_ple_gate[layer], w_ple_proj[layer])
        (q, k, v, hq, hf, hi, hg, gates), side_bf = _input_stage(
            x2d, pos3d, b, s, _row(mix_norm_g[layer]), wall, _row(q_a_norm_g[layer]), wuq,
            _row(kv_a_norm_g[layer]), wkv,
            _row(_qk_head_layout(gq[:QK_NOPE_DIM], gq[QK_NOPE_DIM:])),
            _row(_qk_head_layout(gk[:QK_NOPE_DIM], gk[QK_NOPE_DIM:])), side)
        wb, wout, wgate, wup, wdown, wpg, wpp = side_bf

        attn = _attention_stage(q, k, v)
        rec = _hgrn_stage(hq.reshape(b, s, HG_WIDTH), hf.reshape(b, s, HG_WIDTH),
                          hi.reshape(b, s, HG_WIDTH), hg.reshape(b, s, HG_WIDTH),
                          _row(lower_bounds[layer]), _row(hg_out_norm_g[layer]))

        x2d = _output_stage(
            x2d, attn.reshape(b * s, BRANCH_WIDTH), rec.reshape(b * s, HG_WIDTH), gates,
            p[layer].reshape(b * s, PLE_DIM),
            wb, wout, _row(ffn_norm_g[layer]),
            wgate, wup, wdown, _row(ple_gate_norm_g[layer]), wpg, wpp,
            _row(ple_post_norm_g[layer]))
    return x2d.reshape(b, s, d)
```

```python
import functools
import math

import numpy as np

import jax
import jax.numpy as jnp
from jax import lax
from jax.experimental import pallas as pl
from jax.experimental.pallas import tpu as pltpu

D_MODEL = 1024
MLA_HEADS = 8
QK_NOPE_DIM = 64
QK_ROPE_DIM = 32
ROPE_HALF = QK_ROPE_DIM // 2
QK_HEAD_DIM = QK_NOPE_DIM + QK_ROPE_DIM
V_HEAD_DIM = 64
Q_LORA_RANK = 384
KV_LORA_RANK = 256
ROPE_BASE = 10000.0
HG_HEADS = 4
HG_DIM = 128
HG_WIDTH = HG_HEADS * HG_DIM
BRANCH_WIDTH = MLA_HEADS * V_HEAD_DIM
FFN_HIDDEN = 2816
PLE_DIM = 256
EPS = 1e-6

LANES = 128
BF16_ROWS = 16
HEAD_PAD = LANES
HEAD_HALF_NOPE = QK_NOPE_DIM // 2
W_LAT = Q_LORA_RANK + KV_LORA_RANK + HEAD_PAD
DENOM_LANE = V_HEAD_DIM

VMEM_LIMIT = 60 * 1024 * 1024

TOK_TILE = 512
ATT_TILE = 512
ATT_HEADS = 8
HG_BLOCK = 1024
HG_CHUNK = 64
HG_LEVELS = tuple(HG_CHUNK >> (i + 1) for i in range(HG_CHUNK.bit_length() - 1))
FFN_SPLITS = (1024, 1024, 768)

BF16 = jnp.bfloat16
F32 = jnp.float32


def _const_spec(shape):
    nd = len(shape)
    return pl.BlockSpec(shape, lambda *_: (0,) * nd, pipeline_mode=pl.Buffered(1))


def _rms(v, gain, width=None):
    n = v.shape[-1] if width is None else width
    ms = jnp.sum(v * v, axis=-1, keepdims=True) * (1.0 / n)
    return v * lax.rsqrt(ms + EPS) * gain


def _dot(a, b):
    return jnp.dot(a, b, preferred_element_type=F32)


def _dot_nt(a, b):
    return lax.dot_general(a, b, (((1,), (1,)), ((), ())), preferred_element_type=F32)


def _rope_tables(pos_row):
    tm = pos_row.shape[1]
    fidx = lax.broadcasted_iota(jnp.int32, (ROPE_HALF, 1), 0).astype(F32)
    inv_freq = jnp.exp(fidx * (-math.log(ROPE_BASE) * 2.0 / QK_ROPE_DIM))
    ang = pos_row.astype(F32) * inv_freq
    cos = jnp.cos(ang)
    sin = jnp.sin(ang)
    ones = jnp.ones((HEAD_HALF_NOPE, tm), F32)
    zeros_n = jnp.zeros((HEAD_HALF_NOPE, tm), F32)
    zeros_p = jnp.zeros((ROPE_HALF, tm), F32)
    cos_t = jnp.concatenate([ones, cos, zeros_p, ones, cos, zeros_p], axis=0)
    sin_t = jnp.concatenate([zeros_n, -sin, zeros_p, zeros_n, sin, zeros_p], axis=0)
    return cos_t.T, sin_t.T


def _input_kernel(x_ref, pos_ref, gmix_ref, wall_ref, gqa_ref, wuq_ref, gkva_ref, wkv_ref,
                  gq_ref, gk_ref, *rest):
    n_side = (len(rest) - 8) // 2
    side_in = rest[:n_side]
    q_ref, k_ref, v_ref, hq_ref, hf_ref, hi_ref, hg_ref, gates_ref = rest[n_side:n_side + 8]
    side_out = rest[n_side + 8:]
    for w_ref, wb_ref in zip(side_in, side_out):
        wb_ref[...] = w_ref[...].astype(BF16)

    x = x_ref[...]
    h = _rms(x, gmix_ref[...]).astype(BF16)

    lat = _dot(h, wall_ref[:, :W_LAT])
    cq = _rms(lat[:, :Q_LORA_RANK], gqa_ref[...]).astype(BF16)
    ckv = _rms(lat[:, Q_LORA_RANK:Q_LORA_RANK + KV_LORA_RANK], gkva_ref[...]).astype(BF16)
    k_rope = lat[:, Q_LORA_RANK + KV_LORA_RANK:]

    cos_tab, sin_tab = _rope_tables(pos_ref[0])
    gq = gq_ref[...]
    gk = gk_ref[...]
    q_cos = cos_tab * gq
    q_sin = sin_tab * pltpu.roll(gq, LANES // 2, 1)
    kr = k_rope * gk
    kr = kr * cos_tab + pltpu.roll(kr, LANES // 2, 1) * sin_tab
    kr_ss = jnp.sum(k_rope * k_rope, axis=-1, keepdims=True)

    q_all = _dot(cq, wuq_ref[...])
    kv_all = _dot(ckv, wkv_ref[...])
    lane = lax.broadcasted_iota(jnp.int32, (1, LANES), 1)
    one_lane = (lane == DENOM_LANE).astype(F32)
    inv_width = 1.0 / QK_HEAD_DIM
    for hd in range(MLA_HEADS):
        sl = slice(hd * HEAD_PAD, (hd + 1) * HEAD_PAD)
        t = q_all[:, sl]
        r = lax.rsqrt(jnp.sum(t * t, axis=-1, keepdims=True) * inv_width + EPS)
        q_ref[0, hd] = ((t * q_cos + pltpu.roll(t, LANES // 2, 1) * q_sin) * r).astype(BF16)
        t = kv_all[:, sl]
        r = lax.rsqrt((jnp.sum(t * t, axis=-1, keepdims=True) + kr_ss) * inv_width + EPS)
        k_ref[0, hd] = ((t * gk + kr) * r).astype(BF16)
        vsl = slice(MLA_HEADS * HEAD_PAD + hd * HEAD_PAD, MLA_HEADS * HEAD_PAD + (hd + 1) * HEAD_PAD)
        v_ref[0, hd] = (kv_all[:, vsl] + one_lane).astype(BF16)

    hh = _dot(h, wall_ref[:, W_LAT:W_LAT + 4 * HG_WIDTH])
    hq_ref[...] = hh[:, :HG_WIDTH].astype(BF16)
    hf_ref[...] = hh[:, HG_WIDTH:2 * HG_WIDTH]
    hi_ref[...] = hh[:, 2 * HG_WIDTH:3 * HG_WIDTH].astype(BF16)
    hg_ref[...] = hh[:, 3 * HG_WIDTH:].astype(BF16)

    gates_ref[...] = jax.nn.sigmoid(_dot(h, wall_ref[:, W_LAT + 4 * HG_WIDTH:])).astype(BF16)


W_PREP_COLS = 2432


def _input_weight_kernel(wt_ref, o_ref):
    c0 = Q_LORA_RANK + KV_LORA_RANK
    j = pl.program_id(0)
    n = W_PREP_COLS
    lat_full = c0 // n
    lat_rest = c0 - lat_full * n

    @pl.when(j != lat_full)
    def _():
        o_ref[...] = wt_ref[...].T.astype(BF16)

    @pl.when(j == lat_full)
    def _():
        d = wt_ref.shape[1]
        z_n = jnp.zeros((HEAD_HALF_NOPE, d), F32)
        z_p = jnp.zeros((ROPE_HALF, d), F32)
        parts = [wt_ref[:lat_rest, :], z_n, wt_ref[lat_rest:lat_rest + ROPE_HALF, :], z_p,
                 z_n, wt_ref[lat_rest + ROPE_HALF:lat_rest + QK_ROPE_DIM, :], z_p]
        tail = n - lat_rest - HEAD_PAD
        if tail:
            parts.append(wt_ref[lat_rest + QK_ROPE_DIM:lat_rest + QK_ROPE_DIM + tail, :])
        o_ref[...] = jnp.concatenate(parts, axis=0).T.astype(BF16)


def _input_weight_layout(wt):
    cols, rows = wt.shape
    out_cols = cols - QK_ROPE_DIM + HEAD_PAD
    n = W_PREP_COLS
    lat_full = (Q_LORA_RANK + KV_LORA_RANK) // n
    pad = HEAD_PAD - QK_ROPE_DIM

    def in_rows(j):
        return (pl.multiple_of(jnp.where(j <= lat_full, j * n, j * n - pad), QK_ROPE_DIM), 0)

    return pl.pallas_call(
        _input_weight_kernel,
        grid=(out_cols // n,),
        in_specs=[pl.BlockSpec((pl.Element(n), pl.Element(rows)), in_rows)],
        out_specs=pl.BlockSpec((rows, n), lambda j: (0, j)),
        out_shape=jax.ShapeDtypeStruct((rows, out_cols), BF16),
        compiler_params=pltpu.CompilerParams(
            dimension_semantics=("arbitrary",), vmem_limit_bytes=VMEM_LIMIT),
        name="input_weight_layout",
    )(wt)


def _side_cast_spec(rows, cols, steps):
    for share in (1, 2):
        blk, rem = divmod(rows * share, steps)
        if rem == 0 and blk % BF16_ROWS == 0:
            return pl.BlockSpec((blk, cols), lambda i: (i // share, 0))
    raise ValueError(f"no row block for a ({rows},{cols}) weight over {steps} steps")


def _input_stage(x2d, pos3d, b, s, gmix, wall, gqa, wuq, gkva, wkv, gq, gk, side_weights):
    t = x2d.shape[0]
    tm = TOK_TILE
    nt = s // tm
    steps = t // tm

    def tok(width):
        return pl.BlockSpec((tm, width), lambda i: (i, 0))

    def head_spec():
        return pl.BlockSpec((1, MLA_HEADS, tm, HEAD_PAD), lambda i: (i // nt, 0, i % nt, 0))

    head_shape = jax.ShapeDtypeStruct((b, MLA_HEADS, s, HEAD_PAD), BF16)
    consts = (gmix, wall, gqa, wuq, gkva, wkv, gq, gk)
    side_specs = [_side_cast_spec(w.shape[0], w.shape[1], steps) for w in side_weights]
    outs = pl.pallas_call(
        _input_kernel,
        grid=(steps,),
        in_specs=[tok(D_MODEL), pl.BlockSpec((1, 1, tm), lambda i: (i, 0, 0))]
        + [_const_spec(c.shape) for c in consts] + side_specs,
        out_specs=[head_spec(), head_spec(), head_spec(),
                   tok(HG_WIDTH), tok(HG_WIDTH), tok(HG_WIDTH), tok(HG_WIDTH), tok(2 * D_MODEL)]
        + side_specs,
        out_shape=[head_shape, head_shape, head_shape,
                   jax.ShapeDtypeStruct((t, HG_WIDTH), BF16),
                   jax.ShapeDtypeStruct((t, HG_WIDTH), F32),
                   jax.ShapeDtypeStruct((t, HG_WIDTH), BF16),
                   jax.ShapeDtypeStruct((t, HG_WIDTH), BF16),
                   jax.ShapeDtypeStruct((t, 2 * D_MODEL), BF16)]
        + [jax.ShapeDtypeStruct(w.shape, BF16) for w in side_weights],
        compiler_params=pltpu.CompilerParams(
            dimension_semantics=("arbitrary",), vmem_limit_bytes=VMEM_LIMIT),
        name="input_stage",
    )(x2d, pos3d, *consts, *side_weights)
    return outs[:8], outs[8:]


def _attention_kernel(q_ref, k_ref, v_ref, o_ref, m_ref, acc_ref, s_ref):
    qi = pl.program_id(2)
    tq = ATT_TILE
    half = tq // 2
    m_ref[...] = jnp.full(m_ref.shape, -jnp.inf, F32)
    acc_ref[...] = jnp.zeros(acc_ref.shape, F32)

    def block(j, rows, n_keys, visible):
        start = pl.multiple_of(j * tq, tq)
        for hd in range(ATT_HEADS):
            s_ref[hd, rows, :n_keys] = _dot_nt(q_ref[0, hd, rows, :],
                                               k_ref[0, hd, pl.ds(start, n_keys), :])
        for hd in range(ATT_HEADS):
            sc = s_ref[hd, rows, :n_keys]
            if visible is not None:
                sc = jnp.where(visible, sc, -jnp.inf)
            m_old = m_ref[hd, rows, :]
            m_new = jnp.maximum(m_old, jnp.max(sc, axis=-1, keepdims=True))
            p = jnp.exp2(sc - jnp.concatenate([m_new] * (n_keys // LANES), axis=1)).astype(BF16)
            acc_ref[hd, rows, :] = (jnp.exp2(m_old - m_new) * acc_ref[hd, rows, :]
                                    + _dot(p, v_ref[0, hd, pl.ds(start, n_keys), :]))
            m_ref[hd, rows, :] = m_new

    def body(t, carry):
        block(2 * t, slice(0, tq), tq, None)
        block(2 * t + 1, slice(0, tq), tq, None)
        return carry

    lax.fori_loop(0, qi // 2, body, 0)

    @pl.when(qi % 2 == 1)
    def _():
        block(qi - 1, slice(0, tq), tq, None)

    def lower(n_keys, shift):
        r_i = lax.broadcasted_iota(jnp.int32, (half, n_keys), 0)
        c_i = lax.broadcasted_iota(jnp.int32, (half, n_keys), 1)
        return c_i <= r_i + shift

    block(qi, slice(0, half), half, lower(half, 0))
    block(qi, slice(half, tq), tq, lower(tq, half))

    outs = []
    for hd in range(ATT_HEADS):
        acc = acc_ref[hd]
        outs.append(acc[:, :V_HEAD_DIM] / acc[:, DENOM_LANE:DENOM_LANE + 1])
    o_ref[0] = jnp.concatenate(outs, axis=-1).astype(BF16)


def _attention_stage(q, k, v):
    b, nh, s, _ = q.shape
    tq = ATT_TILE
    g = ATT_HEADS
    return pl.pallas_call(
        _attention_kernel,
        grid=(b, nh // g, s // tq),
        in_specs=[pl.BlockSpec((1, g, tq, HEAD_PAD), lambda bi, hp, i: (bi, hp, i, 0)),
                  pl.BlockSpec((1, g, s, HEAD_PAD), lambda bi, hp, i: (bi, hp, 0, 0)),
                  pl.BlockSpec((1, g, s, HEAD_PAD), lambda bi, hp, i: (bi, hp, 0, 0))],
        out_specs=pl.BlockSpec((1, tq, g * V_HEAD_DIM), lambda bi, hp, i: (bi, i, hp)),
        out_shape=jax.ShapeDtypeStruct((b, s, BRANCH_WIDTH), BF16),
        scratch_shapes=[pltpu.VMEM((g, tq, LANES), F32), pltpu.VMEM((g, tq, HEAD_PAD), F32),
                        pltpu.VMEM((g, tq, tq), F32)],
        compiler_params=pltpu.CompilerParams(
            dimension_semantics=("arbitrary", "arbitrary", "arbitrary"),
            vmem_limit_bytes=VMEM_LIMIT),
        name="causal_attention",
    )(q, k, v)


def _hgrn_span_matrix():
    c = HG_CHUNK
    x = np.arange(c)[:, None]
    y = np.arange(c)[None, :]
    mats = [(y <= x)]
    for b in HG_LEVELS:
        r = (x // (2 * b)) * (2 * b) + b
        mats.append((y > np.minimum(x, r)) & (y <= np.maximum(x, r)))
    m = np.concatenate(mats, axis=0).astype(np.float32)
    return np.concatenate([m, m], axis=1)


def _block_diag_rows(x):
    a, b = x[:, :HG_DIM], x[:, HG_DIM:]
    zero = jnp.zeros_like(a)
    return jnp.concatenate([jnp.concatenate([a, zero], axis=1),
                            jnp.concatenate([zero, b], axis=1)], axis=0)


def _hgrn_chunk_local(q, hf, v_bf, lb, span, right_rows, pair_masks):
    c = HG_CHUNK
    f = lb + (1.0 - lb) * jax.nn.sigmoid(hf)
    lf = jnp.log2(f)
    kk = 1.0 - f
    lf_hi = lf.astype(BF16)
    lf_lo = (lf - lf_hi.astype(F32)).astype(BF16)
    spans = _dot(span, jnp.concatenate([lf_hi, lf_lo], axis=0))
    cum = spans[:c]
    last = cum[c - 1:c, :]
    q_dec = (q * jnp.exp2(cum)).astype(BF16)
    k_dec = (kk * jnp.exp2(last - cum)).astype(BF16)
    q_bf = q.astype(BF16)
    k_bf = kk.astype(BF16)
    zs = [(jnp.where(right_rows[lvl], q, kk)
           * jnp.exp2(spans[(lvl + 1) * c:(lvl + 2) * c])).astype(BF16)
          for lvl in range(len(HG_LEVELS))]

    o_intra, incr = [], []
    for pair in range(HG_HEADS // 2):
        psl = slice(2 * pair * HG_DIM, (2 * pair + 2) * HG_DIM)
        att = jnp.where(pair_masks[0], _dot_nt(q_bf[:, psl], _block_diag_rows(k_bf[:, psl])), 0.0)
        for lvl in range(len(HG_LEVELS)):
            z = zs[lvl][:, psl]
            att = jnp.where(pair_masks[lvl + 1], _dot_nt(z, _block_diag_rows(z)), att)
        v_pair = v_bf[:, psl]
        o_intra.append(_dot(att.astype(BF16), _block_diag_rows(v_pair)))
        for hd in (2 * pair, 2 * pair + 1):
            sl = slice(hd * HG_DIM, (hd + 1) * HG_DIM)
            incr.append(_dot(v_bf[:, sl].astype(F32).T.astype(BF16), k_dec[:, sl]))
    return o_intra, q_dec, incr, jnp.exp2(last)


def _hgrn_kernel(hq_ref, hf_ref, hi_ref, hg_ref, lb_ref, gout_ref, span_ref, o_ref, state_ref):
    @pl.when(pl.program_id(1) == 0)
    def _():
        state_ref[...] = jnp.zeros(state_ref.shape, F32)

    c = HG_CHUNK
    t_i = lax.broadcasted_iota(jnp.int32, (c, 2 * c), 0)
    s_i = lax.broadcasted_iota(jnp.int32, (c, 2 * c), 1) % c
    row = lax.broadcasted_iota(jnp.int32, (c, HG_WIDTH), 0)
    pair_masks = [t_i == s_i] + [((t_i // b) ^ (s_i // b) == 1) & (s_i < t_i) for b in HG_LEVELS]
    right_rows = [(row // b) % 2 == 1 for b in HG_LEVELS]
    span = span_ref[...]
    lb = lb_ref[...]
    n_chunks = HG_BLOCK // c

    local = []
    for ci in range(n_chunks):
        rows = slice(ci * c, (ci + 1) * c)
        local.append(_hgrn_chunk_local(hq_ref[0, rows, :].astype(F32), hf_ref[0, rows, :],
                                       hi_ref[0, rows, :], lb, span, right_rows, pair_masks))

    gout = gout_ref[...]
    for pair in range(HG_HEADS // 2):
        hds = (2 * pair, 2 * pair + 1)
        psl = slice(2 * pair * HG_DIM, (2 * pair + 2) * HG_DIM)
        states = [state_ref[hd] for hd in hds]
        for ci in range(n_chunks):
            rows = slice(ci * c, (ci + 1) * c)
            o_intra, q_dec, incr, decay = local[ci]
            both = jnp.concatenate([st.astype(BF16) for st in states], axis=1)
            o_pair = o_intra[pair] + _dot_nt(q_dec[:, psl], _block_diag_rows(both))
            for n, hd in enumerate(hds):
                sl = slice(hd * HG_DIM, (hd + 1) * HG_DIM)
                states[n] = states[n] * decay[:, sl] + incr[hd]
                o = o_pair[:, n * HG_DIM:(n + 1) * HG_DIM]
                gate = hg_ref[0, rows, sl].astype(F32)
                o_ref[0, rows, sl] = (_rms(o, gout) * (gate * jax.nn.sigmoid(gate))).astype(BF16)
        for n, hd in enumerate(hds):
            state_ref[hd] = states[n]


def _hgrn_stage(hq, hf, hi, hg, lb, gout):
    span = jnp.asarray(_hgrn_span_matrix(), BF16)
    b, s, _ = hq.shape
    tb = HG_BLOCK
    tok = pl.BlockSpec((1, tb, HG_WIDTH), lambda bi, i: (bi, i, 0))
    return pl.pallas_call(
        _hgrn_kernel,
        grid=(b, s // tb),
        in_specs=[tok, tok, tok, tok, _const_spec(lb.shape), _const_spec(gout.shape),
                  _const_spec(span.shape)],
        out_specs=tok,
        out_shape=jax.ShapeDtypeStruct((b, s, HG_WIDTH), BF16),
        scratch_shapes=[pltpu.VMEM((HG_HEADS, HG_DIM, HG_DIM), F32)],
        compiler_params=pltpu.CompilerParams(
            dimension_semantics=("arbitrary", "arbitrary"), vmem_limit_bytes=VMEM_LIMIT),
        name="hgrn2_recurrence",
    )(hq, hf, hi, hg, lb, gout, span)


def _output_kernel(x_ref, attn_ref, rec_ref, gates_ref, p_ref,
                   wb_ref, wout_ref, gffn_ref, wgate_ref, wup_ref, wdown_ref,
                   gpg_ref, wpg_ref, wpp_ref, gpost_ref, o_ref):
    y0 = _dot(attn_ref[...], wb_ref[:BRANCH_WIDTH, :])
    y1 = _dot(rec_ref[...], wb_ref[BRANCH_WIDTH:, :])
    g0 = gates_ref[:, :D_MODEL].astype(F32)
    g1 = gates_ref[:, D_MODEL:].astype(F32)
    merged = (g0 * y0 + g1 * y1).astype(BF16)
    x1 = x_ref[...] + _dot(merged, wout_ref[...])

    e = _rms(_dot(p_ref[...].astype(BF16), wpp_ref[...]), gpost_ref[...])
    h2 = _rms(x1, gffn_ref[...]).astype(BF16)
    x2 = x1
    lo = 0
    for width in FFN_SPLITS:
        a = _dot(h2, wgate_ref[:, lo:lo + width])
        u = _dot(h2, wup_ref[:, lo:lo + width])
        z = (a * jax.nn.sigmoid(a) * u).astype(BF16)
        x2 = x2 + _dot(z, wdown_ref[lo:lo + width, :])
        lo += width

    g = jax.nn.sigmoid(_dot(_rms(x2, gpg_ref[...]).astype(BF16), wpg_ref[...]))
    o_ref[...] = x2 + g * e


def _output_stage(x2d, attn, rec, gates, p2d, wb, wout, gffn, wgate, wup, wdown,
                  gpg, wpg, wpp, gpost):
    t = x2d.shape[0]
    tm = TOK_TILE

    def tok(width):
        return pl.BlockSpec((tm, width), lambda i: (i, 0))

    consts = (wb, wout, gffn, wgate, wup, wdown, gpg, wpg, wpp, gpost)
    return pl.pallas_call(
        _output_kernel,
        grid=(t // tm,),
        in_specs=[tok(D_MODEL), tok(BRANCH_WIDTH), tok(HG_WIDTH), tok(2 * D_MODEL), tok(PLE_DIM)]
        + [_const_spec(c.shape) for c in consts],
        out_specs=tok(D_MODEL),
        out_shape=jax.ShapeDtypeStruct((t, D_MODEL), F32),
        compiler_params=pltpu.CompilerParams(
            dimension_semantics=("arbitrary",), vmem_limit_bytes=VMEM_LIMIT),
        name="output_stage",
    )(x2d, attn, rec, gates, p2d, *consts)


def _pad_heads(w, heads, width):
    rows = w.shape[0]
    w = w.reshape(rows, heads, width)
    w = jnp.pad(w, ((0, 0), (0, 0), (0, HEAD_PAD - width)))
    return w.reshape(rows, heads * HEAD_PAD)


def _qk_head_layout(nope, rope):
    like = nope if nope is not None else rope

    def z(width):
        return jnp.zeros(like.shape[:-1] + (width,), like.dtype)

    n0 = nope[..., :HEAD_HALF_NOPE] if nope is not None else z(HEAD_HALF_NOPE)
    n1 = nope[..., HEAD_HALF_NOPE:] if nope is not None else z(HEAD_HALF_NOPE)
    x1 = rope[..., :ROPE_HALF] if rope is not None else z(ROPE_HALF)
    x2 = rope[..., ROPE_HALF:] if rope is not None else z(ROPE_HALF)
    return jnp.concatenate([n0, x1, z(ROPE_HALF), n1, x2, z(ROPE_HALF)], axis=-1)


def _row(v):
    return v.reshape(1, -1).astype(F32)


def kernel(x, p, positions, mix_norm_g, w_in, q_a_norm_g, w_uq, kv_a_norm_g, w_ukv, q_norm_g, k_norm_g, hg_lb_logits, hg_out_norm_g, w_branch, w_out, ffn_norm_g, w_ffn_gate, w_ffn_up, w_ffn_down, ple_gate_norm_g, w_ple_gate, w_ple_proj, ple_post_norm_g):
    b, s, d = x.shape
    depth = w_in.shape[0]
    lower_bounds = jnp.cumsum(jax.nn.softmax(hg_lb_logits.astype(F32), axis=0), axis=0)
    pos3d = positions.reshape(b * s // TOK_TILE, 1, TOK_TILE)
    x2d = x.reshape(b * s, d)
    q_scale = QK_HEAD_DIM ** -0.5 * math.log2(math.e)

    for layer in range(depth):
        wi = w_in[layer]
        wall = _input_weight_layout(wi.T)
        wq3 = w_uq[layer].reshape(Q_LORA_RANK, MLA_HEADS, QK_HEAD_DIM)
        wuq = _qk_head_layout(wq3[..., :QK_NOPE_DIM], wq3[..., QK_NOPE_DIM:])
        wuq = wuq.reshape(Q_LORA_RANK, MLA_HEADS * HEAD_PAD).astype(BF16)
        wukv = w_ukv[layer].reshape(KV_LORA_RANK, MLA_HEADS, QK_NOPE_DIM + V_HEAD_DIM)
        wk = _qk_head_layout(wukv[..., :QK_NOPE_DIM], None).reshape(KV_LORA_RANK, -1)
        wv = _pad_heads(wukv[:, :, QK_NOPE_DIM:].reshape(KV_LORA_RANK, -1), MLA_HEADS, V_HEAD_DIM)
        wkv = jnp.concatenate([wk, wv], axis=1).astype(BF16)
        gq = q_norm_g[layer].astype(F32) * q_scale
        gk = k_norm_g[layer].astype(F32)

        side = (w_branch[layer].reshape(2 * BRANCH_WIDTH, d), w_out[layer], w_ffn_gate[layer],
                w_ffn_up[layer], w_ffn_down[layer], w_ple_gate[layer], w_ple_proj[layer])
        (q, k, v, hq, hf, hi, hg, gates), side_bf = _input_stage(
            x2d, pos3d, b, s, _row(mix_norm_g[layer]), wall, _row(q_a_norm_g[layer]), wuq,
            _row(kv_a_norm_g[layer]), wkv,
            _row(_qk_head_layout(gq[:QK_NOPE_DIM], gq[QK_NOPE_DIM:])),
            _row(_qk_head_layout(gk[:QK_NOPE_DIM], gk[QK_NOPE_DIM:])), side)
        wb, wout, wgate, wup, wdown, wpg, wpp = side_bf

        attn = _attention_stage(q, k, v)
        rec = _hgrn_stage(hq.reshape(b, s, HG_WIDTH), hf.reshape(b, s, HG_WIDTH),
                          hi.reshape(b, s, HG_WIDTH), hg.reshape(b, s, HG_WIDTH),
                          _row(lower_bounds[layer]), _row(hg_out_norm_g[layer]))

        x2d = _output_stage(
            x2d, attn.reshape(b * s, BRANCH_WIDTH), rec.reshape(b * s, HG_WIDTH), gates,
            p[layer].reshape(b * s, PLE_DIM),
            wb, wout, _row(ffn_norm_g[layer]),
            wgate, wup, wdown, _row(ple_gate_norm_g[layer]), wpg, wpp,
            _row(ple_post_norm_g[layer]))
    return x2d.reshape(b, s, d)
```

```python
import functools
import math

import numpy as np

import jax
import jax.numpy as jnp
from jax import lax
from jax.experimental import pallas as pl
from jax.experimental.pallas import tpu as pltpu

D_MODEL = 1024
MLA_HEADS = 8
QK_NOPE_DIM = 64
QK_ROPE_DIM = 32
ROPE_HALF = QK_ROPE_DIM // 2
QK_HEAD_DIM = QK_NOPE_DIM + QK_ROPE_DIM
V_HEAD_DIM = 64
Q_LORA_RANK = 384
KV_LORA_RANK = 256
ROPE_BASE = 10000.0
HG_HEADS = 4
HG_DIM = 128
HG_WIDTH = HG_HEADS * HG_DIM
BRANCH_WIDTH = MLA_HEADS * V_HEAD_DIM
FFN_HIDDEN = 2816
PLE_DIM = 256
EPS = 1e-6

LANES = 128
BF16_ROWS = 16
HEAD_PAD = LANES
HEAD_HALF_NOPE = QK_NOPE_DIM // 2
W_LAT = Q_LORA_RANK + KV_LORA_RANK + HEAD_PAD
DENOM_LANE = V_HEAD_DIM

VMEM_LIMIT = 60 * 1024 * 1024

TOK_TILE = 512
ATT_TILE = 512
ATT_HEADS = 8
HG_BLOCK = 1024
HG_CHUNK = 64
HG_LEVELS = tuple(HG_CHUNK >> (i + 1) for i in range(HG_CHUNK.bit_length() - 1))
FFN_SPLITS = (1024, 1024, 768)

BF16 = jnp.bfloat16
F32 = jnp.float32


def _const_spec(shape):
    nd = len(shape)
    return pl.BlockSpec(shape, lambda *_: (0,) * nd, pipeline_mode=pl.Buffered(1))


def _rms(v, gain, width=None):
    n = v.shape[-1] if width is None else width
    ms = jnp.sum(v * v, axis=-1, keepdims=True) * (1.0 / n)
    return v * lax.rsqrt(ms + EPS) * gain


def _dot(a, b):
    return jnp.dot(a, b, preferred_element_type=F32)


def _dot_nt(a, b):
    return lax.dot_general(a, b, (((1,), (1,)), ((), ())), preferred_element_type=F32)


def _rope_tables(pos_row):
    tm = pos_row.shape[1]
    fidx = lax.broadcasted_iota(jnp.int32, (ROPE_HALF, 1), 0).astype(F32)
    inv_freq = jnp.exp(fidx * (-math.log(ROPE_BASE) * 2.0 / QK_ROPE_DIM))
    ang = pos_row.astype(F32) * inv_freq
    cos = jnp.cos(ang)
    sin = jnp.sin(ang)
    ones = jnp.ones((HEAD_HALF_NOPE, tm), F32)
    zeros_n = jnp.zeros((HEAD_HALF_NOPE, tm), F32)
    zeros_p = jnp.zeros((ROPE_HALF, tm), F32)
    cos_t = jnp.concatenate([ones, cos, zeros_p, ones, cos, zeros_p], axis=0)
    sin_t = jnp.concatenate([zeros_n, -sin, zeros_p, zeros_n, sin, zeros_p], axis=0)
    return cos_t.T, sin_t.T


def _input_kernel(x_ref, pos_ref, gmix_ref, wall_ref, gqa_ref, wuq_ref, gkva_ref, wkv_ref,
                  gq_ref, gk_ref, *rest):
    n_side = (len(rest) - 8) // 2
    side_in = rest[:n_side]
    q_ref, k_ref, v_ref, hq_ref, hf_ref, hi_ref, hg_ref, gates_ref = rest[n_side:n_side + 8]
    side_out = rest[n_side + 8:]
    for w_ref, wb_ref in zip(side_in, side_out):
        wb_ref[...] = w_ref[...].astype(BF16)

    x = x_ref[...]
    h = _rms(x, gmix_ref[...]).astype(BF16)

    lat = _dot(h, wall_ref[:, :W_LAT])
    cq = _rms(lat[:, :Q_LORA_RANK], gqa_ref[...]).astype(BF16)
    ckv = _rms(lat[:, Q_LORA_RANK:Q_LORA_RANK + KV_LORA_RANK], gkva_ref[...]).astype(BF16)
    k_rope = lat[:, Q_LORA_RANK + KV_LORA_RANK:]

    cos_tab, sin_tab = _rope_tables(pos_ref[0])
    gq = gq_ref[...]
    gk = gk_ref[...]
    q_cos = cos_tab * gq
    q_sin = sin_tab * pltpu.roll(gq, LANES // 2, 1)
    kr = k_rope * gk
    kr = kr * cos_tab + pltpu.roll(kr, LANES // 2, 1) * sin_tab
    kr_ss = jnp.sum(k_rope * k_rope, axis=-1, keepdims=True)

    q_all = _dot(cq, wuq_ref[...])
    kv_all = _dot(ckv, wkv_ref[...])
    lane = lax.broadcasted_iota(jnp.int32, (1, LANES), 1)
    one_lane = (lane == DENOM_LANE).astype(F32)
    inv_width = 1.0 / QK_HEAD_DIM
    for hd in range(MLA_HEADS):
        sl = slice(hd * HEAD_PAD, (hd + 1) * HEAD_PAD)
        t = q_all[:, sl]
        r = lax.rsqrt(jnp.sum(t * t, axis=-1, keepdims=True) * inv_width + EPS)
        q_ref[0, hd] = ((t * q_cos + pltpu.roll(t, LANES // 2, 1) * q_sin) * r).astype(BF16)
        t = kv_all[:, sl]
        r = lax.rsqrt((jnp.sum(t * t, axis=-1, keepdims=True) + kr_ss) * inv_width + EPS)
        k_ref[0, hd] = ((t * gk + kr) * r).astype(BF16)
        vsl = slice(MLA_HEADS * HEAD_PAD + hd * HEAD_PAD, MLA_HEADS * HEAD_PAD + (hd + 1) * HEAD_PAD)
        v_ref[0, hd] = (kv_all[:, vsl] + one_lane).astype(BF16)

    hh = _dot(h, wall_ref[:, W_LAT:W_LAT + 4 * HG_WIDTH])
    hq_ref[...] = hh[:, :HG_WIDTH].astype(BF16)
    hf_ref[...] = hh[:, HG_WIDTH:2 * HG_WIDTH]
    hi_ref[...] = hh[:, 2 * HG_WIDTH:3 * HG_WIDTH].astype(BF16)
    hg_ref[...] = hh[:, 3 * HG_WIDTH:].astype(BF16)

    gates_ref[...] = jax.nn.sigmoid(_dot(h, wall_ref[:, W_LAT + 4 * HG_WIDTH:])).astype(BF16)


W_PREP_COLS = 2432


def _input_weight_kernel(wt_ref, o_ref):
    c0 = Q_LORA_RANK + KV_LORA_RANK
    j = pl.program_id(0)
    n = W_PREP_COLS
    lat_full = c0 // n
    lat_rest = c0 - lat_full * n

    @pl.when(j != lat_full)
    def _():
        o_ref[...] = wt_ref[...].T.astype(BF16)

    @pl.when(j == lat_full)
    def _():
        d = wt_ref.shape[1]
        z_n = jnp.zeros((HEAD_HALF_NOPE, d), F32)
        z_p = jnp.zeros((ROPE_HALF, d), F32)
        parts = [wt_ref[:lat_rest, :], z_n, wt_ref[lat_rest:lat_rest + ROPE_HALF, :], z_p,
                 z_n, wt_ref[lat_rest + ROPE_HALF:lat_rest + QK_ROPE_DIM, :], z_p]
        tail = n - lat_rest - HEAD_PAD
        if tail:
            parts.append(wt_ref[lat_rest + QK_ROPE_DIM:lat_rest + QK_ROPE_DIM + tail, :])
        o_ref[...] = jnp.concatenate(parts, axis=0).T.astype(BF16)


def _input_weight_layout(wt):
    cols, rows = wt.shape
    out_cols = cols - QK_ROPE_DIM + HEAD_PAD
    n = W_PREP_COLS
    lat_full = (Q_LORA_RANK + KV_LORA_RANK) // n
    pad = HEAD_PAD - QK_ROPE_DIM

    def in_rows(j):
        return (pl.multiple_of(jnp.where(j <= lat_full, j * n, j * n - pad), QK_ROPE_DIM), 0)

    return pl.pallas_call(
        _input_weight_kernel,
        grid=(out_cols // n,),
        in_specs=[pl.BlockSpec((pl.Element(n), pl.Element(rows)), in_rows)],
        out_specs=pl.BlockSpec((rows, n), lambda j: (0, j)),
        out_shape=jax.ShapeDtypeStruct((rows, out_cols), BF16),
        compiler_params=pltpu.CompilerParams(
            dimension_semantics=("arbitrary",), vmem_limit_bytes=VMEM_LIMIT),
        name="input_weight_layout",
    )(wt)


def _side_cast_spec(rows, cols, steps):
    for share in (1, 2):
        blk, rem = divmod(rows * share, steps)
        if rem == 0 and blk % BF16_ROWS == 0:
            return pl.BlockSpec((blk, cols), lambda i: (i // share, 0))
    raise ValueError(f"no row block for a ({rows},{cols}) weight over {steps} steps")


def _input_stage(x2d, pos3d, b, s, gmix, wall, gqa, wuq, gkva, wkv, gq, gk, side_weights):
    t = x2d.shape[0]
    tm = TOK_TILE
    nt = s // tm
    steps = t // tm

    def tok(width):
        return pl.BlockSpec((tm, width), lambda i: (i, 0))

    def head_spec():
        return pl.BlockSpec((1, MLA_HEADS, tm, HEAD_PAD), lambda i: (i // nt, 0, i % nt, 0))

    head_shape = jax.ShapeDtypeStruct((b, MLA_HEADS, s, HEAD_PAD), BF16)
    consts = (gmix, wall, gqa, wuq, gkva, wkv, gq, gk)
    side_specs = [_side_cast_spec(w.shape[0], w.shape[1], steps) for w in side_weights]
    outs = pl.pallas_call(
        _input_kernel,
        grid=(steps,),
        in_specs=[tok(D_MODEL), pl.BlockSpec((1, 1, tm), lambda i: (i, 0, 0))]
        + [_const_spec(c.shape) for c in consts] + side_specs,
        out_specs=[head_spec(), head_spec(), head_spec(),
                   tok(HG_WIDTH), tok(HG_WIDTH), tok(HG_WIDTH), tok(HG_WIDTH), tok(2 * D_MODEL)]
        + side_specs,
        out_shape=[head_shape, head_shape, head_shape,
                   jax.ShapeDtypeStruct((t, HG_WIDTH), BF16),
                   jax.ShapeDtypeStruct((t, HG_WIDTH), F32),
                   jax.ShapeDtypeStruct((t, HG_WIDTH), BF16),
                   jax.ShapeDtypeStruct((t, HG_WIDTH), BF16),
                   jax.ShapeDtypeStruct((t, 2 * D_MODEL), BF16)]
        + [jax.ShapeDtypeStruct(w.shape, BF16) for w in side_weights],
        compiler_params=pltpu.CompilerParams(
            dimension_semantics=("arbitrary",), vmem_limit_bytes=VMEM_LIMIT),
        name="input_stage",
    )(x2d, pos3d, *consts, *side_weights)
    return outs[:8], outs[8:]


def _attention_kernel(q_ref, k_ref, v_ref, o_ref, m_ref, acc_ref, s_ref):
    qi = pl.program_id(2)
    tq = ATT_TILE
    half = tq // 2
    m_ref[...] = jnp.full(m_ref.shape, -jnp.inf, F32)
    acc_ref[...] = jnp.zeros(acc_ref.shape, F32)

    def block(j, rows, n_keys, visible):
        start = pl.multiple_of(j * tq, tq)
        for hd in range(ATT_HEADS):
            s_ref[hd, rows, :n_keys] = _dot_nt(q_ref[0, hd, rows, :],
                                               k_ref[0, hd, pl.ds(start, n_keys), :])
        for hd in range(ATT_HEADS):
            sc = s_ref[hd, rows, :n_keys]
            if visible is not None:
                sc = jnp.where(visible, sc, -jnp.inf)
            m_old = m_ref[hd, rows, :]
            m_new = jnp.maximum(m_old, jnp.max(sc, axis=-1, keepdims=True))
            p = jnp.exp2(sc - jnp.concatenate([m_new] * (n_keys // LANES), axis=1)).astype(BF16)
            acc_ref[hd, rows, :] = (jnp.exp2(m_old - m_new) * acc_ref[hd, rows, :]
                                    + _dot(p, v_ref[0, hd, pl.ds(start, n_keys), :]))
            m_ref[hd, rows, :] = m_new

    def body(t, carry):
        block(2 * t, slice(0, tq), tq, None)
        block(2 * t + 1, slice(0, tq), tq, None)
        return carry

    lax.fori_loop(0, qi // 2, body, 0)

    @pl.when(qi % 2 == 1)
    def _():
        block(qi - 1, slice(0, tq), tq, None)

    def lower(n_keys, shift):
        r_i = lax.broadcasted_iota(jnp.int32, (half, n_keys), 0)
        c_i = lax.broadcasted_iota(jnp.int32, (half, n_keys), 1)
        return c_i <= r_i + shift

    block(qi, slice(0, half), half, lower(half, 0))
    block(qi, slice(half, tq), tq, lower(tq, half))

    outs = []
    for hd in range(ATT_HEADS):
        acc = acc_ref[hd]
        outs.append(acc[:, :V_HEAD_DIM] / acc[:, DENOM_LANE:DENOM_LANE + 1])
    o_ref[0] = jnp.concatenate(outs, axis=-1).astype(BF16)


def _attention_stage(q, k, v):
    b, nh, s, _ = q.shape
    tq = ATT_TILE
    g = ATT_HEADS
    return pl.pallas_call(
        _attention_kernel,
        grid=(b, nh // g, s // tq),
        in_specs=[pl.BlockSpec((1, g, tq, HEAD_PAD), lambda bi, hp, i: (bi, hp, i, 0)),
                  pl.BlockSpec((1, g, s, HEAD_PAD), lambda bi, hp, i: (bi, hp, 0, 0)),
                  pl.BlockSpec((1, g, s, HEAD_PAD), lambda bi, hp, i: (bi, hp, 0, 0))],
        out_specs=pl.BlockSpec((1, tq, g * V_HEAD_DIM), lambda bi, hp, i: (bi, i, hp)),
        out_shape=jax.ShapeDtypeStruct((b, s, BRANCH_WIDTH), BF16),
        scratch_shapes=[pltpu.VMEM((g, tq, LANES), F32), pltpu.VMEM((g, tq, HEAD_PAD), F32),
                        pltpu.VMEM((g, tq, tq), F32)],
        compiler_params=pltpu.CompilerParams(
            dimension_semantics=("arbitrary", "arbitrary", "arbitrary"),
            vmem_limit_bytes=VMEM_LIMIT),
        name="causal_attention",
    )(q, k, v)


def _hgrn_span_matrix():
    c = HG_CHUNK
    x = np.arange(c)[:, None]
    y = np.arange(c)[None, :]
    mats = [(y <= x)]
    for b in HG_LEVELS:
        r = (x // (2 * b)) * (2 * b) + b
        mats.append((y > np.minimum(x, r)) & (y <= np.maximum(x, r)))
    m = np.concatenate(mats, axis=0).astype(np.float32)
    return np.concatenate([m, m], axis=1)


def _block_diag_rows(x):
    a, b = x[:, :HG_DIM], x[:, HG_DIM:]
    zero = jnp.zeros_like(a)
    return jnp.concatenate([jnp.concatenate([a, zero], axis=1),
                            jnp.concatenate([zero, b], axis=1)], axis=0)


def _hgrn_chunk_local(q, hf, v_bf, lb, span, right_rows, pair_masks):
    c = HG_CHUNK
    f = lb + (1.0 - lb) * jax.nn.sigmoid(hf)
    lf = jnp.log2(f)
    kk = 1.0 - f
    lf_hi = lf.astype(BF16)
    lf_lo = (lf - lf_hi.astype(F32)).astype(BF16)
    spans = _dot(span, jnp.concatenate([lf_hi, lf_lo], axis=0))
    cum = spans[:c]
    last = cum[c - 1:c, :]
    q_dec = (q * jnp.exp2(cum)).astype(BF16)
    k_dec = (kk * jnp.exp2(last - cum)).astype(BF16)
    q_bf = q.astype(BF16)
    k_bf = kk.astype(BF16)
    zs = [(jnp.where(right_rows[lvl], q, kk)
           * jnp.exp2(spans[(lvl + 1) * c:(lvl + 2) * c])).astype(BF16)
          for lvl in range(len(HG_LEVELS))]

    o_intra, incr = [], []
    for pair in range(HG_HEADS // 2):
        psl = slice(2 * pair * HG_DIM, (2 * pair + 2) * HG_DIM)
        att = jnp.where(pair_masks[0], _dot_nt(q_bf[:, psl], _block_diag_rows(k_bf[:, psl])), 0.0)
        for lvl in range(len(HG_LEVELS)):
            z = zs[lvl][:, psl]
            att = jnp.where(pair_masks[lvl + 1], _dot_nt(z, _block_diag_rows(z)), att)
        v_pair = v_bf[:, psl]
        o_intra.append(_dot(att.astype(BF16), _block_diag_rows(v_pair)))
        for hd in (2 * pair, 2 * pair + 1):
            sl = slice(hd * HG_DIM, (hd + 1) * HG_DIM)
            incr.append(_dot(v_bf[:, sl].astype(F32).T.astype(BF16), k_dec[:, sl]))
    return o_intra, q_dec, incr, jnp.exp2(last)


def _hgrn_kernel(hq_ref, hf_ref, hi_ref, hg_ref, lb_ref, gout_ref, span_ref, o_ref, state_ref):
    @pl.when(pl.program_id(1) == 0)
    def _():
        state_ref[...] = jnp.zeros(state_ref.shape, F32)

    c = HG_CHUNK
    t_i = lax.broadcasted_iota(jnp.int32, (c, 2 * c), 0)
    s_i = lax.broadcasted_iota(jnp.int32, (c, 2 * c), 1) % c
    row = lax.broadcasted_iota(jnp.int32, (c, HG_WIDTH), 0)
    pair_masks = [t_i == s_i] + [((t_i // b) ^ (s_i // b) == 1) & (s_i < t_i) for b in HG_LEVELS]
    right_rows = [(row // b) % 2 == 1 for b in HG_LEVELS]
    span = span_ref[...]
    lb = lb_ref[...]
    n_chunks = HG_BLOCK // c

    local = []
    for ci in range(n_chunks):
        rows = slice(ci * c, (ci + 1) * c)
        local.append(_hgrn_chunk_local(hq_ref[0, rows, :].astype(F32), hf_ref[0, rows, :],
                                       hi_ref[0, rows, :], lb, span, right_rows, pair_masks))

    gout = gout_ref[...]
    for pair in range(HG_HEADS // 2):
        hds = (2 * pair, 2 * pair + 1)
        psl = slice(2 * pair * HG_DIM, (2 * pair + 2) * HG_DIM)
        states = [state_ref[hd] for hd in hds]
        for ci in range(n_chunks):
            rows = slice(ci * c, (ci + 1) * c)
            o_intra, q_dec, incr, decay = local[ci]
            both = jnp.concatenate([st.astype(BF16) for st in states], axis=1)
            o_pair = o_intra[pair] + _dot_nt(q_dec[:, psl], _block_diag_rows(both))
            for n, hd in enumerate(hds):
                sl = slice(hd * HG_DIM, (hd + 1) * HG_DIM)
                states[n] = states[n] * decay[:, sl] + incr[hd]
                o = o_pair[:, n * HG_DIM:(n + 1) * HG_DIM]
                gate = hg_ref[0, rows, sl].astype(F32)
                o_ref[0, rows, sl] = (_rms(o, gout) * (gate * jax.nn.sigmoid(gate))).astype(BF16)
        for n, hd in enumerate(hds):
            state_ref[hd] = states[n]


def _hgrn_stage(hq, hf, hi, hg, lb, gout):
    span = jnp.asarray(_hgrn_span_matrix(), BF16)
    b, s, _ = hq.shape
    tb = HG_BLOCK
    tok = pl.BlockSpec((1, tb, HG_WIDTH), lambda bi, i: (bi, i, 0))
    return pl.pallas_call(
        _hgrn_kernel,
        grid=(b, s // tb),
        in_specs=[tok, tok, tok, tok, _const_spec(lb.shape), _const_spec(gout.shape),
                  _const_spec(span.shape)],
        out_specs=tok,
        out_shape=jax.ShapeDtypeStruct((b, s, HG_WIDTH), BF16),
        scratch_shapes=[pltpu.VMEM((HG_HEADS, HG_DIM, HG_DIM), F32)],
        compiler_params=pltpu.CompilerParams(
            dimension_semantics=("arbitrary", "arbitrary"), vmem_limit_bytes=VMEM_LIMIT),
        name="hgrn2_recurrence",
    )(hq, hf, hi, hg, lb, gout, span)


def _output_kernel(x_ref, attn_ref, rec_ref, gates_ref, p_ref,
                   wb_ref, wout_ref, gffn_ref, wgate_ref, wup_ref, wdown_ref,
                   gpg_ref, wpg_ref, wpp_ref, gpost_ref, o_ref):
    n_rows = x_ref.shape[0]
    halves = (slice(0, n_rows // 2), slice(n_rows // 2, n_rows))
    merged = []
    for rows in halves:
        y0 = _dot(attn_ref[rows, :], wb_ref[:BRANCH_WIDTH, :])
        y1 = _dot(rec_ref[rows, :], wb_ref[BRANCH_WIDTH:, :])
        g0 = gates_ref[rows, :D_MODEL].astype(F32)
        g1 = gates_ref[rows, D_MODEL:].astype(F32)
        merged.append((g0 * y0 + g1 * y1).astype(BF16))
    x1 = [x_ref[rows, :] + _dot(m, wout_ref[...]) for rows, m in zip(halves, merged)]
    e = [_rms(_dot(p_ref[rows, :].astype(BF16), wpp_ref[...]), gpost_ref[...]) for rows in halves]
    h2 = [_rms(v, gffn_ref[...]).astype(BF16) for v in x1]
    x2 = list(x1)
    lo = 0
    for width in FFN_SPLITS:
        for n in range(2):
            a = _dot(h2[n], wgate_ref[:, lo:lo + width])
            u = _dot(h2[n], wup_ref[:, lo:lo + width])
            z = (a * jax.nn.sigmoid(a) * u).astype(BF16)
            x2[n] = x2[n] + _dot(z, wdown_ref[lo:lo + width, :])
        lo += width
    for n, rows in enumerate(halves):
        g = jax.nn.sigmoid(_dot(_rms(x2[n], gpg_ref[...]).astype(BF16), wpg_ref[...]))
        o_ref[rows, :] = x2[n] + g * e[n]


def _output_stage(x2d, attn, rec, gates, p2d, wb, wout, gffn, wgate, wup, wdown,
                  gpg, wpg, wpp, gpost):
    t = x2d.shape[0]
    tm = TOK_TILE

    def tok(width):
        return pl.BlockSpec((tm, width), lambda i: (i, 0))

    consts = (wb, wout, gffn, wgate, wup, wdown, gpg, wpg, wpp, gpost)
    return pl.pallas_call(
        _output_kernel,
        grid=(t // tm,),
        in_specs=[tok(D_MODEL), tok(BRANCH_WIDTH), tok(HG_WIDTH), tok(2 * D_MODEL), tok(PLE_DIM)]
        + [_const_spec(c.shape) for c in consts],
        out_specs=tok(D_MODEL),
        out_shape=jax.ShapeDtypeStruct((t, D_MODEL), F32),
        compiler_params=pltpu.CompilerParams(
            dimension_semantics=("arbitrary",), vmem_limit_bytes=VMEM_LIMIT),
        name="output_stage",
    )(x2d, attn, rec, gates, p2d, *consts)


def _pad_heads(w, heads, width):
    rows = w.shape[0]
    w = w.reshape(rows, heads, width)
    w = jnp.pad(w, ((0, 0), (0, 0), (0, HEAD_PAD - width)))
    return w.reshape(rows, heads * HEAD_PAD)


def _qk_head_layout(nope, rope):
    like = nope if nope is not None else rope

    def z(width):
        return jnp.zeros(like.shape[:-1] + (width,), like.dtype)

    n0 = nope[..., :HEAD_HALF_NOPE] if nope is not None else z(HEAD_HALF_NOPE)
    n1 = nope[..., HEAD_HALF_NOPE:] if nope is not None else z(HEAD_HALF_NOPE)
    x1 = rope[..., :ROPE_HALF] if rope is not None else z(ROPE_HALF)
    x2 = rope[..., ROPE_HALF:] if rope is not None else z(ROPE_HALF)
    return jnp.concatenate([n0, x1, z(ROPE_HALF), n1, x2, z(ROPE_HALF)], axis=-1)


def _row(v):
    return v.reshape(1, -1).astype(F32)


def kernel(x, p, positions, mix_norm_g, w_in, q_a_norm_g, w_uq, kv_a_norm_g, w_ukv, q_norm_g, k_norm_g, hg_lb_logits, hg_out_norm_g, w_branch, w_out, ffn_norm_g, w_ffn_gate, w_ffn_up, w_ffn_down, ple_gate_norm_g, w_ple_gate, w_ple_proj, ple_post_norm_g):
    b, s, d = x.shape
    depth = w_in.shape[0]
    lower_bounds = jnp.cumsum(jax.nn.softmax(hg_lb_logits.astype(F32), axis=0), axis=0)
    pos3d = positions.reshape(b * s // TOK_TILE, 1, TOK_TILE)
    x2d = x.reshape(b * s, d)
    q_scale = QK_HEAD_DIM ** -0.5 * math.log2(math.e)

    for layer in range(depth):
        wi = w_in[layer]
        wall = _input_weight_layout(wi.T)
        wq3 = w_uq[layer].reshape(Q_LORA_RANK, MLA_HEADS, QK_HEAD_DIM)
        wuq = _qk_head_layout(wq3[..., :QK_NOPE_DIM], wq3[..., QK_NOPE_DIM:])
        wuq = wuq.reshape(Q_LORA_RANK, MLA_HEADS * HEAD_PAD).astype(BF16)
        wukv = w_ukv[layer].reshape(KV_LORA_RANK, MLA_HEADS, QK_NOPE_DIM + V_HEAD_DIM)
        wk = _qk_head_layout(wukv[..., :QK_NOPE_DIM], None).reshape(KV_LORA_RANK, -1)
        wv = _pad_heads(wukv[:, :, QK_NOPE_DIM:].reshape(KV_LORA_RANK, -1), MLA_HEADS, V_HEAD_DIM)
        wkv = jnp.concatenate([wk, wv], axis=1).astype(BF16)
        gq = q_norm_g[layer].astype(F32) * q_scale
        gk = k_norm_g[layer].astype(F32)

        side = (w_branch[layer].reshape(2 * BRANCH_WIDTH, d), w_out[layer], w_ffn_gate[layer],
                w_ffn_up[layer], w_ffn_down[layer], w_ple_gate[layer], w_ple_proj[layer])
        (q, k, v, hq, hf, hi, hg, gates), side_bf = _input_stage(
            x2d, pos3d, b, s, _row(mix_norm_g[layer]), wall, _row(q_a_norm_g[layer]), wuq,
            _row(kv_a_norm_g[layer]), wkv,
            _row(_qk_head_layout(gq[:QK_NOPE_DIM], gq[QK_NOPE_DIM:])),
            _row(_qk_head_layout(gk[:QK_NOPE_DIM], gk[QK_NOPE_DIM:])), side)
        wb, wout, wgate, wup, wdown, wpg, wpp = side_bf

        attn = _attention_stage(q, k, v)
        rec = _hgrn_stage(hq.reshape(b, s, HG_WIDTH), hf.reshape(b, s, HG_WIDTH),
                          hi.reshape(b, s, HG_WIDTH), hg.reshape(b, s, HG_WIDTH),
                          _row(lower_bounds[layer]), _row(hg_out_norm_g[layer]))

        x2d = _output_stage(
            x2d, attn.reshape(b * s, BRANCH_WIDTH), rec.reshape(b * s, HG_WIDTH), gates,
            p[layer].reshape(b * s, PLE_DIM),
            wb, wout, _row(ffn_norm_g[layer]),
            wgate, wup, wdown, _row(ple_gate_norm_g[layer]), wpg, wpp,
            _row(ple_post_norm_g[layer]))
    return x2d.reshape(b, s, d)
```

```python
import functools
import math

import numpy as np

import jax
import jax.numpy as jnp
from jax import lax
from jax.experimental import pallas as pl
from jax.experimental.pallas import tpu as pltpu

D_MODEL = 1024
MLA_HEADS = 8
QK_NOPE_DIM = 64
QK_ROPE_DIM = 32
ROPE_HALF = QK_ROPE_DIM // 2
QK_HEAD_DIM = QK_NOPE_DIM + QK_ROPE_DIM
V_HEAD_DIM = 64
Q_LORA_RANK = 384
KV_LORA_RANK = 256
ROPE_BASE = 10000.0
HG_HEADS = 4
HG_DIM = 128
HG_WIDTH = HG_HEADS * HG_DIM
BRANCH_WIDTH = MLA_HEADS * V_HEAD_DIM
FFN_HIDDEN = 2816
PLE_DIM = 256
EPS = 1e-6

LANES = 128
BF16_ROWS = 16
HEAD_PAD = LANES
HEAD_HALF_NOPE = QK_NOPE_DIM // 2
W_LAT = Q_LORA_RANK + KV_LORA_RANK + HEAD_PAD
DENOM_LANE = V_HEAD_DIM

VMEM_LIMIT = 60 * 1024 * 1024

TOK_TILE = 512
ROW_PARTS = 2
ATT_TILE = 512
ATT_HEADS = 8
HG_BLOCK = 1024
HG_CHUNK = 64
HG_LEVELS = tuple(HG_CHUNK >> (i + 1) for i in range(HG_CHUNK.bit_length() - 1))
FFN_SPLITS = (1024, 1024, 768)

BF16 = jnp.bfloat16
F32 = jnp.float32


def _const_spec(shape):
    nd = len(shape)
    return pl.BlockSpec(shape, lambda *_: (0,) * nd, pipeline_mode=pl.Buffered(1))


def _rms(v, gain, width=None):
    n = v.shape[-1] if width is None else width
    ms = jnp.sum(v * v, axis=-1, keepdims=True) * (1.0 / n)
    return v * lax.rsqrt(ms + EPS) * gain


def _dot(a, b):
    return jnp.dot(a, b, preferred_element_type=F32)


def _row_parts(n_rows):
    step = n_rows // ROW_PARTS
    return tuple(slice(i * step, (i + 1) * step) for i in range(ROW_PARTS))


def _dot_nt(a, b):
    return lax.dot_general(a, b, (((1,), (1,)), ((), ())), preferred_element_type=F32)


def _rope_tables(pos_row):
    tm = pos_row.shape[1]
    fidx = lax.broadcasted_iota(jnp.int32, (ROPE_HALF, 1), 0).astype(F32)
    inv_freq = jnp.exp(fidx * (-math.log(ROPE_BASE) * 2.0 / QK_ROPE_DIM))
    ang = pos_row.astype(F32) * inv_freq
    cos = jnp.cos(ang)
    sin = jnp.sin(ang)
    ones = jnp.ones((HEAD_HALF_NOPE, tm), F32)
    zeros_n = jnp.zeros((HEAD_HALF_NOPE, tm), F32)
    zeros_p = jnp.zeros((ROPE_HALF, tm), F32)
    cos_t = jnp.concatenate([ones, cos, zeros_p, ones, cos, zeros_p], axis=0)
    sin_t = jnp.concatenate([zeros_n, -sin, zeros_p, zeros_n, sin, zeros_p], axis=0)
    return cos_t.T, sin_t.T


def _input_kernel(x_ref, pos_ref, gmix_ref, wall_ref, gqa_ref, wuq_ref, gkva_ref, wkv_ref,
                  gq_ref, gk_ref, *rest):
    n_side = (len(rest) - 8) // 2
    side_in = rest[:n_side]
    q_ref, k_ref, v_ref, hq_ref, hf_ref, hi_ref, hg_ref, gates_ref = rest[n_side:n_side + 8]
    side_out = rest[n_side + 8:]
    for w_ref, wb_ref in zip(side_in, side_out):
        wb_ref[...] = w_ref[...].astype(BF16)

    halves = _row_parts(x_ref.shape[0])
    gq = gq_ref[...]
    gk = gk_ref[...]
    lane = lax.broadcasted_iota(jnp.int32, (1, LANES), 1)
    one_lane = (lane == DENOM_LANE).astype(F32)
    inv_width = 1.0 / QK_HEAD_DIM

    h = [_rms(x_ref[rows, :], gmix_ref[...]).astype(BF16) for rows in halves]
    lat = [_dot(v, wall_ref[:, :W_LAT]) for v in h]
    cq = [_rms(v[:, :Q_LORA_RANK], gqa_ref[...]).astype(BF16) for v in lat]
    ckv = [_rms(v[:, Q_LORA_RANK:Q_LORA_RANK + KV_LORA_RANK], gkva_ref[...]).astype(BF16)
           for v in lat]
    q_all = [_dot(v, wuq_ref[...]) for v in cq]
    kv_all = [_dot(v, wkv_ref[...]) for v in ckv]

    for n, rows in enumerate(halves):
        k_rope = lat[n][:, Q_LORA_RANK + KV_LORA_RANK:]
        cos_tab, sin_tab = _rope_tables(pos_ref[0, :, rows])
        q_cos = cos_tab * gq
        q_sin = sin_tab * pltpu.roll(gq, LANES // 2, 1)
        kr = k_rope * gk
        kr = kr * cos_tab + pltpu.roll(kr, LANES // 2, 1) * sin_tab
        kr_ss = jnp.sum(k_rope * k_rope, axis=-1, keepdims=True)
        for hd in range(MLA_HEADS):
            sl = slice(hd * HEAD_PAD, (hd + 1) * HEAD_PAD)
            t = q_all[n][:, sl]
            r = lax.rsqrt(jnp.sum(t * t, axis=-1, keepdims=True) * inv_width + EPS)
            q_ref[0, hd, rows, :] = ((t * q_cos + pltpu.roll(t, LANES // 2, 1) * q_sin) * r).astype(BF16)
            t = kv_all[n][:, sl]
            r = lax.rsqrt((jnp.sum(t * t, axis=-1, keepdims=True) + kr_ss) * inv_width + EPS)
            k_ref[0, hd, rows, :] = ((t * gk + kr) * r).astype(BF16)
            vsl = slice(MLA_HEADS * HEAD_PAD + hd * HEAD_PAD, MLA_HEADS * HEAD_PAD + (hd + 1) * HEAD_PAD)
            v_ref[0, hd, rows, :] = (kv_all[n][:, vsl] + one_lane).astype(BF16)

    for n, rows in enumerate(halves):
        hh = _dot(h[n], wall_ref[:, W_LAT:W_LAT + 4 * HG_WIDTH])
        hq_ref[rows, :] = hh[:, :HG_WIDTH].astype(BF16)
        hf_ref[rows, :] = hh[:, HG_WIDTH:2 * HG_WIDTH]
        hi_ref[rows, :] = hh[:, 2 * HG_WIDTH:3 * HG_WIDTH].astype(BF16)
        hg_ref[rows, :] = hh[:, 3 * HG_WIDTH:].astype(BF16)
    for n, rows in enumerate(halves):
        gates_ref[rows, :] = jax.nn.sigmoid(
            _dot(h[n], wall_ref[:, W_LAT + 4 * HG_WIDTH:])).astype(BF16)


W_PREP_COLS = 2432


def _input_weight_kernel(wt_ref, o_ref):
    c0 = Q_LORA_RANK + KV_LORA_RANK
    j = pl.program_id(0)
    n = W_PREP_COLS
    lat_full = c0 // n
    lat_rest = c0 - lat_full * n

    @pl.when(j != lat_full)
    def _():
        o_ref[...] = wt_ref[...].T.astype(BF16)

    @pl.when(j == lat_full)
    def _():
        d = wt_ref.shape[1]
        z_n = jnp.zeros((HEAD_HALF_NOPE, d), F32)
        z_p = jnp.zeros((ROPE_HALF, d), F32)
        parts = [wt_ref[:lat_rest, :], z_n, wt_ref[lat_rest:lat_rest + ROPE_HALF, :], z_p,
                 z_n, wt_ref[lat_rest + ROPE_HALF:lat_rest + QK_ROPE_DIM, :], z_p]
        tail = n - lat_rest - HEAD_PAD
        if tail:
            parts.append(wt_ref[lat_rest + QK_ROPE_DIM:lat_rest + QK_ROPE_DIM + tail, :])
        o_ref[...] = jnp.concatenate(parts, axis=0).T.astype(BF16)


def _input_weight_layout(wt):
    cols, rows = wt.shape
    out_cols = cols - QK_ROPE_DIM + HEAD_PAD
    n = W_PREP_COLS
    lat_full = (Q_LORA_RANK + KV_LORA_RANK) // n
    pad = HEAD_PAD - QK_ROPE_DIM

    def in_rows(j):
        return (pl.multiple_of(jnp.where(j <= lat_full, j * n, j * n - pad), QK_ROPE_DIM), 0)

    return pl.pallas_call(
        _input_weight_kernel,
        grid=(out_cols // n,),
        in_specs=[pl.BlockSpec((pl.Element(n), pl.Element(rows)), in_rows)],
        out_specs=pl.BlockSpec((rows, n), lambda j: (0, j)),
        out_shape=jax.ShapeDtypeStruct((rows, out_cols), BF16),
        compiler_params=pltpu.CompilerParams(
            dimension_semantics=("arbitrary",), vmem_limit_bytes=VMEM_LIMIT),
        name="input_weight_layout",
    )(wt)


def _side_cast_spec(rows, cols, steps):
    for share in (1, 2):
        blk, rem = divmod(rows * share, steps)
        if rem == 0 and blk % BF16_ROWS == 0:
            return pl.BlockSpec((blk, cols), lambda i: (i // share, 0))
    raise ValueError(f"no row block for a ({rows},{cols}) weight over {steps} steps")


def _input_stage(x2d, pos3d, b, s, gmix, wall, gqa, wuq, gkva, wkv, gq, gk, side_weights):
    t = x2d.shape[0]
    tm = TOK_TILE
    nt = s // tm
    steps = t // tm

    def tok(width):
        return pl.BlockSpec((tm, width), lambda i: (i, 0))

    def head_spec():
        return pl.BlockSpec((1, MLA_HEADS, tm, HEAD_PAD), lambda i: (i // nt, 0, i % nt, 0))

    head_shape = jax.ShapeDtypeStruct((b, MLA_HEADS, s, HEAD_PAD), BF16)
    consts = (gmix, wall, gqa, wuq, gkva, wkv, gq, gk)
    side_specs = [_side_cast_spec(w.shape[0], w.shape[1], steps) for w in side_weights]
    outs = pl.pallas_call(
        _input_kernel,
        grid=(steps,),
        in_specs=[tok(D_MODEL), pl.BlockSpec((1, 1, tm), lambda i: (i, 0, 0))]
        + [_const_spec(c.shape) for c in consts] + side_specs,
        out_specs=[head_spec(), head_spec(), head_spec(),
                   tok(HG_WIDTH), tok(HG_WIDTH), tok(HG_WIDTH), tok(HG_WIDTH), tok(2 * D_MODEL)]
        + side_specs,
        out_shape=[head_shape, head_shape, head_shape,
                   jax.ShapeDtypeStruct((t, HG_WIDTH), BF16),
                   jax.ShapeDtypeStruct((t, HG_WIDTH), F32),
                   jax.ShapeDtypeStruct((t, HG_WIDTH), BF16),
                   jax.ShapeDtypeStruct((t, HG_WIDTH), BF16),
                   jax.ShapeDtypeStruct((t, 2 * D_MODEL), BF16)]
        + [jax.ShapeDtypeStruct(w.shape, BF16) for w in side_weights],
        compiler_params=pltpu.CompilerParams(
            dimension_semantics=("arbitrary",), vmem_limit_bytes=VMEM_LIMIT),
        name="input_stage",
    )(x2d, pos3d, *consts, *side_weights)
    return outs[:8], outs[8:]


def _attention_kernel(q_ref, k_ref, v_ref, o_ref, m_ref, acc_ref, s_ref):
    qi = pl.program_id(2)
    tq = ATT_TILE
    half = tq // 2
    m_ref[...] = jnp.full(m_ref.shape, -jnp.inf, F32)
    acc_ref[...] = jnp.zeros(acc_ref.shape, F32)

    all_rows = slice(0, tq)

    def scores(j, hd, rows, n_keys):
        start = pl.multiple_of(j * tq, tq)
        s_ref[hd, rows, :n_keys] = _dot_nt(q_ref[0, hd, rows, :],
                                           k_ref[0, hd, pl.ds(start, n_keys), :])

    def accumulate(j, hd, rows, n_keys, visible):
        start = pl.multiple_of(j * tq, tq)
        sc = s_ref[hd, rows, :n_keys]
        if visible is not None:
            sc = jnp.where(visible, sc, -jnp.inf)
        m_old = m_ref[hd, rows, :]
        m_new = jnp.maximum(m_old, jnp.max(sc, axis=-1, keepdims=True))
        p = jnp.exp2(sc - jnp.concatenate([m_new] * (n_keys // LANES), axis=1)).astype(BF16)
        acc_ref[hd, rows, :] = (jnp.exp2(m_old - m_new) * acc_ref[hd, rows, :]
                                + _dot(p, v_ref[0, hd, pl.ds(start, n_keys), :]))
        m_ref[hd, rows, :] = m_new

    def block(j, rows, n_keys, visible):
        for hd in range(ATT_HEADS):
            scores(j, hd, rows, n_keys)
        for hd in range(ATT_HEADS):
            accumulate(j, hd, rows, n_keys, visible)

    def body(t, carry):
        for hd in range(ATT_HEADS):
            scores(2 * t, hd, all_rows, tq)
        for hd in range(ATT_HEADS):
            accumulate(2 * t, hd, all_rows, tq, None)
            scores(2 * t + 1, hd, all_rows, tq)
        for hd in range(ATT_HEADS):
            accumulate(2 * t + 1, hd, all_rows, tq, None)
        return carry

    lax.fori_loop(0, qi // 2, body, 0)

    @pl.when(qi % 2 == 1)
    def _():
        block(qi - 1, slice(0, tq), tq, None)

    def lower(n_keys, shift):
        r_i = lax.broadcasted_iota(jnp.int32, (half, n_keys), 0)
        c_i = lax.broadcasted_iota(jnp.int32, (half, n_keys), 1)
        return c_i <= r_i + shift

    block(qi, slice(0, half), half, lower(half, 0))
    block(qi, slice(half, tq), tq, lower(tq, half))

    outs = []
    for hd in range(ATT_HEADS):
        acc = acc_ref[hd]
        outs.append(acc[:, :V_HEAD_DIM] / acc[:, DENOM_LANE:DENOM_LANE + 1])
    o_ref[0] = jnp.concatenate(outs, axis=-1).astype(BF16)


def _attention_stage(q, k, v):
    b, nh, s, _ = q.shape
    tq = ATT_TILE
    g = ATT_HEADS
    return pl.pallas_call(
        _attention_kernel,
        grid=(b, nh // g, s // tq),
        in_specs=[pl.BlockSpec((1, g, tq, HEAD_PAD), lambda bi, hp, i: (bi, hp, i, 0)),
                  pl.BlockSpec((1, g, s, HEAD_PAD), lambda bi, hp, i: (bi, hp, 0, 0)),
                  pl.BlockSpec((1, g, s, HEAD_PAD), lambda bi, hp, i: (bi, hp, 0, 0))],
        out_specs=pl.BlockSpec((1, tq, g * V_HEAD_DIM), lambda bi, hp, i: (bi, i, hp)),
        out_shape=jax.ShapeDtypeStruct((b, s, BRANCH_WIDTH), BF16),
        scratch_shapes=[pltpu.VMEM((g, tq, LANES), F32), pltpu.VMEM((g, tq, HEAD_PAD), F32),
                        pltpu.VMEM((g, tq, tq), F32)],
        compiler_params=pltpu.CompilerParams(
            dimension_semantics=("arbitrary", "arbitrary", "arbitrary"),
            vmem_limit_bytes=VMEM_LIMIT),
        name="causal_attention",
    )(q, k, v)


def _hgrn_span_matrix():
    c = HG_CHUNK
    x = np.arange(c)[:, None]
    y = np.arange(c)[None, :]
    mats = [(y <= x)]
    for b in HG_LEVELS:
        r = (x // (2 * b)) * (2 * b) + b
        mats.append((y > np.minimum(x, r)) & (y <= np.maximum(x, r)))
    m = np.concatenate(mats, axis=0).astype(np.float32)
    return np.concatenate([m, m], axis=1)


def _block_diag_rows(x):
    a, b = x[:, :HG_DIM], x[:, HG_DIM:]
    zero = jnp.zeros_like(a)
    return jnp.concatenate([jnp.concatenate([a, zero], axis=1),
                            jnp.concatenate([zero, b], axis=1)], axis=0)


def _hgrn_chunk_local(q, hf, v_bf, lb, span, right_rows, pair_masks):
    c = HG_CHUNK
    f = lb + (1.0 - lb) * jax.nn.sigmoid(hf)
    lf = jnp.log2(f)
    kk = 1.0 - f
    lf_hi = lf.astype(BF16)
    lf_lo = (lf - lf_hi.astype(F32)).astype(BF16)
    spans = _dot(span, jnp.concatenate([lf_hi, lf_lo], axis=0))
    cum = spans[:c]
    last = cum[c - 1:c, :]
    q_dec = (q * jnp.exp2(cum)).astype(BF16)
    k_dec = (kk * jnp.exp2(last - cum)).astype(BF16)
    q_bf = q.astype(BF16)
    k_bf = kk.astype(BF16)
    zs = [(jnp.where(right_rows[lvl], q, kk)
           * jnp.exp2(spans[(lvl + 1) * c:(lvl + 2) * c])).astype(BF16)
          for lvl in range(len(HG_LEVELS))]

    o_intra, incr = [], []
    for pair in range(HG_HEADS // 2):
        psl = slice(2 * pair * HG_DIM, (2 * pair + 2) * HG_DIM)
        att = jnp.where(pair_masks[0], _dot_nt(q_bf[:, psl], _block_diag_rows(k_bf[:, psl])), 0.0)
        for lvl in range(len(HG_LEVELS)):
            z = zs[lvl][:, psl]
            att = jnp.where(pair_masks[lvl + 1], _dot_nt(z, _block_diag_rows(z)), att)
        v_pair = v_bf[:, psl]
        o_intra.append(_dot(att.astype(BF16), _block_diag_rows(v_pair)))
        for hd in (2 * pair, 2 * pair + 1):
            sl = slice(hd * HG_DIM, (hd + 1) * HG_DIM)
            incr.append(_dot(v_bf[:, sl].astype(F32).T.astype(BF16), k_dec[:, sl]))
    return o_intra, q_dec, incr, jnp.exp2(last)


def _hgrn_kernel(hq_ref, hf_ref, hi_ref, hg_ref, lb_ref, gout_ref, span_ref, o_ref, state_ref):
    @pl.when(pl.program_id(1) == 0)
    def _():
        state_ref[...] = jnp.zeros(state_ref.shape, F32)

    c = HG_CHUNK
    t_i = lax.broadcasted_iota(jnp.int32, (c, 2 * c), 0)
    s_i = lax.broadcasted_iota(jnp.int32, (c, 2 * c), 1) % c
    row = lax.broadcasted_iota(jnp.int32, (c, HG_WIDTH), 0)
    pair_masks = [t_i == s_i] + [((t_i // b) ^ (s_i // b) == 1) & (s_i < t_i) for b in HG_LEVELS]
    right_rows = [(row // b) % 2 == 1 for b in HG_LEVELS]
    span = span_ref[...]
    lb = lb_ref[...]
    n_chunks = HG_BLOCK // c

    local = []
    for ci in range(n_chunks):
        rows = slice(ci * c, (ci + 1) * c)
        local.append(_hgrn_chunk_local(hq_ref[0, rows, :].astype(F32), hf_ref[0, rows, :],
                                       hi_ref[0, rows, :], lb, span, right_rows, pair_masks))

    gout = gout_ref[...]
    for pair in range(HG_HEADS // 2):
        hds = (2 * pair, 2 * pair + 1)
        psl = slice(2 * pair * HG_DIM, (2 * pair + 2) * HG_DIM)
        states = [state_ref[hd] for hd in hds]
        for ci in range(n_chunks):
            rows = slice(ci * c, (ci + 1) * c)
            o_intra, q_dec, incr, decay = local[ci]
            both = jnp.concatenate([st.astype(BF16) for st in states], axis=1)
            o_pair = o_intra[pair] + _dot_nt(q_dec[:, psl], _block_diag_rows(both))
            for n, hd in enumerate(hds):
                sl = slice(hd * HG_DIM, (hd + 1) * HG_DIM)
                states[n] = states[n] * decay[:, sl] + incr[hd]
                o = o_pair[:, n * HG_DIM:(n + 1) * HG_DIM]
                gate = hg_ref[0, rows, sl].astype(F32)
                o_ref[0, rows, sl] = (_rms(o, gout) * (gate * jax.nn.sigmoid(gate))).astype(BF16)
        for n, hd in enumerate(hds):
            state_ref[hd] = states[n]


def _hgrn_stage(hq, hf, hi, hg, lb, gout):
    span = jnp.asarray(_hgrn_span_matrix(), BF16)
    b, s, _ = hq.shape
    tb = HG_BLOCK
    tok = pl.BlockSpec((1, tb, HG_WIDTH), lambda bi, i: (bi, i, 0))
    return pl.pallas_call(
        _hgrn_kernel,
        grid=(b, s // tb),
        in_specs=[tok, tok, tok, tok, _const_spec(lb.shape), _const_spec(gout.shape),
                  _const_spec(span.shape)],
        out_specs=tok,
        out_shape=jax.ShapeDtypeStruct((b, s, HG_WIDTH), BF16),
        scratch_shapes=[pltpu.VMEM((HG_HEADS, HG_DIM, HG_DIM), F32)],
        compiler_params=pltpu.CompilerParams(
            dimension_semantics=("arbitrary", "arbitrary"), vmem_limit_bytes=VMEM_LIMIT),
        name="hgrn2_recurrence",
    )(hq, hf, hi, hg, lb, gout, span)


def _output_kernel(x_ref, attn_ref, rec_ref, gates_ref, p_ref,
                   wb_ref, wout_ref, gffn_ref, wgate_ref, wup_ref, wdown_ref,
                   gpg_ref, wpg_ref, wpp_ref, gpost_ref, o_ref):
    halves = _row_parts(x_ref.shape[0])
    merged = []
    for rows in halves:
        y0 = _dot(attn_ref[rows, :], wb_ref[:BRANCH_WIDTH, :])
        y1 = _dot(rec_ref[rows, :], wb_ref[BRANCH_WIDTH:, :])
        g0 = gates_ref[rows, :D_MODEL].astype(F32)
        g1 = gates_ref[rows, D_MODEL:].astype(F32)
        merged.append((g0 * y0 + g1 * y1).astype(BF16))
    x1 = [x_ref[rows, :] + _dot(m, wout_ref[...]) for rows, m in zip(halves, merged)]
    e = [_rms(_dot(p_ref[rows, :].astype(BF16), wpp_ref[...]), gpost_ref[...]) for rows in halves]
    h2 = [_rms(v, gffn_ref[...]).astype(BF16) for v in x1]
    x2 = list(x1)
    lo = 0
    for width in FFN_SPLITS:
        for n in range(len(halves)):
            a = _dot(h2[n], wgate_ref[:, lo:lo + width])
            u = _dot(h2[n], wup_ref[:, lo:lo + width])
            z = (a * jax.nn.sigmoid(a) * u).astype(BF16)
            x2[n] = x2[n] + _dot(z, wdown_ref[lo:lo + width, :])
        lo += width
    for n, rows in enumerate(halves):
        g = jax.nn.sigmoid(_dot(_rms(x2[n], gpg_ref[...]).astype(BF16), wpg_ref[...]))
        o_ref[rows, :] = x2[n] + g * e[n]


def _output_stage(x2d, attn, rec, gates, p2d, wb, wout, gffn, wgate, wup, wdown,
                  gpg, wpg, wpp, gpost):
    t = x2d.shape[0]
    tm = TOK_TILE

    def tok(width):
        return pl.BlockSpec((tm, width), lambda i: (i, 0))

    consts = (wb, wout, gffn, wgate, wup, wdown, gpg, wpg, wpp, gpost)
    return pl.pallas_call(
        _output_kernel,
        grid=(t // tm,),
        in_specs=[tok(D_MODEL), tok(BRANCH_WIDTH), tok(HG_WIDTH), tok(2 * D_MODEL), tok(PLE_DIM)]
        + [_const_spec(c.shape) for c in consts],
        out_specs=tok(D_MODEL),
        out_shape=jax.ShapeDtypeStruct((t, D_MODEL), F32),
        compiler_params=pltpu.CompilerParams(
            dimension_semantics=("arbitrary",), vmem_limit_bytes=VMEM_LIMIT),
        name="output_stage",
    )(x2d, attn, rec, gates, p2d, *consts)


def _pad_heads(w, heads, width):
    rows = w.shape[0]
    w = w.reshape(rows, heads, width)
    w = jnp.pad(w, ((0, 0), (0, 0), (0, HEAD_PAD - width)))
    return w.reshape(rows, heads * HEAD_PAD)


def _qk_head_layout(nope, rope):
    like = nope if nope is not None else rope

    def z(width):
        return jnp.zeros(like.shape[:-1] + (width,), like.dtype)

    n0 = nope[..., :HEAD_HALF_NOPE] if nope is not None else z(HEAD_HALF_NOPE)
    n1 = nope[..., HEAD_HALF_NOPE:] if nope is not None else z(HEAD_HALF_NOPE)
    x1 = rope[..., :ROPE_HALF] if rope is not None else z(ROPE_HALF)
    x2 = rope[..., ROPE_HALF:] if rope is not None else z(ROPE_HALF)
    return jnp.concatenate([n0, x1, z(ROPE_HALF), n1, x2, z(ROPE_HALF)], axis=-1)


def _row(v):
    return v.reshape(1, -1).astype(F32)


def kernel(x, p, positions, mix_norm_g, w_in, q_a_norm_g, w_uq, kv_a_norm_g, w_ukv, q_norm_g, k_norm_g, hg_lb_logits, hg_out_norm_g, w_branch, w_out, ffn_norm_g, w_ffn_gate, w_ffn_up, w_ffn_down, ple_gate_norm_g, w_ple_gate, w_ple_proj, ple_post_norm_g):
    b, s, d = x.shape
    depth = w_in.shape[0]
    lower_bounds = jnp.cumsum(jax.nn.softmax(hg_lb_logits.astype(F32), axis=0), axis=0)
    pos3d = positions.reshape(b * s // TOK_TILE, 1, TOK_TILE)
    x2d = x.reshape(b * s, d)
    q_scale = QK_HEAD_DIM ** -0.5 * math.log2(math.e)

    for layer in range(depth):
        wi = w_in[layer]
        wall = _input_weight_layout(wi.T)
        wq3 = w_uq[layer].reshape(Q_LORA_RANK, MLA_HEADS, QK_HEAD_DIM)
        wuq = _qk_head_layout(wq3[..., :QK_NOPE_DIM], wq3[..., QK_NOPE_DIM:])
        wuq = wuq.reshape(Q_LORA_RANK, MLA_HEADS * HEAD_PAD).astype(BF16)
        wukv = w_ukv[layer].reshape(KV_LORA_RANK, MLA_HEADS, QK_NOPE_DIM + V_HEAD_DIM)
        wk = _qk_head_layout(wukv[..., :QK_NOPE_DIM], None).reshape(KV_LORA_RANK, -1)
        wv = _pad_heads(wukv[:, :, QK_NOPE_DIM:].reshape(KV_LORA_RANK, -1), MLA_HEADS, V_HEAD_DIM)
        wkv = jnp.concatenate([wk, wv], axis=1).astype(BF16)
        gq = q_norm_g[layer].astype(F32) * q_scale
        gk = k_norm_g[layer].astype(F32)

        side = (w_branch[layer].reshape(2 * BRANCH_WIDTH, d), w_out[layer], w_ffn_gate[layer],
                w_ffn_up[layer], w_ffn_down[layer], w_ple_gate[layer], w_ple_proj[layer])
        (q, k, v, hq, hf, hi, hg, gates), side_bf = _input_stage(
            x2d, pos3d, b, s, _row(mix_norm_g[layer]), wall, _row(q_a_norm_g[layer]), wuq,
            _row(kv_a_norm_g[layer]), wkv,
            _row(_qk_head_layout(gq[:QK_NOPE_DIM], gq[QK_NOPE_DIM:])),
            _row(_qk_head_layout(gk[:QK_NOPE_DIM], gk[QK_NOPE_DIM:])), side)
        wb, wout, wgate, wup, wdown, wpg, wpp = side_bf

        attn = _attention_stage(q, k, v)
        rec = _hgrn_stage(hq.reshape(b, s, HG_WIDTH), hf.reshape(b, s, HG_WIDTH),
                          hi.reshape(b, s, HG_WIDTH), hg.reshape(b, s, HG_WIDTH),
                          _row(lower_bounds[layer]), _row(hg_out_norm_g[layer]))

        x2d = _output_stage(
            x2d, attn.reshape(b * s, BRANCH_WIDTH), rec.reshape(b * s, HG_WIDTH), gates,
            p[layer].reshape(b * s, PLE_DIM),
            wb, wout, _row(ffn_norm_g[layer]),
            wgate, wup, wdown, _row(ple_gate_norm_g[layer]), wpg, wpp,
            _row(ple_post_norm_g[layer]))
    return x2d.reshape(b, s, d)
```

```python
import math

import numpy as np

import jax
import jax.numpy as jnp
from jax import lax
from jax.experimental import pallas as pl
from jax.experimental.pallas import tpu as pltpu

D_MODEL = 1024
MLA_HEADS = 8
QK_NOPE_DIM = 64
QK_ROPE_DIM = 32
ROPE_HALF = QK_ROPE_DIM // 2
QK_HEAD_DIM = QK_NOPE_DIM + QK_ROPE_DIM
V_HEAD_DIM = 64
Q_LORA_RANK = 384
KV_LORA_RANK = 256
ROPE_BASE = 10000.0
HG_HEADS = 4
HG_DIM = 128
HG_WIDTH = HG_HEADS * HG_DIM
BRANCH_WIDTH = MLA_HEADS * V_HEAD_DIM
FFN_HIDDEN = 2816
PLE_DIM = 256
EPS = 1e-6

LANES = 128
BF16_ROWS = 16
HEAD_PAD = LANES
HEAD_HALF_NOPE = QK_NOPE_DIM // 2
W_LAT = Q_LORA_RANK + KV_LORA_RANK + HEAD_PAD
DENOM_LANE = V_HEAD_DIM

VMEM_LIMIT = 60 * 1024 * 1024

TOK_TILE = 512
ROW_PARTS = 2
ATT_TILE = 512
ATT_HEADS = 8
HG_BLOCK = 1024
HG_CHUNK = 64
HG_LEVELS = tuple(HG_CHUNK >> (i + 1) for i in range(HG_CHUNK.bit_length() - 1))
FFN_SPLITS = (1024, 1024, 768)

BF16 = jnp.bfloat16
F32 = jnp.float32


def _const_spec(shape):
    nd = len(shape)
    return pl.BlockSpec(shape, lambda *_: (0,) * nd, pipeline_mode=pl.Buffered(1))


def _rms(v, gain):
    ms = jnp.sum(v * v, axis=-1, keepdims=True) * (1.0 / v.shape[-1])
    return v * lax.rsqrt(ms + EPS) * gain


def _dot(a, b):
    return jnp.dot(a, b, preferred_element_type=F32)


def _row_parts(n_rows):
    step = n_rows // ROW_PARTS
    return tuple(slice(i * step, (i + 1) * step) for i in range(ROW_PARTS))


def _dot_nt(a, b):
    return lax.dot_general(a, b, (((1,), (1,)), ((), ())), preferred_element_type=F32)


def _rope_tables(pos_row):
    tm = pos_row.shape[1]
    fidx = lax.broadcasted_iota(jnp.int32, (ROPE_HALF, 1), 0).astype(F32)
    inv_freq = jnp.exp(fidx * (-math.log(ROPE_BASE) * 2.0 / QK_ROPE_DIM))
    ang = pos_row.astype(F32) * inv_freq
    cos = jnp.cos(ang)
    sin = jnp.sin(ang)
    ones = jnp.ones((HEAD_HALF_NOPE, tm), F32)
    zeros_n = jnp.zeros((HEAD_HALF_NOPE, tm), F32)
    zeros_p = jnp.zeros((ROPE_HALF, tm), F32)
    cos_t = jnp.concatenate([ones, cos, zeros_p, ones, cos, zeros_p], axis=0)
    sin_t = jnp.concatenate([zeros_n, -sin, zeros_p, zeros_n, sin, zeros_p], axis=0)
    return cos_t.T, sin_t.T


def _input_kernel(x_ref, pos_ref, gmix_ref, wall_ref, gqa_ref, wuq_ref, gkva_ref, wkv_ref,
                  gq_ref, gk_ref, *rest):
    n_side = (len(rest) - 8) // 2
    side_in = rest[:n_side]
    q_ref, k_ref, v_ref, hq_ref, hf_ref, hi_ref, hg_ref, gates_ref = rest[n_side:n_side + 8]
    side_out = rest[n_side + 8:]
    for w_ref, wb_ref in zip(side_in, side_out):
        wb_ref[...] = w_ref[...].astype(BF16)

    halves = _row_parts(x_ref.shape[0])
    gq = gq_ref[...]
    gk = gk_ref[...]
    lane = lax.broadcasted_iota(jnp.int32, (1, LANES), 1)
    one_lane = (lane == DENOM_LANE).astype(F32)
    inv_width = 1.0 / QK_HEAD_DIM

    h = [_rms(x_ref[rows, :], gmix_ref[...]).astype(BF16) for rows in halves]
    lat = [_dot(v, wall_ref[:, :W_LAT]) for v in h]
    cq = [_rms(v[:, :Q_LORA_RANK], gqa_ref[...]).astype(BF16) for v in lat]
    ckv = [_rms(v[:, Q_LORA_RANK:Q_LORA_RANK + KV_LORA_RANK], gkva_ref[...]).astype(BF16)
           for v in lat]
    q_all = [_dot(v, wuq_ref[...]) for v in cq]
    kv_all = [_dot(v, wkv_ref[...]) for v in ckv]

    for n, rows in enumerate(halves):
        k_rope = lat[n][:, Q_LORA_RANK + KV_LORA_RANK:]
        cos_tab, sin_tab = _rope_tables(pos_ref[0, :, rows])
        q_cos = cos_tab * gq
        q_sin = sin_tab * pltpu.roll(gq, LANES // 2, 1)
        kr = k_rope * gk
        kr = kr * cos_tab + pltpu.roll(kr, LANES // 2, 1) * sin_tab
        kr_ss = jnp.sum(k_rope * k_rope, axis=-1, keepdims=True)
        for hd in range(MLA_HEADS):
            sl = slice(hd * HEAD_PAD, (hd + 1) * HEAD_PAD)
            t = q_all[n][:, sl]
            r = lax.rsqrt(jnp.sum(t * t, axis=-1, keepdims=True) * inv_width + EPS)
            q_ref[0, hd, rows, :] = ((t * q_cos + pltpu.roll(t, LANES // 2, 1) * q_sin) * r).astype(BF16)
            t = kv_all[n][:, sl]
            r = lax.rsqrt((jnp.sum(t * t, axis=-1, keepdims=True) + kr_ss) * inv_width + EPS)
            k_ref[0, hd, rows, :] = ((t * gk + kr) * r).astype(BF16)
            vsl = slice(MLA_HEADS * HEAD_PAD + hd * HEAD_PAD, MLA_HEADS * HEAD_PAD + (hd + 1) * HEAD_PAD)
            v_ref[0, hd, rows, :] = (kv_all[n][:, vsl] + one_lane).astype(BF16)

    for n, rows in enumerate(halves):
        hh = _dot(h[n], wall_ref[:, W_LAT:W_LAT + 4 * HG_WIDTH])
        hq_ref[rows, :] = hh[:, :HG_WIDTH].astype(BF16)
        hf_ref[rows, :] = hh[:, HG_WIDTH:2 * HG_WIDTH]
        hi_ref[rows, :] = hh[:, 2 * HG_WIDTH:3 * HG_WIDTH].astype(BF16)
        hg_ref[rows, :] = hh[:, 3 * HG_WIDTH:].astype(BF16)
    for n, rows in enumerate(halves):
        gates_ref[rows, :] = jax.nn.sigmoid(
            _dot(h[n], wall_ref[:, W_LAT + 4 * HG_WIDTH:])).astype(BF16)


W_ALL_COLS = W_LAT + 4 * HG_WIDTH + 2 * D_MODEL
W_PREP_COLS = W_ALL_COLS // 2


def _input_weight_kernel(wt_ref, o_ref):
    c0 = Q_LORA_RANK + KV_LORA_RANK
    j = pl.program_id(0)
    n = W_PREP_COLS
    lat_full = c0 // n
    lat_rest = c0 - lat_full * n

    @pl.when(j != lat_full)
    def _():
        o_ref[...] = wt_ref[...].T.astype(BF16)

    @pl.when(j == lat_full)
    def _():
        d = wt_ref.shape[1]
        z_n = jnp.zeros((HEAD_HALF_NOPE, d), F32)
        z_p = jnp.zeros((ROPE_HALF, d), F32)
        parts = [wt_ref[:lat_rest, :], z_n, wt_ref[lat_rest:lat_rest + ROPE_HALF, :], z_p,
                 z_n, wt_ref[lat_rest + ROPE_HALF:lat_rest + QK_ROPE_DIM, :], z_p]
        tail = n - lat_rest - HEAD_PAD
        if tail:
            parts.append(wt_ref[lat_rest + QK_ROPE_DIM:lat_rest + QK_ROPE_DIM + tail, :])
        o_ref[...] = jnp.concatenate(parts, axis=0).T.astype(BF16)


def _input_weight_layout(wt):
    cols, rows = wt.shape
    out_cols = cols - QK_ROPE_DIM + HEAD_PAD
    n = W_PREP_COLS
    lat_full = (Q_LORA_RANK + KV_LORA_RANK) // n
    pad = HEAD_PAD - QK_ROPE_DIM

    def in_rows(j):
        return (pl.multiple_of(jnp.where(j <= lat_full, j * n, j * n - pad), QK_ROPE_DIM), 0)

    return pl.pallas_call(
        _input_weight_kernel,
        grid=(out_cols // n,),
        in_specs=[pl.BlockSpec((pl.Element(n), pl.Element(rows)), in_rows)],
        out_specs=pl.BlockSpec((rows, n), lambda j: (0, j)),
        out_shape=jax.ShapeDtypeStruct((rows, out_cols), BF16),
        compiler_params=pltpu.CompilerParams(
            dimension_semantics=("arbitrary",), vmem_limit_bytes=VMEM_LIMIT),
        name="input_weight_layout",
    )(wt)


def _side_cast_spec(rows, cols, steps):
    for share in (1, 2):
        blk, rem = divmod(rows * share, steps)
        if rem == 0 and blk % BF16_ROWS == 0:
            return pl.BlockSpec((blk, cols), lambda i: (i // share, 0))
    raise ValueError(f"no row block for a ({rows},{cols}) weight over {steps} steps")


def _input_stage(x2d, pos3d, b, s, gmix, wall, gqa, wuq, gkva, wkv, gq, gk, side_weights):
    t = x2d.shape[0]
    tm = TOK_TILE
    nt = s // tm
    steps = t // tm

    def tok(width):
        return pl.BlockSpec((tm, width), lambda i: (i, 0))

    def head_spec():
        return pl.BlockSpec((1, MLA_HEADS, tm, HEAD_PAD), lambda i: (i // nt, 0, i % nt, 0))

    head_shape = jax.ShapeDtypeStruct((b, MLA_HEADS, s, HEAD_PAD), BF16)
    consts = (gmix, wall, gqa, wuq, gkva, wkv, gq, gk)
    side_specs = [_side_cast_spec(w.shape[0], w.shape[1], steps) for w in side_weights]
    outs = pl.pallas_call(
        _input_kernel,
        grid=(steps,),
        in_specs=[tok(D_MODEL), pl.BlockSpec((1, 1, tm), lambda i: (i, 0, 0))]
        + [_const_spec(c.shape) for c in consts] + side_specs,
        out_specs=[head_spec(), head_spec(), head_spec(),
                   tok(HG_WIDTH), tok(HG_WIDTH), tok(HG_WIDTH), tok(HG_WIDTH), tok(2 * D_MODEL)]
        + side_specs,
        out_shape=[head_shape, head_shape, head_shape,
                   jax.ShapeDtypeStruct((t, HG_WIDTH), BF16),
                   jax.ShapeDtypeStruct((t, HG_WIDTH), F32),
                   jax.ShapeDtypeStruct((t, HG_WIDTH), BF16),
                   jax.ShapeDtypeStruct((t, HG_WIDTH), BF16),
                   jax.ShapeDtypeStruct((t, 2 * D_MODEL), BF16)]
        + [jax.ShapeDtypeStruct(w.shape, BF16) for w in side_weights],
        compiler_params=pltpu.CompilerParams(
            dimension_semantics=("arbitrary",), vmem_limit_bytes=VMEM_LIMIT),
        name="input_stage",
    )(x2d, pos3d, *consts, *side_weights)
    return outs[:8], outs[8:]


def _attention_kernel(q_ref, k_ref, v_ref, o_ref, m_ref, acc_ref, s_ref):
    qi = pl.program_id(2)
    tq = ATT_TILE
    half = tq // 2
    m_ref[...] = jnp.full(m_ref.shape, -jnp.inf, F32)
    acc_ref[...] = jnp.zeros(acc_ref.shape, F32)

    all_rows = slice(0, tq)

    def scores(j, hd, rows, n_keys):
        start = pl.multiple_of(j * tq, tq)
        s_ref[hd, rows, :n_keys] = _dot_nt(q_ref[0, hd, rows, :],
                                           k_ref[0, hd, pl.ds(start, n_keys), :])

    def accumulate(j, hd, rows, n_keys, visible):
        start = pl.multiple_of(j * tq, tq)
        sc = s_ref[hd, rows, :n_keys]
        if visible is not None:
            sc = jnp.where(visible, sc, -jnp.inf)
        m_old = m_ref[hd, rows, :]
        m_new = jnp.maximum(m_old, jnp.max(sc, axis=-1, keepdims=True))
        p = jnp.exp2(sc - jnp.concatenate([m_new] * (n_keys // LANES), axis=1)).astype(BF16)
        acc_ref[hd, rows, :] = (jnp.exp2(m_old - m_new) * acc_ref[hd, rows, :]
                                + _dot(p, v_ref[0, hd, pl.ds(start, n_keys), :]))
        m_ref[hd, rows, :] = m_new

    def block(j, rows, n_keys, visible):
        for hd in range(ATT_HEADS):
            scores(j, hd, rows, n_keys)
        for hd in range(ATT_HEADS):
            accumulate(j, hd, rows, n_keys, visible)

    def body(t, carry):
        for hd in range(ATT_HEADS):
            scores(2 * t, hd, all_rows, tq)
        for hd in range(ATT_HEADS):
            accumulate(2 * t, hd, all_rows, tq, None)
            scores(2 * t + 1, hd, all_rows, tq)
        for hd in range(ATT_HEADS):
            accumulate(2 * t + 1, hd, all_rows, tq, None)
        return carry

    lax.fori_loop(0, qi // 2, body, 0)

    @pl.when(qi % 2 == 1)
    def _():
        block(qi - 1, slice(0, tq), tq, None)

    def lower(n_keys, shift):
        r_i = lax.broadcasted_iota(jnp.int32, (half, n_keys), 0)
        c_i = lax.broadcasted_iota(jnp.int32, (half, n_keys), 1)
        return c_i <= r_i + shift

    block(qi, slice(0, half), half, lower(half, 0))
    block(qi, slice(half, tq), tq, lower(tq, half))

    outs = []
    for hd in range(ATT_HEADS):
        acc = acc_ref[hd]
        outs.append(acc[:, :V_HEAD_DIM] / acc[:, DENOM_LANE:DENOM_LANE + 1])
    o_ref[0] = jnp.concatenate(outs, axis=-1).astype(BF16)


def _attention_stage(q, k, v):
    b, nh, s, _ = q.shape
    tq = ATT_TILE
    g = ATT_HEADS
    return pl.pallas_call(
        _attention_kernel,
        grid=(b, nh // g, s // tq),
        in_specs=[pl.BlockSpec((1, g, tq, HEAD_PAD), lambda bi, hp, i: (bi, hp, i, 0)),
                  pl.BlockSpec((1, g, s, HEAD_PAD), lambda bi, hp, i: (bi, hp, 0, 0)),
                  pl.BlockSpec((1, g, s, HEAD_PAD), lambda bi, hp, i: (bi, hp, 0, 0))],
        out_specs=pl.BlockSpec((1, tq, g * V_HEAD_DIM), lambda bi, hp, i: (bi, i, hp)),
        out_shape=jax.ShapeDtypeStruct((b, s, BRANCH_WIDTH), BF16),
        scratch_shapes=[pltpu.VMEM((g, tq, LANES), F32), pltpu.VMEM((g, tq, HEAD_PAD), F32),
                        pltpu.VMEM((g, tq, tq), F32)],
        compiler_params=pltpu.CompilerParams(
            dimension_semantics=("arbitrary", "arbitrary", "arbitrary"),
            vmem_limit_bytes=VMEM_LIMIT),
        name="causal_attention",
    )(q, k, v)


def _hgrn_span_matrix():
    c = HG_CHUNK
    x = np.arange(c)[:, None]
    y = np.arange(c)[None, :]
    mats = [(y <= x)]
    for b in HG_LEVELS:
        r = (x // (2 * b)) * (2 * b) + b
        mats.append((y > np.minimum(x, r)) & (y <= np.maximum(x, r)))
    m = np.concatenate(mats, axis=0).astype(np.float32)
    return np.concatenate([m, m], axis=1)


def _block_diag_rows(x):
    a, b = x[:, :HG_DIM], x[:, HG_DIM:]
    zero = jnp.zeros_like(a)
    return jnp.concatenate([jnp.concatenate([a, zero], axis=1),
                            jnp.concatenate([zero, b], axis=1)], axis=0)


def _hgrn_chunk_local(q, hf, v_bf, lb, span, right_rows, pair_masks):
    c = HG_CHUNK
    f = lb + (1.0 - lb) * jax.nn.sigmoid(hf)
    lf = jnp.log2(f)
    kk = 1.0 - f
    lf_hi = lf.astype(BF16)
    lf_lo = (lf - lf_hi.astype(F32)).astype(BF16)
    spans = _dot(span, jnp.concatenate([lf_hi, lf_lo], axis=0))
    cum = spans[:c]
    last = cum[c - 1:c, :]
    q_dec = (q * jnp.exp2(cum)).astype(BF16)
    k_dec = (kk * jnp.exp2(last - cum)).astype(BF16)
    q_bf = q.astype(BF16)
    k_bf = kk.astype(BF16)
    zs = [(jnp.where(right_rows[lvl], q, kk)
           * jnp.exp2(spans[(lvl + 1) * c:(lvl + 2) * c])).astype(BF16)
          for lvl in range(len(HG_LEVELS))]

    o_intra, incr = [], []
    for pair in range(HG_HEADS // 2):
        psl = slice(2 * pair * HG_DIM, (2 * pair + 2) * HG_DIM)
        att = jnp.where(pair_masks[0], _dot_nt(q_bf[:, psl], _block_diag_rows(k_bf[:, psl])), 0.0)
        for lvl in range(len(HG_LEVELS)):
            z = zs[lvl][:, psl]
            att = jnp.where(pair_masks[lvl + 1], _dot_nt(z, _block_diag_rows(z)), att)
        v_pair = v_bf[:, psl]
        o_intra.append(_dot(att.astype(BF16), _block_diag_rows(v_pair)))
        for hd in (2 * pair, 2 * pair + 1):
            sl = slice(hd * HG_DIM, (hd + 1) * HG_DIM)
            incr.append(_dot(v_bf[:, sl].astype(F32).T.astype(BF16), k_dec[:, sl]))
    return o_intra, q_dec, incr, jnp.exp2(last)


def _hgrn_kernel(hq_ref, hf_ref, hi_ref, hg_ref, lb_ref, gout_ref, span_ref, o_ref, state_ref):
    @pl.when(pl.program_id(1) == 0)
    def _():
        state_ref[...] = jnp.zeros(state_ref.shape, F32)

    c = HG_CHUNK
    t_i = lax.broadcasted_iota(jnp.int32, (c, 2 * c), 0)
    s_i = lax.broadcasted_iota(jnp.int32, (c, 2 * c), 1) % c
    row = lax.broadcasted_iota(jnp.int32, (c, HG_WIDTH), 0)
    pair_masks = [t_i == s_i] + [((t_i // b) ^ (s_i // b) == 1) & (s_i < t_i) for b in HG_LEVELS]
    right_rows = [(row // b) % 2 == 1 for b in HG_LEVELS]
    span = span_ref[...]
    lb = lb_ref[...]
    n_chunks = HG_BLOCK // c

    local = []
    for ci in range(n_chunks):
        rows = slice(ci * c, (ci + 1) * c)
        local.append(_hgrn_chunk_local(hq_ref[0, rows, :].astype(F32), hf_ref[0, rows, :],
                                       hi_ref[0, rows, :], lb, span, right_rows, pair_masks))

    gout = gout_ref[...]
    for pair in range(HG_HEADS // 2):
        hds = (2 * pair, 2 * pair + 1)
        psl = slice(2 * pair * HG_DIM, (2 * pair + 2) * HG_DIM)
        states = [state_ref[hd] for hd in hds]
        for ci in range(n_chunks):
            rows = slice(ci * c, (ci + 1) * c)
            o_intra, q_dec, incr, decay = local[ci]
            both = jnp.concatenate([st.astype(BF16) for st in states], axis=1)
            o_pair = o_intra[pair] + _dot_nt(q_dec[:, psl], _block_diag_rows(both))
            for n, hd in enumerate(hds):
                sl = slice(hd * HG_DIM, (hd + 1) * HG_DIM)
                states[n] = states[n] * decay[:, sl] + incr[hd]
                o = o_pair[:, n * HG_DIM:(n + 1) * HG_DIM]
                gate = hg_ref[0, rows, sl].astype(F32)
                o_ref[0, rows, sl] = (_rms(o, gout) * (gate * jax.nn.sigmoid(gate))).astype(BF16)
        for n, hd in enumerate(hds):
            state_ref[hd] = states[n]


def _hgrn_stage(hq, hf, hi, hg, lb, gout):
    span = jnp.asarray(_hgrn_span_matrix(), BF16)
    b, s, _ = hq.shape
    tb = HG_BLOCK
    tok = pl.BlockSpec((1, tb, HG_WIDTH), lambda bi, i: (bi, i, 0))
    return pl.pallas_call(
        _hgrn_kernel,
        grid=(b, s // tb),
        in_specs=[tok, tok, tok, tok, _const_spec(lb.shape), _const_spec(gout.shape),
                  _const_spec(span.shape)],
        out_specs=tok,
        out_shape=jax.ShapeDtypeStruct((b, s, HG_WIDTH), BF16),
        scratch_shapes=[pltpu.VMEM((HG_HEADS, HG_DIM, HG_DIM), F32)],
        compiler_params=pltpu.CompilerParams(
            dimension_semantics=("arbitrary", "arbitrary"), vmem_limit_bytes=VMEM_LIMIT),
        name="hgrn2_recurrence",
    )(hq, hf, hi, hg, lb, gout, span)


def _output_kernel(x_ref, attn_ref, rec_ref, gates_ref, p_ref,
                   wb_ref, wout_ref, gffn_ref, wgate_ref, wup_ref, wdown_ref,
                   gpg_ref, wpg_ref, wpp_ref, gpost_ref, o_ref):
    halves = _row_parts(x_ref.shape[0])
    merged = []
    for rows in halves:
        y0 = _dot(attn_ref[rows, :], wb_ref[:BRANCH_WIDTH, :])
        y1 = _dot(rec_ref[rows, :], wb_ref[BRANCH_WIDTH:, :])
        g0 = gates_ref[rows, :D_MODEL].astype(F32)
        g1 = gates_ref[rows, D_MODEL:].astype(F32)
        merged.append((g0 * y0 + g1 * y1).astype(BF16))
    x1 = [x_ref[rows, :] + _dot(m, wout_ref[...]) for rows, m in zip(halves, merged)]
    e = [_rms(_dot(p_ref[rows, :].astype(BF16), wpp_ref[...]), gpost_ref[...]) for rows in halves]
    h2 = [_rms(v, gffn_ref[...]).astype(BF16) for v in x1]
    x2 = list(x1)
    lo = 0
    for width in FFN_SPLITS:
        for n in range(len(halves)):
            a = _dot(h2[n], wgate_ref[:, lo:lo + width])
            u = _dot(h2[n], wup_ref[:, lo:lo + width])
            z = (a * jax.nn.sigmoid(a) * u).astype(BF16)
            x2[n] = x2[n] + _dot(z, wdown_ref[lo:lo + width, :])
        lo += width
    for n, rows in enumerate(halves):
        g = jax.nn.sigmoid(_dot(_rms(x2[n], gpg_ref[...]).astype(BF16), wpg_ref[...]))
        o_ref[rows, :] = x2[n] + g * e[n]


def _output_stage(x2d, attn, rec, gates, p2d, wb, wout, gffn, wgate, wup, wdown,
                  gpg, wpg, wpp, gpost):
    t = x2d.shape[0]
    tm = TOK_TILE

    def tok(width):
        return pl.BlockSpec((tm, width), lambda i: (i, 0))

    consts = (wb, wout, gffn, wgate, wup, wdown, gpg, wpg, wpp, gpost)
    return pl.pallas_call(
        _output_kernel,
        grid=(t // tm,),
        in_specs=[tok(D_MODEL), tok(BRANCH_WIDTH), tok(HG_WIDTH), tok(2 * D_MODEL), tok(PLE_DIM)]
        + [_const_spec(c.shape) for c in consts],
        out_specs=tok(D_MODEL),
        out_shape=jax.ShapeDtypeStruct((t, D_MODEL), F32),
        compiler_params=pltpu.CompilerParams(
            dimension_semantics=("arbitrary",), vmem_limit_bytes=VMEM_LIMIT),
        name="output_stage",
    )(x2d, attn, rec, gates, p2d, *consts)


def _pad_heads(w, heads, width):
    rows = w.shape[0]
    w = w.reshape(rows, heads, width)
    w = jnp.pad(w, ((0, 0), (0, 0), (0, HEAD_PAD - width)))
    return w.reshape(rows, heads * HEAD_PAD)


def _qk_head_layout(nope, rope):
    like = nope if nope is not None else rope

    def z(width):
        return jnp.zeros(like.shape[:-1] + (width,), like.dtype)

    n0 = nope[..., :HEAD_HALF_NOPE] if nope is not None else z(HEAD_HALF_NOPE)
    n1 = nope[..., HEAD_HALF_NOPE:] if nope is not None else z(HEAD_HALF_NOPE)
    x1 = rope[..., :ROPE_HALF] if rope is not None else z(ROPE_HALF)
    x2 = rope[..., ROPE_HALF:] if rope is not None else z(ROPE_HALF)
    return jnp.concatenate([n0, x1, z(ROPE_HALF), n1, x2, z(ROPE_HALF)], axis=-1)


def _row(v):
    return v.reshape(1, -1).astype(F32)


def kernel(x, p, positions, mix_norm_g, w_in, q_a_norm_g, w_uq, kv_a_norm_g, w_ukv, q_norm_g, k_norm_g, hg_lb_logits, hg_out_norm_g, w_branch, w_out, ffn_norm_g, w_ffn_gate, w_ffn_up, w_ffn_down, ple_gate_norm_g, w_ple_gate, w_ple_proj, ple_post_norm_g):
    b, s, d = x.shape
    depth = w_in.shape[0]
    lower_bounds = jnp.cumsum(jax.nn.softmax(hg_lb_logits.astype(F32), axis=0), axis=0)
    pos3d = positions.reshape(b * s // TOK_TILE, 1, TOK_TILE)
    x2d = x.reshape(b * s, d)
    q_scale = QK_HEAD_DIM ** -0.5 * math.log2(math.e)

    for layer in range(depth):
        wi = w_in[layer]
        wall = _input_weight_layout(wi.T)
        wq3 = w_uq[layer].reshape(Q_LORA_RANK, MLA_HEADS, QK_HEAD_DIM)
        wuq = _qk_head_layout(wq3[..., :QK_NOPE_DIM], wq3[..., QK_NOPE_DIM:])
        wuq = wuq.reshape(Q_LORA_RANK, MLA_HEADS * HEAD_PAD).astype(BF16)
        wukv = w_ukv[layer].reshape(KV_LORA_RANK, MLA_HEADS, QK_NOPE_DIM + V_HEAD_DIM)
        wk = _qk_head_layout(wukv[..., :QK_NOPE_DIM], None).reshape(KV_LORA_RANK, -1)
        wv = _pad_heads(wukv[:, :, QK_NOPE_DIM:].reshape(KV_LORA_RANK, -1), MLA_HEADS, V_HEAD_DIM)
        wkv = jnp.concatenate([wk, wv], axis=1).astype(BF16)
        gq = q_norm_g[layer].astype(F32) * q_scale
        gk = k_norm_g[layer].astype(F32)

        side = (w_branch[layer].reshape(2 * BRANCH_WIDTH, d), w_out[layer], w_ffn_gate[layer],
                w_ffn_up[layer], w_ffn_down[layer], w_ple_gate[layer], w_ple_proj[layer])
        (q, k, v, hq, hf, hi, hg, gates), side_bf = _input_stage(
            x2d, pos3d, b, s, _row(mix_norm_g[layer]), wall, _row(q_a_norm_g[layer]), wuq,
            _row(kv_a_norm_g[layer]), wkv,
            _row(_qk_head_layout(gq[:QK_NOPE_DIM], gq[QK_NOPE_DIM:])),
            _row(_qk_head_layout(gk[:QK_NOPE_DIM], gk[QK_NOPE_DIM:])), side)
        wb, wout, wgate, wup, wdown, wpg, wpp = side_bf

        attn = _attention_stage(q, k, v)
        rec = _hgrn_stage(hq.reshape(b, s, HG_WIDTH), hf.reshape(b, s, HG_WIDTH),
                          hi.reshape(b, s, HG_WIDTH), hg.reshape(b, s, HG_WIDTH),
                          _row(lower_bounds[layer]), _row(hg_out_norm_g[layer]))

        x2d = _output_stage(
            x2d, attn.reshape(b * s, BRANCH_WIDTH), rec.reshape(b * s, HG_WIDTH), gates,
            p[layer].reshape(b * s, PLE_DIM),
            wb, wout, _row(ffn_norm_g[layer]),
            wgate, wup, wdown, _row(ple_gate_norm_g[layer]), wpg, wpp,
            _row(ple_post_norm_g[layer]))
    return x2d.reshape(b, s, d)
```

```python
import math

import numpy as np

import jax
import jax.numpy as jnp
from jax import lax
from jax.experimental import pallas as pl
from jax.experimental.pallas import tpu as pltpu

D_MODEL = 1024
MLA_HEADS = 8
QK_NOPE_DIM = 64
QK_ROPE_DIM = 32
ROPE_HALF = QK_ROPE_DIM // 2
QK_HEAD_DIM = QK_NOPE_DIM + QK_ROPE_DIM
V_HEAD_DIM = 64
Q_LORA_RANK = 384
KV_LORA_RANK = 256
ROPE_BASE = 10000.0
HG_HEADS = 4
HG_DIM = 128
HG_WIDTH = HG_HEADS * HG_DIM
BRANCH_WIDTH = MLA_HEADS * V_HEAD_DIM
FFN_HIDDEN = 2816
PLE_DIM = 256
EPS = 1e-6

LANES = 128
BF16_ROWS = 16
HEAD_PAD = LANES
HEAD_HALF_NOPE = QK_NOPE_DIM // 2
W_LAT = Q_LORA_RANK + KV_LORA_RANK + HEAD_PAD

VMEM_LIMIT = 60 * 1024 * 1024

TOK_TILE = 512
ROW_PARTS = 2
ATT_TILE = 512
ATT_HEADS = 8
HG_BLOCK = 1024
HG_CHUNK = 64
HG_LEVELS = tuple(HG_CHUNK >> (i + 1) for i in range(HG_CHUNK.bit_length() - 1))
FFN_SPLITS = (1024, 1024, 768)

BF16 = jnp.bfloat16
F32 = jnp.float32


def _const_spec(shape):
    nd = len(shape)
    return pl.BlockSpec(shape, lambda *_: (0,) * nd, pipeline_mode=pl.Buffered(1))


def _rms(v, gain):
    ms = jnp.sum(v * v, axis=-1, keepdims=True) * (1.0 / v.shape[-1])
    return v * lax.rsqrt(ms + EPS) * gain


def _dot(a, b):
    return jnp.dot(a, b, preferred_element_type=F32)


def _row_parts(n_rows):
    step = n_rows // ROW_PARTS
    return tuple(slice(i * step, (i + 1) * step) for i in range(ROW_PARTS))


def _dot_nt(a, b):
    return lax.dot_general(a, b, (((1,), (1,)), ((), ())), preferred_element_type=F32)


def _rope_tables(pos_row):
    tm = pos_row.shape[1]
    fidx = lax.broadcasted_iota(jnp.int32, (ROPE_HALF, 1), 0).astype(F32)
    inv_freq = jnp.exp(fidx * (-math.log(ROPE_BASE) * 2.0 / QK_ROPE_DIM))
    ang = pos_row.astype(F32) * inv_freq
    cos = jnp.cos(ang)
    sin = jnp.sin(ang)
    ones = jnp.ones((HEAD_HALF_NOPE, tm), F32)
    zeros_n = jnp.zeros((HEAD_HALF_NOPE, tm), F32)
    zeros_p = jnp.zeros((ROPE_HALF, tm), F32)
    cos_t = jnp.concatenate([ones, cos, zeros_p, ones, cos, zeros_p], axis=0)
    sin_t = jnp.concatenate([zeros_n, -sin, zeros_p, zeros_n, sin, zeros_p], axis=0)
    return cos_t.T, sin_t.T


def _input_kernel(x_ref, pos_ref, gmix_ref, wall_ref, gqa_ref, wuq_ref, gkva_ref, wkv_ref,
                  gq_ref, gk_ref, *rest):
    n_side = (len(rest) - 8) // 2
    side_in = rest[:n_side]
    q_ref, k_ref, v_ref, hq_ref, hf_ref, hi_ref, hg_ref, gates_ref = rest[n_side:n_side + 8]
    side_out = rest[n_side + 8:]
    for w_ref, wb_ref in zip(side_in, side_out):
        wb_ref[...] = w_ref[...].astype(BF16)

    halves = _row_parts(x_ref.shape[0])
    gq = gq_ref[...]
    gk = gk_ref[...]
    lane = lax.broadcasted_iota(jnp.int32, (1, LANES), 1)
    one_lane = (lane >= V_HEAD_DIM).astype(F32)
    inv_width = 1.0 / QK_HEAD_DIM

    h = [_rms(x_ref[rows, :], gmix_ref[...]).astype(BF16) for rows in halves]
    lat = [_dot(v, wall_ref[:, :W_LAT]) for v in h]
    cq = [_rms(v[:, :Q_LORA_RANK], gqa_ref[...]).astype(BF16) for v in lat]
    ckv = [_rms(v[:, Q_LORA_RANK:Q_LORA_RANK + KV_LORA_RANK], gkva_ref[...]).astype(BF16)
           for v in lat]
    q_all = [_dot(v, wuq_ref[...]) for v in cq]
    kv_all = [_dot(v, wkv_ref[...]) for v in ckv]

    for n, rows in enumerate(halves):
        k_rope = lat[n][:, Q_LORA_RANK + KV_LORA_RANK:]
        cos_tab, sin_tab = _rope_tables(pos_ref[0, :, rows])
        q_cos = cos_tab * gq
        q_sin = sin_tab * pltpu.roll(gq, LANES // 2, 1)
        kr = k_rope * gk
        kr = kr * cos_tab + pltpu.roll(kr, LANES // 2, 1) * sin_tab
        kr_ss = jnp.sum(k_rope * k_rope, axis=-1, keepdims=True)
        for hd in range(MLA_HEADS):
            sl = slice(hd * HEAD_PAD, (hd + 1) * HEAD_PAD)
            t = q_all[n][:, sl]
            r = lax.rsqrt(jnp.sum(t * t, axis=-1, keepdims=True) * inv_width + EPS)
            q_ref[0, hd, rows, :] = ((t * q_cos + pltpu.roll(t, LANES // 2, 1) * q_sin) * r).astype(BF16)
            t = kv_all[n][:, sl]
            r = lax.rsqrt((jnp.sum(t * t, axis=-1, keepdims=True) + kr_ss) * inv_width + EPS)
            k_ref[0, hd, rows, :] = ((t * gk + kr) * r).astype(BF16)
            vsl = slice(MLA_HEADS * HEAD_PAD + hd * HEAD_PAD, MLA_HEADS * HEAD_PAD + (hd + 1) * HEAD_PAD)
            v_ref[0, hd, rows, :] = (kv_all[n][:, vsl] + one_lane).astype(BF16)

    for n, rows in enumerate(halves):
        hh = _dot(h[n], wall_ref[:, W_LAT:W_LAT + 4 * HG_WIDTH])
        hq_ref[rows, :] = hh[:, :HG_WIDTH].astype(BF16)
        hf_ref[rows, :] = hh[:, HG_WIDTH:2 * HG_WIDTH]
        hi_ref[rows, :] = hh[:, 2 * HG_WIDTH:3 * HG_WIDTH].astype(BF16)
        hg_ref[rows, :] = hh[:, 3 * HG_WIDTH:].astype(BF16)
    for n, rows in enumerate(halves):
        gates_ref[rows, :] = jax.nn.sigmoid(
            _dot(h[n], wall_ref[:, W_LAT + 4 * HG_WIDTH:])).astype(BF16)


W_ALL_COLS = W_LAT + 4 * HG_WIDTH + 2 * D_MODEL
W_PREP_COLS = W_ALL_COLS // 2


def _input_weight_kernel(wt_ref, o_ref):
    c0 = Q_LORA_RANK + KV_LORA_RANK
    j = pl.program_id(0)
    n = W_PREP_COLS
    lat_full = c0 // n
    lat_rest = c0 - lat_full * n

    @pl.when(j != lat_full)
    def _():
        o_ref[...] = wt_ref[...].T.astype(BF16)

    @pl.when(j == lat_full)
    def _():
        d = wt_ref.shape[1]
        z_n = jnp.zeros((HEAD_HALF_NOPE, d), F32)
        z_p = jnp.zeros((ROPE_HALF, d), F32)
        parts = [wt_ref[:lat_rest, :], z_n, wt_ref[lat_rest:lat_rest + ROPE_HALF, :], z_p,
                 z_n, wt_ref[lat_rest + ROPE_HALF:lat_rest + QK_ROPE_DIM, :], z_p]
        tail = n - lat_rest - HEAD_PAD
        if tail:
            parts.append(wt_ref[lat_rest + QK_ROPE_DIM:lat_rest + QK_ROPE_DIM + tail, :])
        o_ref[...] = jnp.concatenate(parts, axis=0).T.astype(BF16)


def _input_weight_layout(wt):
    cols, rows = wt.shape
    out_cols = cols - QK_ROPE_DIM + HEAD_PAD
    n = W_PREP_COLS
    lat_full = (Q_LORA_RANK + KV_LORA_RANK) // n
    pad = HEAD_PAD - QK_ROPE_DIM

    def in_rows(j):
        return (pl.multiple_of(jnp.where(j <= lat_full, j * n, j * n - pad), QK_ROPE_DIM), 0)

    return pl.pallas_call(
        _input_weight_kernel,
        grid=(out_cols // n,),
        in_specs=[pl.BlockSpec((pl.Element(n), pl.Element(rows)), in_rows)],
        out_specs=pl.BlockSpec((rows, n), lambda j: (0, j)),
        out_shape=jax.ShapeDtypeStruct((rows, out_cols), BF16),
        compiler_params=pltpu.CompilerParams(
            dimension_semantics=("arbitrary",), vmem_limit_bytes=VMEM_LIMIT),
        name="input_weight_layout",
    )(wt)


def _side_cast_spec(rows, cols, steps):
    for share in (1, 2):
        blk, rem = divmod(rows * share, steps)
        if rem == 0 and blk % BF16_ROWS == 0:
            return pl.BlockSpec((blk, cols), lambda i: (i // share, 0))
    raise ValueError(f"no row block for a ({rows},{cols}) weight over {steps} steps")


def _input_stage(x2d, pos3d, b, s, gmix, wall, gqa, wuq, gkva, wkv, gq, gk, side_weights):
    t = x2d.shape[0]
    tm = TOK_TILE
    nt = s // tm
    steps = t // tm

    def tok(width):
        return pl.BlockSpec((tm, width), lambda i: (i, 0))

    def head_spec():
        return pl.BlockSpec((1, MLA_HEADS, tm, HEAD_PAD), lambda i: (i // nt, 0, i % nt, 0))

    head_shape = jax.ShapeDtypeStruct((b, MLA_HEADS, s, HEAD_PAD), BF16)
    consts = (gmix, wall, gqa, wuq, gkva, wkv, gq, gk)
    side_specs = [_side_cast_spec(w.shape[0], w.shape[1], steps) for w in side_weights]
    outs = pl.pallas_call(
        _input_kernel,
        grid=(steps,),
        in_specs=[tok(D_MODEL), pl.BlockSpec((1, 1, tm), lambda i: (i, 0, 0))]
        + [_const_spec(c.shape) for c in consts] + side_specs,
        out_specs=[head_spec(), head_spec(), head_spec(),
                   tok(HG_WIDTH), tok(HG_WIDTH), tok(HG_WIDTH), tok(HG_WIDTH), tok(2 * D_MODEL)]
        + side_specs,
        out_shape=[head_shape, head_shape, head_shape,
                   jax.ShapeDtypeStruct((t, HG_WIDTH), BF16),
                   jax.ShapeDtypeStruct((t, HG_WIDTH), F32),
                   jax.ShapeDtypeStruct((t, HG_WIDTH), BF16),
                   jax.ShapeDtypeStruct((t, HG_WIDTH), BF16),
                   jax.ShapeDtypeStruct((t, 2 * D_MODEL), BF16)]
        + [jax.ShapeDtypeStruct(w.shape, BF16) for w in side_weights],
        compiler_params=pltpu.CompilerParams(
            dimension_semantics=("arbitrary",), vmem_limit_bytes=VMEM_LIMIT),
        name="input_stage",
    )(x2d, pos3d, *consts, *side_weights)
    return outs[:8], outs[8:]


def _attention_kernel(q_ref, k_ref, v_ref, o_ref, m_ref, acc_ref, s_ref):
    qi = pl.program_id(2)
    tq = ATT_TILE
    half = tq // 2
    m_ref[...] = jnp.full(m_ref.shape, -jnp.inf, F32)
    acc_ref[...] = jnp.zeros(acc_ref.shape, F32)

    all_rows = slice(0, tq)

    def scores(j, hd, rows, n_keys):
        start = pl.multiple_of(j * tq, tq)
        s_ref[hd, rows, :n_keys] = _dot_nt(q_ref[0, hd, rows, :],
                                           k_ref[0, hd, pl.ds(start, n_keys), :])

    def accumulate(j, hd, rows, n_keys, visible):
        start = pl.multiple_of(j * tq, tq)
        sc = s_ref[hd, rows, :n_keys]
        if visible is not None:
            sc = jnp.where(visible, sc, -jnp.inf)
        m_old = m_ref[hd, rows, :]
        m_new = jnp.maximum(m_old, jnp.max(sc, axis=-1, keepdims=True))
        p = jnp.exp2(sc - jnp.concatenate([m_new] * (n_keys // LANES), axis=1)).astype(BF16)
        acc_ref[hd, rows, :] = (jnp.exp2(m_old - m_new) * acc_ref[hd, rows, :]
                                + _dot(p, v_ref[0, hd, pl.ds(start, n_keys), :]))
        m_ref[hd, rows, :] = m_new

    def block(j, rows, n_keys, visible):
        for hd in range(ATT_HEADS):
            scores(j, hd, rows, n_keys)
        for hd in range(ATT_HEADS):
            accumulate(j, hd, rows, n_keys, visible)

    def body(t, carry):
        for hd in range(ATT_HEADS):
            scores(2 * t, hd, all_rows, tq)
        for hd in range(ATT_HEADS):
            accumulate(2 * t, hd, all_rows, tq, None)
            scores(2 * t + 1, hd, all_rows, tq)
        for hd in range(ATT_HEADS):
            accumulate(2 * t + 1, hd, all_rows, tq, None)
        return carry

    lax.fori_loop(0, qi // 2, body, 0)

    @pl.when(qi % 2 == 1)
    def _():
        block(qi - 1, slice(0, tq), tq, None)

    def lower(n_keys, shift):
        r_i = lax.broadcasted_iota(jnp.int32, (half, n_keys), 0)
        c_i = lax.broadcasted_iota(jnp.int32, (half, n_keys), 1)
        return c_i <= r_i + shift

    block(qi, slice(0, half), half, lower(half, 0))
    block(qi, slice(half, tq), tq, lower(tq, half))

    low = lax.broadcasted_iota(jnp.int32, (tq, LANES), 1) < V_HEAD_DIM
    for pair in range(ATT_HEADS // 2):
        even, odd = acc_ref[2 * pair], acc_ref[2 * pair + 1]
        out = jnp.where(low, even / pltpu.roll(even, V_HEAD_DIM, 1), pltpu.roll(odd, V_HEAD_DIM, 1) / odd)
        o_ref[0, :, pair * LANES:(pair + 1) * LANES] = out.astype(BF16)


def _attention_stage(q, k, v):
    b, nh, s, _ = q.shape
    tq = ATT_TILE
    g = ATT_HEADS
    return pl.pallas_call(
        _attention_kernel,
        grid=(b, nh // g, s // tq),
        in_specs=[pl.BlockSpec((1, g, tq, HEAD_PAD), lambda bi, hp, i: (bi, hp, i, 0)),
                  pl.BlockSpec((1, g, s, HEAD_PAD), lambda bi, hp, i: (bi, hp, 0, 0)),
                  pl.BlockSpec((1, g, s, HEAD_PAD), lambda bi, hp, i: (bi, hp, 0, 0))],
        out_specs=pl.BlockSpec((1, tq, g * V_HEAD_DIM), lambda bi, hp, i: (bi, i, hp)),
        out_shape=jax.ShapeDtypeStruct((b, s, BRANCH_WIDTH), BF16),
        scratch_shapes=[pltpu.VMEM((g, tq, LANES), F32), pltpu.VMEM((g, tq, HEAD_PAD), F32),
                        pltpu.VMEM((g, tq, tq), F32)],
        compiler_params=pltpu.CompilerParams(
            dimension_semantics=("arbitrary", "arbitrary", "arbitrary"),
            vmem_limit_bytes=VMEM_LIMIT),
        name="causal_attention",
    )(q, k, v)


def _hgrn_span_matrix():
    c = HG_CHUNK
    x = np.arange(c)[:, None]
    y = np.arange(c)[None, :]
    mats = [(y <= x)]
    for b in HG_LEVELS:
        r = (x // (2 * b)) * (2 * b) + b
        mats.append((y > np.minimum(x, r)) & (y <= np.maximum(x, r)))
    m = np.concatenate(mats, axis=0).astype(np.float32)
    return np.concatenate([m, m], axis=1)


def _block_diag_rows(x):
    a, b = x[:, :HG_DIM], x[:, HG_DIM:]
    zero = jnp.zeros_like(a)
    return jnp.concatenate([jnp.concatenate([a, zero], axis=1),
                            jnp.concatenate([zero, b], axis=1)], axis=0)


def _hgrn_chunk_local(q, hf, v_bf, lb, span, right_rows, pair_masks):
    c = HG_CHUNK
    f = lb + (1.0 - lb) * jax.nn.sigmoid(hf)
    lf = jnp.log2(f)
    kk = 1.0 - f
    lf_hi = lf.astype(BF16)
    lf_lo = (lf - lf_hi.astype(F32)).astype(BF16)
    spans = _dot(span, jnp.concatenate([lf_hi, lf_lo], axis=0))
    cum = spans[:c]
    last = cum[c - 1:c, :]
    q_dec = (q * jnp.exp2(cum)).astype(BF16)
    k_dec = (kk * jnp.exp2(last - cum)).astype(BF16)
    q_bf = q.astype(BF16)
    k_bf = kk.astype(BF16)
    zs = [(jnp.where(right_rows[lvl], q, kk)
           * jnp.exp2(spans[(lvl + 1) * c:(lvl + 2) * c])).astype(BF16)
          for lvl in range(len(HG_LEVELS))]

    o_intra, incr = [], []
    for pair in range(HG_HEADS // 2):
        psl = slice(2 * pair * HG_DIM, (2 * pair + 2) * HG_DIM)
        att = jnp.where(pair_masks[0], _dot_nt(q_bf[:, psl], _block_diag_rows(k_bf[:, psl])), 0.0)
        for lvl in range(len(HG_LEVELS)):
            z = zs[lvl][:, psl]
            att = jnp.where(pair_masks[lvl + 1], _dot_nt(z, _block_diag_rows(z)), att)
        v_pair = v_bf[:, psl]
        o_intra.append(_dot(att.astype(BF16), _block_diag_rows(v_pair)))
        for hd in (2 * pair, 2 * pair + 1):
            sl = slice(hd * HG_DIM, (hd + 1) * HG_DIM)
            incr.append(_dot(v_bf[:, sl].astype(F32).T.astype(BF16), k_dec[:, sl]))
    return o_intra, q_dec, incr, jnp.exp2(last)


def _hgrn_kernel(hq_ref, hf_ref, hi_ref, hg_ref, lb_ref, gout_ref, span_ref, o_ref, state_ref):
    @pl.when(pl.program_id(1) == 0)
    def _():
        state_ref[...] = jnp.zeros(state_ref.shape, F32)

    c = HG_CHUNK
    t_i = lax.broadcasted_iota(jnp.int32, (c, 2 * c), 0)
    s_i = lax.broadcasted_iota(jnp.int32, (c, 2 * c), 1) % c
    row = lax.broadcasted_iota(jnp.int32, (c, HG_WIDTH), 0)
    pair_masks = [t_i == s_i] + [((t_i // b) ^ (s_i // b) == 1) & (s_i < t_i) for b in HG_LEVELS]
    right_rows = [(row // b) % 2 == 1 for b in HG_LEVELS]
    span = span_ref[...]
    lb = lb_ref[...]
    n_chunks = HG_BLOCK // c

    local = []
    for ci in range(n_chunks):
        rows = slice(ci * c, (ci + 1) * c)
        local.append(_hgrn_chunk_local(hq_ref[0, rows, :].astype(F32), hf_ref[0, rows, :],
                                       hi_ref[0, rows, :], lb, span, right_rows, pair_masks))

    gout = gout_ref[...]
    for pair in range(HG_HEADS // 2):
        hds = (2 * pair, 2 * pair + 1)
        psl = slice(2 * pair * HG_DIM, (2 * pair + 2) * HG_DIM)
        states = [state_ref[hd] for hd in hds]
        for ci in range(n_chunks):
            rows = slice(ci * c, (ci + 1) * c)
            o_intra, q_dec, incr, decay = local[ci]
            both = jnp.concatenate([st.astype(BF16) for st in states], axis=1)
            o_pair = o_intra[pair] + _dot_nt(q_dec[:, psl], _block_diag_rows(both))
            for n, hd in enumerate(hds):
                sl = slice(hd * HG_DIM, (hd + 1) * HG_DIM)
                states[n] = states[n] * decay[:, sl] + incr[hd]
                o = o_pair[:, n * HG_DIM:(n + 1) * HG_DIM]
                gate = hg_ref[0, rows, sl].astype(F32)
                o_ref[0, rows, sl] = (_rms(o, gout) * (gate * jax.nn.sigmoid(gate))).astype(BF16)
        for n, hd in enumerate(hds):
            state_ref[hd] = states[n]


def _hgrn_stage(hq, hf, hi, hg, lb, gout):
    span = jnp.asarray(_hgrn_span_matrix(), BF16)
    b, s, _ = hq.shape
    tb = HG_BLOCK
    tok = pl.BlockSpec((1, tb, HG_WIDTH), lambda bi, i: (bi, i, 0))
    return pl.pallas_call(
        _hgrn_kernel,
        grid=(b, s // tb),
        in_specs=[tok, tok, tok, tok, _const_spec(lb.shape), _const_spec(gout.shape),
                  _const_spec(span.shape)],
        out_specs=tok,
        out_shape=jax.ShapeDtypeStruct((b, s, HG_WIDTH), BF16),
        scratch_shapes=[pltpu.VMEM((HG_HEADS, HG_DIM, HG_DIM), F32)],
        compiler_params=pltpu.CompilerParams(
            dimension_semantics=("arbitrary", "arbitrary"), vmem_limit_bytes=VMEM_LIMIT),
        name="hgrn2_recurrence",
    )(hq, hf, hi, hg, lb, gout, span)


def _output_kernel(x_ref, attn_ref, rec_ref, gates_ref, p_ref,
                   wb_ref, wout_ref, gffn_ref, wgate_ref, wup_ref, wdown_ref,
                   gpg_ref, wpg_ref, wpp_ref, gpost_ref, o_ref):
    halves = _row_parts(x_ref.shape[0])
    merged = []
    for rows in halves:
        y0 = _dot(attn_ref[rows, :], wb_ref[:BRANCH_WIDTH, :])
        y1 = _dot(rec_ref[rows, :], wb_ref[BRANCH_WIDTH:, :])
        g0 = gates_ref[rows, :D_MODEL].astype(F32)
        g1 = gates_ref[rows, D_MODEL:].astype(F32)
        merged.append((g0 * y0 + g1 * y1).astype(BF16))
    x1 = [x_ref[rows, :] + _dot(m, wout_ref[...]) for rows, m in zip(halves, merged)]
    e = [_rms(_dot(p_ref[rows, :].astype(BF16), wpp_ref[...]), gpost_ref[...]) for rows in halves]
    h2 = [_rms(v, gffn_ref[...]).astype(BF16) for v in x1]
    x2 = list(x1)
    lo = 0
    for width in FFN_SPLITS:
        for n in range(len(halves)):
            a = _dot(h2[n], wgate_ref[:, lo:lo + width])
            u = _dot(h2[n], wup_ref[:, lo:lo + width])
            z = (a * jax.nn.sigmoid(a) * u).astype(BF16)
            x2[n] = x2[n] + _dot(z, wdown_ref[lo:lo + width, :])
        lo += width
    for n, rows in enumerate(halves):
        g = jax.nn.sigmoid(_dot(_rms(x2[n], gpg_ref[...]).astype(BF16), wpg_ref[...]))
        o_ref[rows, :] = x2[n] + g * e[n]


def _output_stage(x2d, attn, rec, gates, p2d, wb, wout, gffn, wgate, wup, wdown,
                  gpg, wpg, wpp, gpost):
    t = x2d.shape[0]
    tm = TOK_TILE

    def tok(width):
        return pl.BlockSpec((tm, width), lambda i: (i, 0))

    consts = (wb, wout, gffn, wgate, wup, wdown, gpg, wpg, wpp, gpost)
    return pl.pallas_call(
        _output_kernel,
        grid=(t // tm,),
        in_specs=[tok(D_MODEL), tok(BRANCH_WIDTH), tok(HG_WIDTH), tok(2 * D_MODEL), tok(PLE_DIM)]
        + [_const_spec(c.shape) for c in consts],
        out_specs=tok(D_MODEL),
        out_shape=jax.ShapeDtypeStruct((t, D_MODEL), F32),
        compiler_params=pltpu.CompilerParams(
            dimension_semantics=("arbitrary",), vmem_limit_bytes=VMEM_LIMIT),
        name="output_stage",
    )(x2d, attn, rec, gates, p2d, *consts)


def _pad_heads(w, heads, width):
    rows = w.shape[0]
    w = w.reshape(rows, heads, width)
    w = jnp.pad(w, ((0, 0), (0, 0), (0, HEAD_PAD - width)))
    return w.reshape(rows, heads * HEAD_PAD)


def _qk_head_layout(nope, rope):
    like = nope if nope is not None else rope

    def z(width):
        return jnp.zeros(like.shape[:-1] + (width,), like.dtype)

    n0 = nope[..., :HEAD_HALF_NOPE] if nope is not None else z(HEAD_HALF_NOPE)
    n1 = nope[..., HEAD_HALF_NOPE:] if nope is not None else z(HEAD_HALF_NOPE)
    x1 = rope[..., :ROPE_HALF] if rope is not None else z(ROPE_HALF)
    x2 = rope[..., ROPE_HALF:] if rope is not None else z(ROPE_HALF)
    return jnp.concatenate([n0, x1, z(ROPE_HALF), n1, x2, z(ROPE_HALF)], axis=-1)


def _row(v):
    return v.reshape(1, -1).astype(F32)


def kernel(x, p, positions, mix_norm_g, w_in, q_a_norm_g, w_uq, kv_a_norm_g, w_ukv, q_norm_g, k_norm_g, hg_lb_logits, hg_out_norm_g, w_branch, w_out, ffn_norm_g, w_ffn_gate, w_ffn_up, w_ffn_down, ple_gate_norm_g, w_ple_gate, w_ple_proj, ple_post_norm_g):
    b, s, d = x.shape
    depth = w_in.shape[0]
    lower_bounds = jnp.cumsum(jax.nn.softmax(hg_lb_logits.astype(F32), axis=0), axis=0)
    pos3d = positions.reshape(b * s // TOK_TILE, 1, TOK_TILE)
    x2d = x.reshape(b * s, d)
    q_scale = QK_HEAD_DIM ** -0.5 * math.log2(math.e)

    for layer in range(depth):
        wi = w_in[layer]
        wall = _input_weight_layout(wi.T)
        wq3 = w_uq[layer].reshape(Q_LORA_RANK, MLA_HEADS, QK_HEAD_DIM)
        wuq = _qk_head_layout(wq3[..., :QK_NOPE_DIM], wq3[..., QK_NOPE_DIM:])
        wuq = wuq.reshape(Q_LORA_RANK, MLA_HEADS * HEAD_PAD).astype(BF16)
        wukv = w_ukv[layer].reshape(KV_LORA_RANK, MLA_HEADS, QK_NOPE_DIM + V_HEAD_DIM)
        wk = _qk_head_layout(wukv[..., :QK_NOPE_DIM], None).reshape(KV_LORA_RANK, -1)
        wv = _pad_heads(wukv[:, :, QK_NOPE_DIM:].reshape(KV_LORA_RANK, -1), MLA_HEADS, V_HEAD_DIM)
        wkv = jnp.concatenate([wk, wv], axis=1).astype(BF16)
        gq = q_norm_g[layer].astype(F32) * q_scale
        gk = k_norm_g[layer].astype(F32)

        side = (w_branch[layer].reshape(2 * BRANCH_WIDTH, d), w_out[layer], w_ffn_gate[layer],
                w_ffn_up[layer], w_ffn_down[layer], w_ple_gate[layer], w_ple_proj[layer])
        (q, k, v, hq, hf, hi, hg, gates), side_bf = _input_stage(
            x2d, pos3d, b, s, _row(mix_norm_g[layer]), wall, _row(q_a_norm_g[layer]), wuq,
            _row(kv_a_norm_g[layer]), wkv,
            _row(_qk_head_layout(gq[:QK_NOPE_DIM], gq[QK_NOPE_DIM:])),
            _row(_qk_head_layout(gk[:QK_NOPE_DIM], gk[QK_NOPE_DIM:])), side)
        wb, wout, wgate, wup, wdown, wpg, wpp = side_bf

        attn = _attention_stage(q, k, v)
        rec = _hgrn_stage(hq.reshape(b, s, HG_WIDTH), hf.reshape(b, s, HG_WIDTH),
                          hi.reshape(b, s, HG_WIDTH), hg.reshape(b, s, HG_WIDTH),
                          _row(lower_bounds[layer]), _row(hg_out_norm_g[layer]))

        x2d = _output_stage(
            x2d, attn.reshape(b * s, BRANCH_WIDTH), rec.reshape(b * s, HG_WIDTH), gates,
            p[layer].reshape(b * s, PLE_DIM),
            wb, wout, _row(ffn_norm_g[layer]),
            wgate, wup, wdown, _row(ple_gate_norm_g[layer]), wpg, wpp,
            _row(ple_post_norm_g[layer]))
    return x2d.reshape(b, s, d)
```

```python
import math

import numpy as np

import jax
import jax.numpy as jnp
from jax import lax
from jax.experimental import pallas as pl
from jax.experimental.pallas import tpu as pltpu

D_MODEL = 1024
MLA_HEADS = 8
QK_NOPE_DIM = 64
QK_ROPE_DIM = 32
ROPE_HALF = QK_ROPE_DIM // 2
QK_HEAD_DIM = QK_NOPE_DIM + QK_ROPE_DIM
V_HEAD_DIM = 64
Q_LORA_RANK = 384
KV_LORA_RANK = 256
ROPE_BASE = 10000.0
HG_HEADS = 4
HG_DIM = 128
HG_WIDTH = HG_HEADS * HG_DIM
BRANCH_WIDTH = MLA_HEADS * V_HEAD_DIM
FFN_HIDDEN = 2816
PLE_DIM = 256
EPS = 1e-6

LANES = 128
BF16_ROWS = 16
HEAD_PAD = LANES
HEAD_HALF_NOPE = QK_NOPE_DIM // 2
W_LAT = Q_LORA_RANK + KV_LORA_RANK + HEAD_PAD

VMEM_LIMIT = 60 * 1024 * 1024

TOK_TILE = 512
ROW_PARTS = 2
ATT_TILE = 512
ATT_HEADS = 8
HG_BLOCK = 1024
HG_CHUNK = 64
HG_LEVELS = tuple(HG_CHUNK >> (i + 1) for i in range(HG_CHUNK.bit_length() - 1))
FFN_SPLITS = (1024, 1024, 768)
assert sum(FFN_SPLITS) == FFN_HIDDEN

BF16 = jnp.bfloat16
F32 = jnp.float32


def _const_spec(shape):
    nd = len(shape)
    return pl.BlockSpec(shape, lambda *_: (0,) * nd, pipeline_mode=pl.Buffered(1))


def _rms(v, gain):
    ms = jnp.sum(v * v, axis=-1, keepdims=True) * (1.0 / v.shape[-1])
    return v * lax.rsqrt(ms + EPS) * gain


def _dot(a, b):
    return jnp.dot(a, b, preferred_element_type=F32)


def _row_parts(n_rows):
    step = n_rows // ROW_PARTS
    return tuple(slice(i * step, (i + 1) * step) for i in range(ROW_PARTS))


def _dot_nt(a, b):
    return lax.dot_general(a, b, (((1,), (1,)), ((), ())), preferred_element_type=F32)


def _rope_tables(pos_row):
    tm = pos_row.shape[1]
    fidx = lax.broadcasted_iota(jnp.int32, (ROPE_HALF, 1), 0).astype(F32)
    inv_freq = jnp.exp(fidx * (-math.log(ROPE_BASE) * 2.0 / QK_ROPE_DIM))
    ang = pos_row.astype(F32) * inv_freq
    cos = jnp.cos(ang)
    sin = jnp.sin(ang)
    ones = jnp.ones((HEAD_HALF_NOPE, tm), F32)
    zeros_n = jnp.zeros((HEAD_HALF_NOPE, tm), F32)
    zeros_p = jnp.zeros((ROPE_HALF, tm), F32)
    cos_t = jnp.concatenate([ones, cos, zeros_p, ones, cos, zeros_p], axis=0)
    sin_t = jnp.concatenate([zeros_n, -sin, zeros_p, zeros_n, sin, zeros_p], axis=0)
    return cos_t.T, sin_t.T


def _input_kernel(x_ref, pos_ref, gmix_ref, wall_ref, gqa_ref, wuq_ref, gkva_ref, wkv_ref,
                  gq_ref, gk_ref, *rest):
    n_side = (len(rest) - 8) // 2
    side_in = rest[:n_side]
    q_ref, k_ref, v_ref, hq_ref, hf_ref, hi_ref, hg_ref, gates_ref = rest[n_side:n_side + 8]
    side_out = rest[n_side + 8:]
    for w_ref, wb_ref in zip(side_in, side_out):
        wb_ref[...] = w_ref[...].astype(BF16)

    halves = _row_parts(x_ref.shape[0])
    gq = gq_ref[...]
    gk = gk_ref[...]
    lane = lax.broadcasted_iota(jnp.int32, (1, LANES), 1)
    one_lane = (lane >= V_HEAD_DIM).astype(F32)
    inv_width = 1.0 / QK_HEAD_DIM

    h = [_rms(x_ref[rows, :], gmix_ref[...]).astype(BF16) for rows in halves]
    lat = [_dot(v, wall_ref[:, :W_LAT]) for v in h]
    cq = [_rms(v[:, :Q_LORA_RANK], gqa_ref[...]).astype(BF16) for v in lat]
    ckv = [_rms(v[:, Q_LORA_RANK:Q_LORA_RANK + KV_LORA_RANK], gkva_ref[...]).astype(BF16)
           for v in lat]
    q_all = [_dot(v, wuq_ref[...]) for v in cq]
    kv_all = [_dot(v, wkv_ref[...]) for v in ckv]

    for n, rows in enumerate(halves):
        k_rope = lat[n][:, Q_LORA_RANK + KV_LORA_RANK:]
        cos_tab, sin_tab = _rope_tables(pos_ref[0, :, rows])
        q_cos = cos_tab * gq
        q_sin = sin_tab * pltpu.roll(gq, LANES // 2, 1)
        kr = k_rope * gk
        kr = kr * cos_tab + pltpu.roll(kr, LANES // 2, 1) * sin_tab
        kr_ss = jnp.sum(k_rope * k_rope, axis=-1, keepdims=True)
        for hd in range(MLA_HEADS):
            sl = slice(hd * HEAD_PAD, (hd + 1) * HEAD_PAD)
            t = q_all[n][:, sl]
            r = lax.rsqrt(jnp.sum(t * t, axis=-1, keepdims=True) * inv_width + EPS)
            q_ref[0, hd, rows, :] = ((t * q_cos + pltpu.roll(t, LANES // 2, 1) * q_sin) * r).astype(BF16)
            t = kv_all[n][:, sl]
            r = lax.rsqrt((jnp.sum(t * t, axis=-1, keepdims=True) + kr_ss) * inv_width + EPS)
            k_ref[0, hd, rows, :] = ((t * gk + kr) * r).astype(BF16)
            vsl = slice(MLA_HEADS * HEAD_PAD + hd * HEAD_PAD, MLA_HEADS * HEAD_PAD + (hd + 1) * HEAD_PAD)
            v_ref[0, hd, rows, :] = (kv_all[n][:, vsl] + one_lane).astype(BF16)

    for n, rows in enumerate(halves):
        hh = _dot(h[n], wall_ref[:, W_LAT:W_LAT + 4 * HG_WIDTH])
        hq_ref[rows, :] = hh[:, :HG_WIDTH].astype(BF16)
        hf_ref[rows, :] = hh[:, HG_WIDTH:2 * HG_WIDTH]
        hi_ref[rows, :] = hh[:, 2 * HG_WIDTH:3 * HG_WIDTH].astype(BF16)
        hg_ref[rows, :] = hh[:, 3 * HG_WIDTH:].astype(BF16)
    for n, rows in enumerate(halves):
        gates_ref[rows, :] = jax.nn.sigmoid(
            _dot(h[n], wall_ref[:, W_LAT + 4 * HG_WIDTH:])).astype(BF16)


W_ALL_COLS = W_LAT + 4 * HG_WIDTH + 2 * D_MODEL
W_PREP_COLS = W_ALL_COLS // 2


def _input_weight_kernel(wt_ref, o_ref):
    c0 = Q_LORA_RANK + KV_LORA_RANK
    j = pl.program_id(0)
    n = W_PREP_COLS
    lat_full = c0 // n
    lat_rest = c0 - lat_full * n

    @pl.when(j != lat_full)
    def _():
        o_ref[...] = wt_ref[...].T.astype(BF16)

    @pl.when(j == lat_full)
    def _():
        d = wt_ref.shape[1]
        z_n = jnp.zeros((HEAD_HALF_NOPE, d), F32)
        z_p = jnp.zeros((ROPE_HALF, d), F32)
        parts = [wt_ref[:lat_rest, :], z_n, wt_ref[lat_rest:lat_rest + ROPE_HALF, :], z_p,
                 z_n, wt_ref[lat_rest + ROPE_HALF:lat_rest + QK_ROPE_DIM, :], z_p]
        tail = n - lat_rest - HEAD_PAD
        if tail:
            parts.append(wt_ref[lat_rest + QK_ROPE_DIM:lat_rest + QK_ROPE_DIM + tail, :])
        o_ref[...] = jnp.concatenate(parts, axis=0).T.astype(BF16)


def _input_weight_layout(wt):
    cols, rows = wt.shape
    out_cols = cols - QK_ROPE_DIM + HEAD_PAD
    n = W_PREP_COLS
    lat_full = (Q_LORA_RANK + KV_LORA_RANK) // n
    pad = HEAD_PAD - QK_ROPE_DIM

    def in_rows(j):
        return (pl.multiple_of(jnp.where(j <= lat_full, j * n, j * n - pad), QK_ROPE_DIM), 0)

    return pl.pallas_call(
        _input_weight_kernel,
        grid=(out_cols // n,),
        in_specs=[pl.BlockSpec((pl.Element(n), pl.Element(rows)), in_rows)],
        out_specs=pl.BlockSpec((rows, n), lambda j: (0, j)),
        out_shape=jax.ShapeDtypeStruct((rows, out_cols), BF16),
        compiler_params=pltpu.CompilerParams(
            dimension_semantics=("arbitrary",), vmem_limit_bytes=VMEM_LIMIT),
        name="input_weight_layout",
    )(wt)


def _side_cast_spec(rows, cols, steps):
    for share in (1, 2):
        blk, rem = divmod(rows * share, steps)
        if rem == 0 and blk % BF16_ROWS == 0:
            return pl.BlockSpec((blk, cols), lambda i: (i // share, 0))
    raise ValueError(f"no row block for a ({rows},{cols}) weight over {steps} steps")


def _input_stage(x2d, pos3d, b, s, gmix, wall, gqa, wuq, gkva, wkv, gq, gk, side_weights):
    t = x2d.shape[0]
    tm = TOK_TILE
    nt = s // tm
    steps = t // tm

    def tok(width):
        return pl.BlockSpec((tm, width), lambda i: (i, 0))

    def head_spec():
        return pl.BlockSpec((1, MLA_HEADS, tm, HEAD_PAD), lambda i: (i // nt, 0, i % nt, 0))

    head_shape = jax.ShapeDtypeStruct((b, MLA_HEADS, s, HEAD_PAD), BF16)
    consts = (gmix, wall, gqa, wuq, gkva, wkv, gq, gk)
    side_specs = [_side_cast_spec(w.shape[0], w.shape[1], steps) for w in side_weights]
    outs = pl.pallas_call(
        _input_kernel,
        grid=(steps,),
        in_specs=[tok(D_MODEL), pl.BlockSpec((1, 1, tm), lambda i: (i, 0, 0))]
        + [_const_spec(c.shape) for c in consts] + side_specs,
        out_specs=[head_spec(), head_spec(), head_spec(),
                   tok(HG_WIDTH), tok(HG_WIDTH), tok(HG_WIDTH), tok(HG_WIDTH), tok(2 * D_MODEL)]
        + side_specs,
        out_shape=[head_shape, head_shape, head_shape,
                   jax.ShapeDtypeStruct((t, HG_WIDTH), BF16),
                   jax.ShapeDtypeStruct((t, HG_WIDTH), F32),
                   jax.ShapeDtypeStruct((t, HG_WIDTH), BF16),
                   jax.ShapeDtypeStruct((t, HG_WIDTH), BF16),
                   jax.ShapeDtypeStruct((t, 2 * D_MODEL), BF16)]
        + [jax.ShapeDtypeStruct(w.shape, BF16) for w in side_weights],
        compiler_params=pltpu.CompilerParams(
            dimension_semantics=("arbitrary",), vmem_limit_bytes=VMEM_LIMIT),
        name="input_stage",
    )(x2d, pos3d, *consts, *side_weights)
    return outs[:8], outs[8:]


def _attention_kernel(q_ref, k_ref, v_ref, o_ref, m_ref, acc_ref, s_ref):
    qi = pl.program_id(2)
    tq = ATT_TILE
    half = tq // 2
    m_ref[...] = jnp.full(m_ref.shape, -jnp.inf, F32)
    acc_ref[...] = jnp.zeros(acc_ref.shape, F32)

    all_rows = slice(0, tq)

    def scores(j, hd, rows, n_keys):
        start = pl.multiple_of(j * tq, tq)
        s_ref[hd, rows, :n_keys] = _dot_nt(q_ref[0, hd, rows, :],
                                           k_ref[0, hd, pl.ds(start, n_keys), :])

    def accumulate(j, hd, rows, n_keys, visible):
        start = pl.multiple_of(j * tq, tq)
        sc = s_ref[hd, rows, :n_keys]
        if visible is not None:
            sc = jnp.where(visible, sc, -jnp.inf)
        m_old = m_ref[hd, rows, :]
        m_new = jnp.maximum(m_old, jnp.max(sc, axis=-1, keepdims=True))
        p = jnp.exp2(sc - jnp.concatenate([m_new] * (n_keys // LANES), axis=1)).astype(BF16)
        acc_ref[hd, rows, :] = (jnp.exp2(m_old - m_new) * acc_ref[hd, rows, :]
                                + _dot(p, v_ref[0, hd, pl.ds(start, n_keys), :]))
        m_ref[hd, rows, :] = m_new

    def block(j, rows, n_keys, visible):
        for hd in range(ATT_HEADS):
            scores(j, hd, rows, n_keys)
        for hd in range(ATT_HEADS):
            accumulate(j, hd, rows, n_keys, visible)

    def body(t, carry):
        for hd in range(ATT_HEADS):
            scores(2 * t, hd, all_rows, tq)
        for hd in range(ATT_HEADS):
            accumulate(2 * t, hd, all_rows, tq, None)
            scores(2 * t + 1, hd, all_rows, tq)
        for hd in range(ATT_HEADS):
            accumulate(2 * t + 1, hd, all_rows, tq, None)
        return carry

    lax.fori_loop(0, qi // 2, body, 0)

    @pl.when(qi % 2 == 1)
    def _():
        block(qi - 1, slice(0, tq), tq, None)

    def lower(n_keys, shift):
        r_i = lax.broadcasted_iota(jnp.int32, (half, n_keys), 0)
        c_i = lax.broadcasted_iota(jnp.int32, (half, n_keys), 1)
        return c_i <= r_i + shift

    block(qi, slice(0, half), half, lower(half, 0))
    block(qi, slice(half, tq), tq, lower(tq, half))

    low = lax.broadcasted_iota(jnp.int32, (tq, LANES), 1) < V_HEAD_DIM
    for pair in range(ATT_HEADS // 2):
        even, odd = acc_ref[2 * pair], acc_ref[2 * pair + 1]
        out = jnp.where(low, even / pltpu.roll(even, V_HEAD_DIM, 1), pltpu.roll(odd, V_HEAD_DIM, 1) / odd)
        o_ref[0, :, pair * LANES:(pair + 1) * LANES] = out.astype(BF16)


def _attention_stage(q, k, v):
    b, nh, s, _ = q.shape
    tq = ATT_TILE
    g = ATT_HEADS
    return pl.pallas_call(
        _attention_kernel,
        grid=(b, nh // g, s // tq),
        in_specs=[pl.BlockSpec((1, g, tq, HEAD_PAD), lambda bi, hp, i: (bi, hp, i, 0)),
                  pl.BlockSpec((1, g, s, HEAD_PAD), lambda bi, hp, i: (bi, hp, 0, 0)),
                  pl.BlockSpec((1, g, s, HEAD_PAD), lambda bi, hp, i: (bi, hp, 0, 0))],
        out_specs=pl.BlockSpec((1, tq, g * V_HEAD_DIM), lambda bi, hp, i: (bi, i, hp)),
        out_shape=jax.ShapeDtypeStruct((b, s, BRANCH_WIDTH), BF16),
        scratch_shapes=[pltpu.VMEM((g, tq, LANES), F32), pltpu.VMEM((g, tq, HEAD_PAD), F32),
                        pltpu.VMEM((g, tq, tq), F32)],
        compiler_params=pltpu.CompilerParams(
            dimension_semantics=("arbitrary", "arbitrary", "arbitrary"),
            vmem_limit_bytes=VMEM_LIMIT),
        name="causal_attention",
    )(q, k, v)


def _hgrn_span_matrix():
    c = HG_CHUNK
    x = np.arange(c)[:, None]
    y = np.arange(c)[None, :]
    mats = [(y <= x)]
    for b in HG_LEVELS:
        r = (x // (2 * b)) * (2 * b) + b
        mats.append((y > np.minimum(x, r)) & (y <= np.maximum(x, r)))
    m = np.concatenate(mats, axis=0).astype(np.float32)
    return np.concatenate([m, m], axis=1)


def _block_diag_rows(x):
    a, b = x[:, :HG_DIM], x[:, HG_DIM:]
    zero = jnp.zeros_like(a)
    return jnp.concatenate([jnp.concatenate([a, zero], axis=1),
                            jnp.concatenate([zero, b], axis=1)], axis=0)


def _hgrn_chunk_local(q, hf, v_bf, lb, span, right_rows, pair_masks):
    c = HG_CHUNK
    f = lb + (1.0 - lb) * jax.nn.sigmoid(hf)
    lf = jnp.log2(f)
    kk = 1.0 - f
    lf_hi = lf.astype(BF16)
    lf_lo = (lf - lf_hi.astype(F32)).astype(BF16)
    spans = _dot(span, jnp.concatenate([lf_hi, lf_lo], axis=0))
    cum = spans[:c]
    last = cum[c - 1:c, :]
    q_dec = (q * jnp.exp2(cum)).astype(BF16)
    k_dec = (kk * jnp.exp2(last - cum)).astype(BF16)
    q_bf = q.astype(BF16)
    k_bf = kk.astype(BF16)
    zs = [(jnp.where(right_rows[lvl], q, kk)
           * jnp.exp2(spans[(lvl + 1) * c:(lvl + 2) * c])).astype(BF16)
          for lvl in range(len(HG_LEVELS))]

    o_intra, incr = [], []
    for pair in range(HG_HEADS // 2):
        psl = slice(2 * pair * HG_DIM, (2 * pair + 2) * HG_DIM)
        att = jnp.where(pair_masks[0], _dot_nt(q_bf[:, psl], _block_diag_rows(k_bf[:, psl])), 0.0)
        for lvl in range(len(HG_LEVELS)):
            z = zs[lvl][:, psl]
            att = jnp.where(pair_masks[lvl + 1], _dot_nt(z, _block_diag_rows(z)), att)
        v_pair = v_bf[:, psl]
        o_intra.append(_dot(att.astype(BF16), _block_diag_rows(v_pair)))
        for hd in (2 * pair, 2 * pair + 1):
            sl = slice(hd * HG_DIM, (hd + 1) * HG_DIM)
            incr.append(_dot(v_bf[:, sl].astype(F32).T.astype(BF16), k_dec[:, sl]))
    return o_intra, q_dec, incr, jnp.exp2(last)


def _hgrn_kernel(hq_ref, hf_ref, hi_ref, hg_ref, lb_ref, gout_ref, span_ref, o_ref, state_ref):
    @pl.when(pl.program_id(1) == 0)
    def _():
        state_ref[...] = jnp.zeros(state_ref.shape, F32)

    c = HG_CHUNK
    t_i = lax.broadcasted_iota(jnp.int32, (c, 2 * c), 0)
    s_i = lax.broadcasted_iota(jnp.int32, (c, 2 * c), 1) % c
    row = lax.broadcasted_iota(jnp.int32, (c, HG_WIDTH), 0)
    pair_masks = [t_i == s_i] + [((t_i // b) ^ (s_i // b) == 1) & (s_i < t_i) for b in HG_LEVELS]
    right_rows = [(row // b) % 2 == 1 for b in HG_LEVELS]
    span = span_ref[...]
    lb = lb_ref[...]
    n_chunks = HG_BLOCK // c

    local = []
    for ci in range(n_chunks):
        rows = slice(ci * c, (ci + 1) * c)
        local.append(_hgrn_chunk_local(hq_ref[0, rows, :].astype(F32), hf_ref[0, rows, :],
                                       hi_ref[0, rows, :], lb, span, right_rows, pair_masks))

    gout = gout_ref[...]
    for pair in range(HG_HEADS // 2):
        hds = (2 * pair, 2 * pair + 1)
        psl = slice(2 * pair * HG_DIM, (2 * pair + 2) * HG_DIM)
        states = [state_ref[hd] for hd in hds]
        for ci in range(n_chunks):
            rows = slice(ci * c, (ci + 1) * c)
            o_intra, q_dec, incr, decay = local[ci]
            both = jnp.concatenate([st.astype(BF16) for st in states], axis=1)
            o_pair = o_intra[pair] + _dot_nt(q_dec[:, psl], _block_diag_rows(both))
            for n, hd in enumerate(hds):
                sl = slice(hd * HG_DIM, (hd + 1) * HG_DIM)
                states[n] = states[n] * decay[:, sl] + incr[hd]
                o = o_pair[:, n * HG_DIM:(n + 1) * HG_DIM]
                gate = hg_ref[0, rows, sl].astype(F32)
                o_ref[0, rows, sl] = (_rms(o, gout) * (gate * jax.nn.sigmoid(gate))).astype(BF16)
        for n, hd in enumerate(hds):
            state_ref[hd] = states[n]


def _hgrn_stage(hq, hf, hi, hg, lb, gout):
    span = jnp.asarray(_hgrn_span_matrix(), BF16)
    b, s, _ = hq.shape
    tb = HG_BLOCK
    tok = pl.BlockSpec((1, tb, HG_WIDTH), lambda bi, i: (bi, i, 0))
    return pl.pallas_call(
        _hgrn_kernel,
        grid=(b, s // tb),
        in_specs=[tok, tok, tok, tok, _const_spec(lb.shape), _const_spec(gout.shape),
                  _const_spec(span.shape)],
        out_specs=tok,
        out_shape=jax.ShapeDtypeStruct((b, s, HG_WIDTH), BF16),
        scratch_shapes=[pltpu.VMEM((HG_HEADS, HG_DIM, HG_DIM), F32)],
        compiler_params=pltpu.CompilerParams(
            dimension_semantics=("arbitrary", "arbitrary"), vmem_limit_bytes=VMEM_LIMIT),
        name="hgrn2_recurrence",
    )(hq, hf, hi, hg, lb, gout, span)


def _output_kernel(x_ref, attn_ref, rec_ref, gates_ref, p_ref,
                   wb_ref, wout_ref, gffn_ref, wgate_ref, wup_ref, wdown_ref,
                   gpg_ref, wpg_ref, wpp_ref, gpost_ref, o_ref):
    halves = _row_parts(x_ref.shape[0])
    merged = []
    for rows in halves:
        y0 = _dot(attn_ref[rows, :], wb_ref[:BRANCH_WIDTH, :])
        y1 = _dot(rec_ref[rows, :], wb_ref[BRANCH_WIDTH:, :])
        g0 = gates_ref[rows, :D_MODEL].astype(F32)
        g1 = gates_ref[rows, D_MODEL:].astype(F32)
        merged.append((g0 * y0 + g1 * y1).astype(BF16))
    x1 = [x_ref[rows, :] + _dot(m, wout_ref[...]) for rows, m in zip(halves, merged)]
    e = [_rms(_dot(p_ref[rows, :].astype(BF16), wpp_ref[...]), gpost_ref[...]) for rows in halves]
    h2 = [_rms(v, gffn_ref[...]).astype(BF16) for v in x1]
    x2 = list(x1)
    lo = 0
    for width in FFN_SPLITS:
        for n in range(len(halves)):
            a = _dot(h2[n], wgate_ref[:, lo:lo + width])
            u = _dot(h2[n], wup_ref[:, lo:lo + width])
            z = (a * jax.nn.sigmoid(a) * u).astype(BF16)
            x2[n] = x2[n] + _dot(z, wdown_ref[lo:lo + width, :])
        lo += width
    for n, rows in enumerate(halves):
        g = jax.nn.sigmoid(_dot(_rms(x2[n], gpg_ref[...]).astype(BF16), wpg_ref[...]))
        o_ref[rows, :] = x2[n] + g * e[n]


def _output_stage(x2d, attn, rec, gates, p2d, wb, wout, gffn, wgate, wup, wdown,
                  gpg, wpg, wpp, gpost):
    t = x2d.shape[0]
    tm = TOK_TILE

    def tok(width):
        return pl.BlockSpec((tm, width), lambda i: (i, 0))

    consts = (wb, wout, gffn, wgate, wup, wdown, gpg, wpg, wpp, gpost)
    return pl.pallas_call(
        _output_kernel,
        grid=(t // tm,),
        in_specs=[tok(D_MODEL), tok(BRANCH_WIDTH), tok(HG_WIDTH), tok(2 * D_MODEL), tok(PLE_DIM)]
        + [_const_spec(c.shape) for c in consts],
        out_specs=tok(D_MODEL),
        out_shape=jax.ShapeDtypeStruct((t, D_MODEL), F32),
        compiler_params=pltpu.CompilerParams(
            dimension_semantics=("arbitrary",), vmem_limit_bytes=VMEM_LIMIT),
        name="output_stage",
    )(x2d, attn, rec, gates, p2d, *consts)


def _pad_heads(w, heads, width):
    rows = w.shape[0]
    w = w.reshape(rows, heads, width)
    w = jnp.pad(w, ((0, 0), (0, 0), (0, HEAD_PAD - width)))
    return w.reshape(rows, heads * HEAD_PAD)


def _qk_head_layout(nope, rope):
    like = nope if nope is not None else rope

    def z(width):
        return jnp.zeros(like.shape[:-1] + (width,), like.dtype)

    n0 = nope[..., :HEAD_HALF_NOPE] if nope is not None else z(HEAD_HALF_NOPE)
    n1 = nope[..., HEAD_HALF_NOPE:] if nope is not None else z(HEAD_HALF_NOPE)
    x1 = rope[..., :ROPE_HALF] if rope is not None else z(ROPE_HALF)
    x2 = rope[..., ROPE_HALF:] if rope is not None else z(ROPE_HALF)
    return jnp.concatenate([n0, x1, z(ROPE_HALF), n1, x2, z(ROPE_HALF)], axis=-1)


def _row(v):
    return v.reshape(1, -1).astype(F32)


def kernel(x, p, positions, mix_norm_g, w_in, q_a_norm_g, w_uq, kv_a_norm_g, w_ukv, q_norm_g, k_norm_g, hg_lb_logits, hg_out_norm_g, w_branch, w_out, ffn_norm_g, w_ffn_gate, w_ffn_up, w_ffn_down, ple_gate_norm_g, w_ple_gate, w_ple_proj, ple_post_norm_g):
    b, s, d = x.shape
    depth = w_in.shape[0]
    assert d == D_MODEL and s % max(HG_BLOCK, ATT_TILE, TOK_TILE) == 0, (b, s, d)
    assert w_in.shape[1:] == (D_MODEL, W_ALL_COLS - HEAD_PAD + QK_ROPE_DIM), w_in.shape
    lower_bounds = jnp.cumsum(jax.nn.softmax(hg_lb_logits.astype(F32), axis=0), axis=0)
    pos3d = positions.reshape(b * s // TOK_TILE, 1, TOK_TILE)
    x2d = x.reshape(b * s, d)
    q_scale = QK_HEAD_DIM ** -0.5 * math.log2(math.e)

    for layer in range(depth):
        wi = w_in[layer]
        wall = _input_weight_layout(wi.T)
        wq3 = w_uq[layer].reshape(Q_LORA_RANK, MLA_HEADS, QK_HEAD_DIM)
        wuq = _qk_head_layout(wq3[..., :QK_NOPE_DIM], wq3[..., QK_NOPE_DIM:])
        wuq = wuq.reshape(Q_LORA_RANK, MLA_HEADS * HEAD_PAD).astype(BF16)
        wukv = w_ukv[layer].reshape(KV_LORA_RANK, MLA_HEADS, QK_NOPE_DIM + V_HEAD_DIM)
        wk = _qk_head_layout(wukv[..., :QK_NOPE_DIM], None).reshape(KV_LORA_RANK, -1)
        wv = _pad_heads(wukv[:, :, QK_NOPE_DIM:].reshape(KV_LORA_RANK, -1), MLA_HEADS, V_HEAD_DIM)
        wkv = jnp.concatenate([wk, wv], axis=1).astype(BF16)
        gq = q_norm_g[layer].astype(F32) * q_scale
        gk = k_norm_g[layer].astype(F32)

        side = (w_branch[layer].reshape(2 * BRANCH_WIDTH, d), w_out[layer], w_ffn_gate[layer],
                w_ffn_up[layer], w_ffn_down[layer], w_ple_gate[layer], w_ple_proj[layer])
        (q, k, v, hq, hf, hi, hg, gates), side_bf = _input_stage(
            x2d, pos3d, b, s, _row(mix_norm_g[layer]), wall, _row(q_a_norm_g[layer]), wuq,
            _row(kv_a_norm_g[layer]), wkv,
            _row(_qk_head_layout(gq[:QK_NOPE_DIM], gq[QK_NOPE_DIM:])),
            _row(_qk_head_layout(gk[:QK_NOPE_DIM], gk[QK_NOPE_DIM:])), side)
        wb, wout, wgate, wup, wdown, wpg, wpp = side_bf

        attn = _attention_stage(q, k, v)
        rec = _hgrn_stage(hq.reshape(b, s, HG_WIDTH), hf.reshape(b, s, HG_WIDTH),
                          hi.reshape(b, s, HG_WIDTH), hg.reshape(b, s, HG_WIDTH),
                          _row(lower_bounds[layer]), _row(hg_out_norm_g[layer]))

        x2d = _output_stage(
            x2d, attn.reshape(b * s, BRANCH_WIDTH), rec.reshape(b * s, HG_WIDTH), gates,
            p[layer].reshape(b * s, PLE_DIM),
            wb, wout, _row(ffn_norm_g[layer]),
            wgate, wup, wdown, _row(ple_gate_norm_g[layer]), wpg, wpp,
            _row(ple_post_norm_g[layer]))
    return x2d.reshape(b, s, d)
```

```python
import math

import numpy as np

import jax
import jax.numpy as jnp
from jax import lax
from jax.experimental import pallas as pl
from jax.experimental.pallas import tpu as pltpu

D_MODEL = 1024
MLA_HEADS = 8
QK_NOPE_DIM = 64
QK_ROPE_DIM = 32
ROPE_HALF = QK_ROPE_DIM // 2
QK_HEAD_DIM = QK_NOPE_DIM + QK_ROPE_DIM
V_HEAD_DIM = 64
Q_LORA_RANK = 384
KV_LORA_RANK = 256
ROPE_BASE = 10000.0
HG_HEADS = 4
HG_DIM = 128
HG_WIDTH = HG_HEADS * HG_DIM
BRANCH_WIDTH = MLA_HEADS * V_HEAD_DIM
FFN_HIDDEN = 2816
PLE_DIM = 256
EPS = 1e-6

LANES = 128
BF16_ROWS = 16
HEAD_PAD = LANES
HEAD_HALF_NOPE = QK_NOPE_DIM // 2
W_LAT = Q_LORA_RANK + KV_LORA_RANK + HEAD_PAD

VMEM_LIMIT = 60 * 1024 * 1024

TOK_TILE = 512
ROW_PARTS = 2
ATT_TILE = 512
ATT_HEADS = 8
HG_BLOCK = 1024
HG_CHUNK = 64
HG_LEVELS = tuple(HG_CHUNK >> (i + 1) for i in range(HG_CHUNK.bit_length() - 1))
FFN_SPLITS = (1024, 1024, 768)
assert sum(FFN_SPLITS) == FFN_HIDDEN

BF16 = jnp.bfloat16
F32 = jnp.float32


def _const_spec(shape):
    nd = len(shape)
    return pl.BlockSpec(shape, lambda *_: (0,) * nd, pipeline_mode=pl.Buffered(1))


def _rms(v, gain):
    ms = jnp.sum(v * v, axis=-1, keepdims=True) * (1.0 / v.shape[-1])
    return v * lax.rsqrt(ms + EPS) * gain


def _dot(a, b):
    return jnp.dot(a, b, preferred_element_type=F32)


def _row_parts(n_rows):
    step = n_rows // ROW_PARTS
    return tuple(slice(i * step, (i + 1) * step) for i in range(ROW_PARTS))


def _dot_nt(a, b):
    return lax.dot_general(a, b, (((1,), (1,)), ((), ())), preferred_element_type=F32)


def _rope_tables(pos_row):
    tm = pos_row.shape[1]
    fidx = lax.broadcasted_iota(jnp.int32, (ROPE_HALF, 1), 0).astype(F32)
    inv_freq = jnp.exp(fidx * (-math.log(ROPE_BASE) * 2.0 / QK_ROPE_DIM))
    ang = pos_row.astype(F32) * inv_freq
    cos = jnp.cos(ang)
    sin = jnp.sin(ang)
    ones = jnp.ones((HEAD_HALF_NOPE, tm), F32)
    zeros_n = jnp.zeros((HEAD_HALF_NOPE, tm), F32)
    zeros_p = jnp.zeros((ROPE_HALF, tm), F32)
    cos_t = jnp.concatenate([ones, cos, zeros_p, ones, cos, zeros_p], axis=0)
    sin_t = jnp.concatenate([zeros_n, -sin, zeros_p, zeros_n, sin, zeros_p], axis=0)
    return cos_t.T, sin_t.T


def _input_kernel(x_ref, pos_ref, gmix_ref, wall_ref, gqa_ref, wuq_ref, gkva_ref, wkv_ref,
                  gq_ref, gk_ref, lb_ref, *rest):
    n_side = (len(rest) - 8) // 2
    side_in = rest[:n_side]
    q_ref, k_ref, v_ref, hq_ref, hf_ref, hi_ref, hg_ref, gates_ref = rest[n_side:n_side + 8]
    side_out = rest[n_side + 8:]
    for w_ref, wb_ref in zip(side_in, side_out):
        wb_ref[...] = w_ref[...].astype(BF16)

    halves = _row_parts(x_ref.shape[0])
    gq = gq_ref[...]
    gk = gk_ref[...]
    lane = lax.broadcasted_iota(jnp.int32, (1, LANES), 1)
    one_lane = (lane >= V_HEAD_DIM).astype(F32)
    inv_width = 1.0 / QK_HEAD_DIM

    h = [_rms(x_ref[rows, :], gmix_ref[...]).astype(BF16) for rows in halves]
    lat = [_dot(v, wall_ref[:, :W_LAT]) for v in h]
    cq = [_rms(v[:, :Q_LORA_RANK], gqa_ref[...]).astype(BF16) for v in lat]
    ckv = [_rms(v[:, Q_LORA_RANK:Q_LORA_RANK + KV_LORA_RANK], gkva_ref[...]).astype(BF16)
           for v in lat]
    q_all = [_dot(v, wuq_ref[...]) for v in cq]
    kv_all = [_dot(v, wkv_ref[...]) for v in ckv]

    for n, rows in enumerate(halves):
        k_rope = lat[n][:, Q_LORA_RANK + KV_LORA_RANK:]
        cos_tab, sin_tab = _rope_tables(pos_ref[0, :, rows])
        q_cos = cos_tab * gq
        q_sin = sin_tab * pltpu.roll(gq, LANES // 2, 1)
        kr = k_rope * gk
        kr = kr * cos_tab + pltpu.roll(kr, LANES // 2, 1) * sin_tab
        kr_ss = jnp.sum(k_rope * k_rope, axis=-1, keepdims=True)
        for hd in range(MLA_HEADS):
            sl = slice(hd * HEAD_PAD, (hd + 1) * HEAD_PAD)
            t = q_all[n][:, sl]
            r = lax.rsqrt(jnp.sum(t * t, axis=-1, keepdims=True) * inv_width + EPS)
            q_ref[0, hd, rows, :] = ((t * q_cos + pltpu.roll(t, LANES // 2, 1) * q_sin) * r).astype(BF16)
            t = kv_all[n][:, sl]
            r = lax.rsqrt((jnp.sum(t * t, axis=-1, keepdims=True) + kr_ss) * inv_width + EPS)
            k_ref[0, hd, rows, :] = ((t * gk + kr) * r).astype(BF16)
            vsl = slice(MLA_HEADS * HEAD_PAD + hd * HEAD_PAD, MLA_HEADS * HEAD_PAD + (hd + 1) * HEAD_PAD)
            v_ref[0, hd, rows, :] = (kv_all[n][:, vsl] + one_lane).astype(BF16)

    for n, rows in enumerate(halves):
        hh = _dot(h[n], wall_ref[:, W_LAT:W_LAT + 4 * HG_WIDTH])
        hq_ref[rows, :] = hh[:, :HG_WIDTH].astype(BF16)
        lb = lb_ref[...]
        f = lb + (1.0 - lb) * jax.nn.sigmoid(hh[:, HG_WIDTH:2 * HG_WIDTH])
        lf = jnp.log2(f)
        lf_hi = lf.astype(BF16)
        hf_ref[rows, :HG_WIDTH] = lf_hi
        hf_ref[rows, HG_WIDTH:2 * HG_WIDTH] = (lf - lf_hi.astype(F32)).astype(BF16)
        hf_ref[rows, 2 * HG_WIDTH:] = (1.0 - f).astype(BF16)
        hi_ref[rows, :] = hh[:, 2 * HG_WIDTH:3 * HG_WIDTH].astype(BF16)
        hg_ref[rows, :] = hh[:, 3 * HG_WIDTH:].astype(BF16)
    for n, rows in enumerate(halves):
        gates_ref[rows, :] = jax.nn.sigmoid(
            _dot(h[n], wall_ref[:, W_LAT + 4 * HG_WIDTH:])).astype(BF16)


W_ALL_COLS = W_LAT + 4 * HG_WIDTH + 2 * D_MODEL
W_PREP_COLS = W_ALL_COLS // 2


def _input_weight_kernel(wt_ref, o_ref):
    c0 = Q_LORA_RANK + KV_LORA_RANK
    j = pl.program_id(0)
    n = W_PREP_COLS
    lat_full = c0 // n
    lat_rest = c0 - lat_full * n

    @pl.when(j != lat_full)
    def _():
        o_ref[...] = wt_ref[...].T.astype(BF16)

    @pl.when(j == lat_full)
    def _():
        d = wt_ref.shape[1]
        z_n = jnp.zeros((HEAD_HALF_NOPE, d), F32)
        z_p = jnp.zeros((ROPE_HALF, d), F32)
        parts = [wt_ref[:lat_rest, :], z_n, wt_ref[lat_rest:lat_rest + ROPE_HALF, :], z_p,
                 z_n, wt_ref[lat_rest + ROPE_HALF:lat_rest + QK_ROPE_DIM, :], z_p]
        tail = n - lat_rest - HEAD_PAD
        if tail:
            parts.append(wt_ref[lat_rest + QK_ROPE_DIM:lat_rest + QK_ROPE_DIM + tail, :])
        o_ref[...] = jnp.concatenate(parts, axis=0).T.astype(BF16)


def _input_weight_layout(wt):
    cols, rows = wt.shape
    out_cols = cols - QK_ROPE_DIM + HEAD_PAD
    n = W_PREP_COLS
    lat_full = (Q_LORA_RANK + KV_LORA_RANK) // n
    pad = HEAD_PAD - QK_ROPE_DIM

    def in_rows(j):
        return (pl.multiple_of(jnp.where(j <= lat_full, j * n, j * n - pad), QK_ROPE_DIM), 0)

    return pl.pallas_call(
        _input_weight_kernel,
        grid=(out_cols // n,),
        in_specs=[pl.BlockSpec((pl.Element(n), pl.Element(rows)), in_rows)],
        out_specs=pl.BlockSpec((rows, n), lambda j: (0, j)),
        out_shape=jax.ShapeDtypeStruct((rows, out_cols), BF16),
        compiler_params=pltpu.CompilerParams(
            dimension_semantics=("arbitrary",), vmem_limit_bytes=VMEM_LIMIT),
        name="input_weight_layout",
    )(wt)


def _side_cast_spec(rows, cols, steps):
    for share in (1, 2):
        blk, rem = divmod(rows * share, steps)
        if rem == 0 and blk % BF16_ROWS == 0:
            return pl.BlockSpec((blk, cols), lambda i: (i // share, 0))
    raise ValueError(f"no row block for a ({rows},{cols}) weight over {steps} steps")


def _input_stage(x2d, pos3d, b, s, gmix, wall, gqa, wuq, gkva, wkv, gq, gk, lb, side_weights):
    t = x2d.shape[0]
    tm = TOK_TILE
    nt = s // tm
    steps = t // tm

    def tok(width):
        return pl.BlockSpec((tm, width), lambda i: (i, 0))

    def head_spec():
        return pl.BlockSpec((1, MLA_HEADS, tm, HEAD_PAD), lambda i: (i // nt, 0, i % nt, 0))

    head_shape = jax.ShapeDtypeStruct((b, MLA_HEADS, s, HEAD_PAD), BF16)
    consts = (gmix, wall, gqa, wuq, gkva, wkv, gq, gk, lb)
    side_specs = [_side_cast_spec(w.shape[0], w.shape[1], steps) for w in side_weights]
    outs = pl.pallas_call(
        _input_kernel,
        grid=(steps,),
        in_specs=[tok(D_MODEL), pl.BlockSpec((1, 1, tm), lambda i: (i, 0, 0))]
        + [_const_spec(c.shape) for c in consts] + side_specs,
        out_specs=[head_spec(), head_spec(), head_spec(),
                   tok(HG_WIDTH), tok(3 * HG_WIDTH), tok(HG_WIDTH), tok(HG_WIDTH), tok(2 * D_MODEL)]
        + side_specs,
        out_shape=[head_shape, head_shape, head_shape,
                   jax.ShapeDtypeStruct((t, HG_WIDTH), BF16),
                   jax.ShapeDtypeStruct((t, 3 * HG_WIDTH), BF16),
                   jax.ShapeDtypeStruct((t, HG_WIDTH), BF16),
                   jax.ShapeDtypeStruct((t, HG_WIDTH), BF16),
                   jax.ShapeDtypeStruct((t, 2 * D_MODEL), BF16)]
        + [jax.ShapeDtypeStruct(w.shape, BF16) for w in side_weights],
        compiler_params=pltpu.CompilerParams(
            dimension_semantics=("arbitrary",), vmem_limit_bytes=VMEM_LIMIT),
        name="input_stage",
    )(x2d, pos3d, *consts, *side_weights)
    return outs[:8], outs[8:]


def _attention_kernel(q_ref, k_ref, v_ref, o_ref, m_ref, acc_ref, s_ref):
    qi = pl.program_id(2)
    tq = ATT_TILE
    half = tq // 2
    m_ref[...] = jnp.full(m_ref.shape, -jnp.inf, F32)
    acc_ref[...] = jnp.zeros(acc_ref.shape, F32)

    all_rows = slice(0, tq)

    def scores(j, hd, rows, n_keys):
        start = pl.multiple_of(j * tq, tq)
        s_ref[hd, rows, :n_keys] = _dot_nt(q_ref[0, hd, rows, :],
                                           k_ref[0, hd, pl.ds(start, n_keys), :])

    def accumulate(j, hd, rows, n_keys, visible):
        start = pl.multiple_of(j * tq, tq)
        sc = s_ref[hd, rows, :n_keys]
        if visible is not None:
            sc = jnp.where(visible, sc, -jnp.inf)
        m_old = m_ref[hd, rows, :]
        m_new = jnp.maximum(m_old, jnp.max(sc, axis=-1, keepdims=True))
        p = jnp.exp2(sc - jnp.concatenate([m_new] * (n_keys // LANES), axis=1)).astype(BF16)
        acc_ref[hd, rows, :] = (jnp.exp2(m_old - m_new) * acc_ref[hd, rows, :]
                                + _dot(p, v_ref[0, hd, pl.ds(start, n_keys), :]))
        m_ref[hd, rows, :] = m_new

    def block(j, rows, n_keys, visible):
        for hd in range(ATT_HEADS):
            scores(j, hd, rows, n_keys)
        for hd in range(ATT_HEADS):
            accumulate(j, hd, rows, n_keys, visible)

    def body(t, carry):
        for hd in range(ATT_HEADS):
            scores(2 * t, hd, all_rows, tq)
        for hd in range(ATT_HEADS):
            accumulate(2 * t, hd, all_rows, tq, None)
            scores(2 * t + 1, hd, all_rows, tq)
        for hd in range(ATT_HEADS):
            accumulate(2 * t + 1, hd, all_rows, tq, None)
        return carry

    lax.fori_loop(0, qi // 2, body, 0)

    @pl.when(qi % 2 == 1)
    def _():
        block(qi - 1, slice(0, tq), tq, None)

    def lower(n_keys, shift):
        r_i = lax.broadcasted_iota(jnp.int32, (half, n_keys), 0)
        c_i = lax.broadcasted_iota(jnp.int32, (half, n_keys), 1)
        return c_i <= r_i + shift

    block(qi, slice(0, half), half, lower(half, 0))
    block(qi, slice(half, tq), tq, lower(tq, half))

    low = lax.broadcasted_iota(jnp.int32, (tq, LANES), 1) < V_HEAD_DIM
    for pair in range(ATT_HEADS // 2):
        even, odd = acc_ref[2 * pair], acc_ref[2 * pair + 1]
        out = jnp.where(low, even / pltpu.roll(even, V_HEAD_DIM, 1), pltpu.roll(odd, V_HEAD_DIM, 1) / odd)
        o_ref[0, :, pair * LANES:(pair + 1) * LANES] = out.astype(BF16)


def _attention_stage(q, k, v):
    b, nh, s, _ = q.shape
    tq = ATT_TILE
    g = ATT_HEADS
    return pl.pallas_call(
        _attention_kernel,
        grid=(b, nh // g, s // tq),
        in_specs=[pl.BlockSpec((1, g, tq, HEAD_PAD), lambda bi, hp, i: (bi, hp, i, 0)),
                  pl.BlockSpec((1, g, s, HEAD_PAD), lambda bi, hp, i: (bi, hp, 0, 0)),
                  pl.BlockSpec((1, g, s, HEAD_PAD), lambda bi, hp, i: (bi, hp, 0, 0))],
        out_specs=pl.BlockSpec((1, tq, g * V_HEAD_DIM), lambda bi, hp, i: (bi, i, hp)),
        out_shape=jax.ShapeDtypeStruct((b, s, BRANCH_WIDTH), BF16),
        scratch_shapes=[pltpu.VMEM((g, tq, LANES), F32), pltpu.VMEM((g, tq, HEAD_PAD), F32),
                        pltpu.VMEM((g, tq, tq), F32)],
        compiler_params=pltpu.CompilerParams(
            dimension_semantics=("arbitrary", "arbitrary", "arbitrary"),
            vmem_limit_bytes=VMEM_LIMIT),
        name="causal_attention",
    )(q, k, v)


def _hgrn_span_matrix():
    c = HG_CHUNK
    x = np.arange(c)[:, None]
    y = np.arange(c)[None, :]
    mats = [(y <= x)]
    for b in HG_LEVELS:
        r = (x // (2 * b)) * (2 * b) + b
        mats.append((y > np.minimum(x, r)) & (y <= np.maximum(x, r)))
    m = np.concatenate(mats, axis=0).astype(np.float32)
    return np.concatenate([m, m], axis=1)


def _block_diag_rows(x):
    a, b = x[:, :HG_DIM], x[:, HG_DIM:]
    zero = jnp.zeros_like(a)
    return jnp.concatenate([jnp.concatenate([a, zero], axis=1),
                            jnp.concatenate([zero, b], axis=1)], axis=0)


def _hgrn_chunk_local(q, lf_split, kk_bf, v_bf, span, right_rows, pair_masks):
    c = HG_CHUNK
    kk = kk_bf.astype(F32)
    spans = _dot(span, jnp.concatenate([lf_split[:, :HG_WIDTH], lf_split[:, HG_WIDTH:]],
                                       axis=0))
    cum = spans[:c]
    last = cum[c - 1:c, :]
    q_dec = (q * jnp.exp2(cum)).astype(BF16)
    k_dec = (kk * jnp.exp2(last - cum)).astype(BF16)
    q_bf = q.astype(BF16)
    k_bf = kk_bf
    zs = [(jnp.where(right_rows[lvl], q, kk)
           * jnp.exp2(spans[(lvl + 1) * c:(lvl + 2) * c])).astype(BF16)
          for lvl in range(len(HG_LEVELS))]

    o_intra, incr = [], []
    for pair in range(HG_HEADS // 2):
        psl = slice(2 * pair * HG_DIM, (2 * pair + 2) * HG_DIM)
        att = jnp.where(pair_masks[0], _dot_nt(q_bf[:, psl], _block_diag_rows(k_bf[:, psl])), 0.0)
        for lvl in range(len(HG_LEVELS)):
            z = zs[lvl][:, psl]
            att = jnp.where(pair_masks[lvl + 1], _dot_nt(z, _block_diag_rows(z)), att)
        v_pair = v_bf[:, psl]
        o_intra.append(_dot(att.astype(BF16), _block_diag_rows(v_pair)))
        for hd in (2 * pair, 2 * pair + 1):
            sl = slice(hd * HG_DIM, (hd + 1) * HG_DIM)
            incr.append(_dot(v_bf[:, sl].astype(F32).T.astype(BF16), k_dec[:, sl]))
    return o_intra, q_dec, incr, jnp.exp2(last)


def _hgrn_kernel(hq_ref, hf_ref, hi_ref, hg_ref, gout_ref, span_ref, o_ref, state_ref):
    @pl.when(pl.program_id(1) == 0)
    def _():
        state_ref[...] = jnp.zeros(state_ref.shape, F32)

    c = HG_CHUNK
    t_i = lax.broadcasted_iota(jnp.int32, (c, 2 * c), 0)
    s_i = lax.broadcasted_iota(jnp.int32, (c, 2 * c), 1) % c
    row = lax.broadcasted_iota(jnp.int32, (c, HG_WIDTH), 0)
    pair_masks = [t_i == s_i] + [((t_i // b) ^ (s_i // b) == 1) & (s_i < t_i) for b in HG_LEVELS]
    right_rows = [(row // b) % 2 == 1 for b in HG_LEVELS]
    span = span_ref[...]
    n_chunks = HG_BLOCK // c

    local = []
    for ci in range(n_chunks):
        rows = slice(ci * c, (ci + 1) * c)
        local.append(_hgrn_chunk_local(hq_ref[0, rows, :].astype(F32),
                                       hf_ref[0, rows, :2 * HG_WIDTH], hf_ref[0, rows, 2 * HG_WIDTH:],
                                       hi_ref[0, rows, :], span, right_rows, pair_masks))

    gout = gout_ref[...]
    for pair in range(HG_HEADS // 2):
        hds = (2 * pair, 2 * pair + 1)
        psl = slice(2 * pair * HG_DIM, (2 * pair + 2) * HG_DIM)
        states = [state_ref[hd] for hd in hds]
        for ci in range(n_chunks):
            rows = slice(ci * c, (ci + 1) * c)
            o_intra, q_dec, incr, decay = local[ci]
            both = jnp.concatenate([st.astype(BF16) for st in states], axis=1)
            o_pair = o_intra[pair] + _dot_nt(q_dec[:, psl], _block_diag_rows(both))
            for n, hd in enumerate(hds):
                sl = slice(hd * HG_DIM, (hd + 1) * HG_DIM)
                states[n] = states[n] * decay[:, sl] + incr[hd]
                o = o_pair[:, n * HG_DIM:(n + 1) * HG_DIM]
                gate = hg_ref[0, rows, sl].astype(F32)
                o_ref[0, rows, sl] = (_rms(o, gout) * (gate * jax.nn.sigmoid(gate))).astype(BF16)
        for n, hd in enumerate(hds):
            state_ref[hd] = states[n]


def _hgrn_stage(hq, hf, hi, hg, gout):
    span = jnp.asarray(_hgrn_span_matrix(), BF16)
    b, s, _ = hq.shape
    tb = HG_BLOCK
    tok = pl.BlockSpec((1, tb, HG_WIDTH), lambda bi, i: (bi, i, 0))
    tok3 = pl.BlockSpec((1, tb, 3 * HG_WIDTH), lambda bi, i: (bi, i, 0))
    return pl.pallas_call(
        _hgrn_kernel,
        grid=(b, s // tb),
        in_specs=[tok, tok3, tok, tok, _const_spec(gout.shape), _const_spec(span.shape)],
        out_specs=tok,
        out_shape=jax.ShapeDtypeStruct((b, s, HG_WIDTH), BF16),
        scratch_shapes=[pltpu.VMEM((HG_HEADS, HG_DIM, HG_DIM), F32)],
        compiler_params=pltpu.CompilerParams(
            dimension_semantics=("arbitrary", "arbitrary"), vmem_limit_bytes=VMEM_LIMIT),
        name="hgrn2_recurrence",
    )(hq, hf, hi, hg, gout, span)


def _output_kernel(x_ref, attn_ref, rec_ref, gates_ref, p_ref,
                   wb_ref, wout_ref, gffn_ref, wgate_ref, wup_ref, wdown_ref,
                   gpg_ref, wpg_ref, wpp_ref, gpost_ref, o_ref):
    halves = _row_parts(x_ref.shape[0])
    merged = []
    for rows in halves:
        y0 = _dot(attn_ref[rows, :], wb_ref[:BRANCH_WIDTH, :])
        y1 = _dot(rec_ref[rows, :], wb_ref[BRANCH_WIDTH:, :])
        g0 = gates_ref[rows, :D_MODEL].astype(F32)
        g1 = gates_ref[rows, D_MODEL:].astype(F32)
        merged.append((g0 * y0 + g1 * y1).astype(BF16))
    x1 = [x_ref[rows, :] + _dot(m, wout_ref[...]) for rows, m in zip(halves, merged)]
    e = [_rms(_dot(p_ref[rows, :].astype(BF16), wpp_ref[...]), gpost_ref[...]) for rows in halves]
    h2 = [_rms(v, gffn_ref[...]).astype(BF16) for v in x1]
    x2 = list(x1)
    lo = 0
    for width in FFN_SPLITS:
        for n in range(len(halves)):
            a = _dot(h2[n], wgate_ref[:, lo:lo + width])
            u = _dot(h2[n], wup_ref[:, lo:lo + width])
            z = (a * jax.nn.sigmoid(a) * u).astype(BF16)
            x2[n] = x2[n] + _dot(z, wdown_ref[lo:lo + width, :])
        lo += width
    for n, rows in enumerate(halves):
        g = jax.nn.sigmoid(_dot(_rms(x2[n], gpg_ref[...]).astype(BF16), wpg_ref[...]))
        o_ref[rows, :] = x2[n] + g * e[n]


def _output_stage(x2d, attn, rec, gates, p2d, wb, wout, gffn, wgate, wup, wdown,
                  gpg, wpg, wpp, gpost):
    t = x2d.shape[0]
    tm = TOK_TILE

    def tok(width):
        return pl.BlockSpec((tm, width), lambda i: (i, 0))

    consts = (wb, wout, gffn, wgate, wup, wdown, gpg, wpg, wpp, gpost)
    return pl.pallas_call(
        _output_kernel,
        grid=(t // tm,),
        in_specs=[tok(D_MODEL), tok(BRANCH_WIDTH), tok(HG_WIDTH), tok(2 * D_MODEL), tok(PLE_DIM)]
        + [_const_spec(c.shape) for c in consts],
        out_specs=tok(D_MODEL),
        out_shape=jax.ShapeDtypeStruct((t, D_MODEL), F32),
        compiler_params=pltpu.CompilerParams(
            dimension_semantics=("arbitrary",), vmem_limit_bytes=VMEM_LIMIT),
        name="output_stage",
    )(x2d, attn, rec, gates, p2d, *consts)


def _pad_heads(w, heads, width):
    rows = w.shape[0]
    w = w.reshape(rows, heads, width)
    w = jnp.pad(w, ((0, 0), (0, 0), (0, HEAD_PAD - width)))
    return w.reshape(rows, heads * HEAD_PAD)


def _qk_head_layout(nope, rope):
    like = nope if nope is not None else rope

    def z(width):
        return jnp.zeros(like.shape[:-1] + (width,), like.dtype)

    n0 = nope[..., :HEAD_HALF_NOPE] if nope is not None else z(HEAD_HALF_NOPE)
    n1 = nope[..., HEAD_HALF_NOPE:] if nope is not None else z(HEAD_HALF_NOPE)
    x1 = rope[..., :ROPE_HALF] if rope is not None else z(ROPE_HALF)
    x2 = rope[..., ROPE_HALF:] if rope is not None else z(ROPE_HALF)
    return jnp.concatenate([n0, x1, z(ROPE_HALF), n1, x2, z(ROPE_HALF)], axis=-1)


def _row(v):
    return v.reshape(1, -1).astype(F32)


def kernel(x, p, positions, mix_norm_g, w_in, q_a_norm_g, w_uq, kv_a_norm_g, w_ukv, q_norm_g, k_norm_g, hg_lb_logits, hg_out_norm_g, w_branch, w_out, ffn_norm_g, w_ffn_gate, w_ffn_up, w_ffn_down, ple_gate_norm_g, w_ple_gate, w_ple_proj, ple_post_norm_g):
    b, s, d = x.shape
    depth = w_in.shape[0]
    assert d == D_MODEL and s % max(HG_BLOCK, ATT_TILE, TOK_TILE) == 0, (b, s, d)
    assert w_in.shape[1:] == (D_MODEL, W_ALL_COLS - HEAD_PAD + QK_ROPE_DIM), w_in.shape
    lower_bounds = jnp.cumsum(jax.nn.softmax(hg_lb_logits.astype(F32), axis=0), axis=0)
    pos3d = positions.reshape(b * s // TOK_TILE, 1, TOK_TILE)
    x2d = x.reshape(b * s, d)
    q_scale = QK_HEAD_DIM ** -0.5 * math.log2(math.e)

    for layer in range(depth):
        wi = w_in[layer]
        wall = _input_weight_layout(wi.T)
        wq3 = w_uq[layer].reshape(Q_LORA_RANK, MLA_HEADS, QK_HEAD_DIM)
        wuq = _qk_head_layout(wq3[..., :QK_NOPE_DIM], wq3[..., QK_NOPE_DIM:])
        wuq = wuq.reshape(Q_LORA_RANK, MLA_HEADS * HEAD_PAD).astype(BF16)
        wukv = w_ukv[layer].reshape(KV_LORA_RANK, MLA_HEADS, QK_NOPE_DIM + V_HEAD_DIM)
        wk = _qk_head_layout(wukv[..., :QK_NOPE_DIM], None).reshape(KV_LORA_RANK, -1)
        wv = _pad_heads(wukv[:, :, QK_NOPE_DIM:].reshape(KV_LORA_RANK, -1), MLA_HEADS, V_HEAD_DIM)
        wkv = jnp.concatenate([wk, wv], axis=1).astype(BF16)
        gq = q_norm_g[layer].astype(F32) * q_scale
        gk = k_norm_g[layer].astype(F32)

        side = (w_branch[layer].reshape(2 * BRANCH_WIDTH, d), w_out[layer], w_ffn_gate[layer],
                w_ffn_up[layer], w_ffn_down[layer], w_ple_gate[layer], w_ple_proj[layer])
        (q, k, v, hq, hf, hi, hg, gates), side_bf = _input_stage(
            x2d, pos3d, b, s, _row(mix_norm_g[layer]), wall, _row(q_a_norm_g[layer]), wuq,
            _row(kv_a_norm_g[layer]), wkv,
            _row(_qk_head_layout(gq[:QK_NOPE_DIM], gq[QK_NOPE_DIM:])),
            _row(_qk_head_layout(gk[:QK_NOPE_DIM], gk[QK_NOPE_DIM:])),
            _row(lower_bounds[layer]), side)
        wb, wout, wgate, wup, wdown, wpg, wpp = side_bf

        attn = _attention_stage(q, k, v)
        rec = _hgrn_stage(hq.reshape(b, s, HG_WIDTH), hf.reshape(b, s, 3 * HG_WIDTH),
                          hi.reshape(b, s, HG_WIDTH), hg.reshape(b, s, HG_WIDTH),
                          _row(hg_out_norm_g[layer]))

        x2d = _output_stage(
            x2d, attn.reshape(b * s, BRANCH_WIDTH), rec.reshape(b * s, HG_WIDTH), gates,
            p[layer].reshape(b * s, PLE_DIM),
            wb, wout, _row(ffn_norm_g[layer]),
            wgate, wup, wdown, _row(ple_gate_norm_g[layer]), wpg, wpp,
            _row(ple_post_norm_g[layer]))
    return x2d.reshape(b, s, d)
```

```python
import math

import numpy as np

import jax
import jax.numpy as jnp
from jax import lax
from jax.experimental import pallas as pl
from jax.experimental.pallas import tpu as pltpu

D_MODEL = 1024
MLA_HEADS = 8
QK_NOPE_DIM = 64
QK_ROPE_DIM = 32
ROPE_HALF = QK_ROPE_DIM // 2
QK_HEAD_DIM = QK_NOPE_DIM + QK_ROPE_DIM
V_HEAD_DIM = 64
Q_LORA_RANK = 384
KV_LORA_RANK = 256
ROPE_BASE = 10000.0
HG_HEADS = 4
HG_DIM = 128
HG_WIDTH = HG_HEADS * HG_DIM
BRANCH_WIDTH = MLA_HEADS * V_HEAD_DIM
FFN_HIDDEN = 2816
PLE_DIM = 256
EPS = 1e-6

LANES = 128
BF16_ROWS = 16
HEAD_PAD = LANES
HEAD_HALF_NOPE = QK_NOPE_DIM // 2
W_LAT = Q_LORA_RANK + KV_LORA_RANK + HEAD_PAD

VMEM_LIMIT = 60 * 1024 * 1024

TOK_TILE = 512
ROW_PARTS = 2
ATT_TILE = 512
ATT_HEADS = 8
HG_BLOCK = 1024
HG_CHUNK = 64
HG_LEVELS = tuple(HG_CHUNK >> (i + 1) for i in range(HG_CHUNK.bit_length() - 1))
FFN_SPLITS = (1024, 1024, 768)
assert sum(FFN_SPLITS) == FFN_HIDDEN

BF16 = jnp.bfloat16
F32 = jnp.float32


def _const_spec(shape):
    nd = len(shape)
    return pl.BlockSpec(shape, lambda *_: (0,) * nd, pipeline_mode=pl.Buffered(1))


def _rms(v, gain):
    ms = jnp.sum(v * v, axis=-1, keepdims=True) * (1.0 / v.shape[-1])
    return v * lax.rsqrt(ms + EPS) * gain


def _dot(a, b):
    return jnp.dot(a, b, preferred_element_type=F32)


def _row_parts(n_rows):
    step = n_rows // ROW_PARTS
    return tuple(slice(i * step, (i + 1) * step) for i in range(ROW_PARTS))


def _dot_nt(a, b):
    return lax.dot_general(a, b, (((1,), (1,)), ((), ())), preferred_element_type=F32)


def _rope_tables(pos_row):
    tm = pos_row.shape[1]
    fidx = lax.broadcasted_iota(jnp.int32, (ROPE_HALF, 1), 0).astype(F32)
    inv_freq = jnp.exp(fidx * (-math.log(ROPE_BASE) * 2.0 / QK_ROPE_DIM))
    ang = pos_row.astype(F32) * inv_freq
    cos = jnp.cos(ang)
    sin = jnp.sin(ang)
    ones = jnp.ones((HEAD_HALF_NOPE, tm), F32)
    zeros_n = jnp.zeros((HEAD_HALF_NOPE, tm), F32)
    zeros_p = jnp.zeros((ROPE_HALF, tm), F32)
    cos_t = jnp.concatenate([ones, cos, zeros_p, ones, cos, zeros_p], axis=0)
    sin_t = jnp.concatenate([zeros_n, -sin, zeros_p, zeros_n, sin, zeros_p], axis=0)
    return cos_t.T, sin_t.T


def _input_kernel(x_ref, pos_ref, gmix_ref, wall_ref, gqa_ref, wuq_ref, gkva_ref, wkv_ref,
                  gq_ref, gk_ref, *rest):
    n_side = (len(rest) - 8) // 2
    side_in = rest[:n_side]
    q_ref, k_ref, v_ref, hq_ref, hf_ref, hi_ref, hg_ref, gates_ref = rest[n_side:n_side + 8]
    side_out = rest[n_side + 8:]
    for w_ref, wb_ref in zip(side_in, side_out):
        wb_ref[...] = w_ref[...].astype(BF16)

    halves = _row_parts(x_ref.shape[0])
    gq = gq_ref[...]
    gk = gk_ref[...]
    lane = lax.broadcasted_iota(jnp.int32, (1, LANES), 1)
    one_lane = (lane >= V_HEAD_DIM).astype(F32)
    inv_width = 1.0 / QK_HEAD_DIM

    h = [_rms(x_ref[rows, :], gmix_ref[...]).astype(BF16) for rows in halves]
    lat = [_dot(v, wall_ref[:, :W_LAT]) for v in h]
    cq = [_rms(v[:, :Q_LORA_RANK], gqa_ref[...]).astype(BF16) for v in lat]
    ckv = [_rms(v[:, Q_LORA_RANK:Q_LORA_RANK + KV_LORA_RANK], gkva_ref[...]).astype(BF16)
           for v in lat]
    q_all = [_dot(v, wuq_ref[...]) for v in cq]
    kv_all = [_dot(v, wkv_ref[...]) for v in ckv]

    for n, rows in enumerate(halves):
        k_rope = lat[n][:, Q_LORA_RANK + KV_LORA_RANK:]
        cos_tab, sin_tab = _rope_tables(pos_ref[0, :, rows])
        q_cos = cos_tab * gq
        q_sin = sin_tab * pltpu.roll(gq, LANES // 2, 1)
        kr = k_rope * gk
        kr = kr * cos_tab + pltpu.roll(kr, LANES // 2, 1) * sin_tab
        kr_ss = jnp.sum(k_rope * k_rope, axis=-1, keepdims=True)
        for hd in range(MLA_HEADS):
            sl = slice(hd * HEAD_PAD, (hd + 1) * HEAD_PAD)
            t = q_all[n][:, sl]
            r = lax.rsqrt(jnp.sum(t * t, axis=-1, keepdims=True) * inv_width + EPS)
            q_ref[0, hd, rows, :] = ((t * q_cos + pltpu.roll(t, LANES // 2, 1) * q_sin) * r).astype(BF16)
            t = kv_all[n][:, sl]
            r = lax.rsqrt((jnp.sum(t * t, axis=-1, keepdims=True) + kr_ss) * inv_width + EPS)
            k_ref[0, hd, rows, :] = ((t * gk + kr) * r).astype(BF16)
            vsl = slice(MLA_HEADS * HEAD_PAD + hd * HEAD_PAD, MLA_HEADS * HEAD_PAD + (hd + 1) * HEAD_PAD)
            v_ref[0, hd, rows, :] = (kv_all[n][:, vsl] + one_lane).astype(BF16)

    for n, rows in enumerate(halves):
        hh = _dot(h[n], wall_ref[:, W_LAT:W_LAT + 4 * HG_WIDTH])
        hq_ref[rows, :] = hh[:, :HG_WIDTH].astype(BF16)
        hf_ref[rows, :] = hh[:, HG_WIDTH:2 * HG_WIDTH]
        hi_ref[rows, :] = hh[:, 2 * HG_WIDTH:3 * HG_WIDTH].astype(BF16)
        hg_ref[rows, :] = hh[:, 3 * HG_WIDTH:].astype(BF16)
    for n, rows in enumerate(halves):
        gates_ref[rows, :] = jax.nn.sigmoid(
            _dot(h[n], wall_ref[:, W_LAT + 4 * HG_WIDTH:])).astype(BF16)


W_ALL_COLS = W_LAT + 4 * HG_WIDTH + 2 * D_MODEL
W_PREP_COLS = W_ALL_COLS // 2


def _input_weight_kernel(wt_ref, o_ref):
    c0 = Q_LORA_RANK + KV_LORA_RANK
    j = pl.program_id(0)
    n = W_PREP_COLS
    lat_full = c0 // n
    lat_rest = c0 - lat_full * n

    @pl.when(j != lat_full)
    def _():
        o_ref[...] = wt_ref[...].T.astype(BF16)

    @pl.when(j == lat_full)
    def _():
        d = wt_ref.shape[1]
        z_n = jnp.zeros((HEAD_HALF_NOPE, d), F32)
        z_p = jnp.zeros((ROPE_HALF, d), F32)
        parts = [wt_ref[:lat_rest, :], z_n, wt_ref[lat_rest:lat_rest + ROPE_HALF, :], z_p,
                 z_n, wt_ref[lat_rest + ROPE_HALF:lat_rest + QK_ROPE_DIM, :], z_p]
        tail = n - lat_rest - HEAD_PAD
        if tail:
            parts.append(wt_ref[lat_rest + QK_ROPE_DIM:lat_rest + QK_ROPE_DIM + tail, :])
        o_ref[...] = jnp.concatenate(parts, axis=0).T.astype(BF16)


def _input_weight_layout(wt):
    cols, rows = wt.shape
    out_cols = cols - QK_ROPE_DIM + HEAD_PAD
    n = W_PREP_COLS
    lat_full = (Q_LORA_RANK + KV_LORA_RANK) // n
    pad = HEAD_PAD - QK_ROPE_DIM

    def in_rows(j):
        return (pl.multiple_of(jnp.where(j <= lat_full, j * n, j * n - pad), QK_ROPE_DIM), 0)

    return pl.pallas_call(
        _input_weight_kernel,
        grid=(out_cols // n,),
        in_specs=[pl.BlockSpec((pl.Element(n), pl.Element(rows)), in_rows)],
        out_specs=pl.BlockSpec((rows, n), lambda j: (0, j)),
        out_shape=jax.ShapeDtypeStruct((rows, out_cols), BF16),
        compiler_params=pltpu.CompilerParams(
            dimension_semantics=("arbitrary",), vmem_limit_bytes=VMEM_LIMIT),
        name="input_weight_layout",
    )(wt)


def _side_cast_spec(rows, cols, steps):
    for share in (1, 2):
        blk, rem = divmod(rows * share, steps)
        if rem == 0 and blk % BF16_ROWS == 0:
            return pl.BlockSpec((blk, cols), lambda i: (i // share, 0))
    raise ValueError(f"no row block for a ({rows},{cols}) weight over {steps} steps")


def _input_stage(x2d, pos3d, b, s, gmix, wall, gqa, wuq, gkva, wkv, gq, gk, side_weights):
    t = x2d.shape[0]
    tm = TOK_TILE
    nt = s // tm
    steps = t // tm

    def tok(width):
        return pl.BlockSpec((tm, width), lambda i: (i, 0))

    def head_spec():
        return pl.BlockSpec((1, MLA_HEADS, tm, HEAD_PAD), lambda i: (i // nt, 0, i % nt, 0))

    head_shape = jax.ShapeDtypeStruct((b, MLA_HEADS, s, HEAD_PAD), BF16)
    consts = (gmix, wall, gqa, wuq, gkva, wkv, gq, gk)
    side_specs = [_side_cast_spec(w.shape[0], w.shape[1], steps) for w in side_weights]
    outs = pl.pallas_call(
        _input_kernel,
        grid=(steps,),
        in_specs=[tok(D_MODEL), pl.BlockSpec((1, 1, tm), lambda i: (i, 0, 0))]
        + [_const_spec(c.shape) for c in consts] + side_specs,
        out_specs=[head_spec(), head_spec(), head_spec(),
                   tok(HG_WIDTH), tok(HG_WIDTH), tok(HG_WIDTH), tok(HG_WIDTH), tok(2 * D_MODEL)]
        + side_specs,
        out_shape=[head_shape, head_shape, head_shape,
                   jax.ShapeDtypeStruct((t, HG_WIDTH), BF16),
                   jax.ShapeDtypeStruct((t, HG_WIDTH), F32),
                   jax.ShapeDtypeStruct((t, HG_WIDTH), BF16),
                   jax.ShapeDtypeStruct((t, HG_WIDTH), BF16),
                   jax.ShapeDtypeStruct((t, 2 * D_MODEL), BF16)]
        + [jax.ShapeDtypeStruct(w.shape, BF16) for w in side_weights],
        compiler_params=pltpu.CompilerParams(
            dimension_semantics=("arbitrary",), vmem_limit_bytes=VMEM_LIMIT),
        name="input_stage",
    )(x2d, pos3d, *consts, *side_weights)
    return outs[:8], outs[8:]


def _attention_kernel(q_ref, k_ref, v_ref, o_ref, m_ref, acc_ref, s_ref):
    qi = pl.program_id(2)
    tq = ATT_TILE
    half = tq // 2
    m_ref[...] = jnp.full(m_ref.shape, -jnp.inf, F32)
    acc_ref[...] = jnp.zeros(acc_ref.shape, F32)

    all_rows = slice(0, tq)

    def scores(j, hd, rows, n_keys):
        start = pl.multiple_of(j * tq, tq)
        s_ref[hd, rows, :n_keys] = _dot_nt(q_ref[0, hd, rows, :],
                                           k_ref[0, hd, pl.ds(start, n_keys), :])

    def accumulate(j, hd, rows, n_keys, visible):
        start = pl.multiple_of(j * tq, tq)
        sc = s_ref[hd, rows, :n_keys]
        if visible is not None:
            sc = jnp.where(visible, sc, -jnp.inf)
        m_old = m_ref[hd, rows, :]
        m_new = jnp.maximum(m_old, jnp.max(sc, axis=-1, keepdims=True))
        p = jnp.exp2(sc - jnp.concatenate([m_new] * (n_keys // LANES), axis=1)).astype(BF16)
        acc_ref[hd, rows, :] = (jnp.exp2(m_old - m_new) * acc_ref[hd, rows, :]
                                + _dot(p, v_ref[0, hd, pl.ds(start, n_keys), :]))
        m_ref[hd, rows, :] = m_new

    def block(j, rows, n_keys, visible):
        for hd in range(ATT_HEADS):
            scores(j, hd, rows, n_keys)
        for hd in range(ATT_HEADS):
            accumulate(j, hd, rows, n_keys, visible)

    def body(t, carry):
        for hd in range(ATT_HEADS):
            scores(2 * t, hd, all_rows, tq)
        for hd in range(ATT_HEADS):
            accumulate(2 * t, hd, all_rows, tq, None)
            scores(2 * t + 1, hd, all_rows, tq)
        for hd in range(ATT_HEADS):
            accumulate(2 * t + 1, hd, all_rows, tq, None)
        return carry

    lax.fori_loop(0, qi // 2, body, 0)

    @pl.when(qi % 2 == 1)
    def _():
        block(qi - 1, slice(0, tq), tq, None)

    def lower(n_keys, shift):
        r_i = lax.broadcasted_iota(jnp.int32, (half, n_keys), 0)
        c_i = lax.broadcasted_iota(jnp.int32, (half, n_keys), 1)
        return c_i <= r_i + shift

    block(qi, slice(0, half), half, lower(half, 0))
    block(qi, slice(half, tq), tq, lower(tq, half))

    low = lax.broadcasted_iota(jnp.int32, (tq, LANES), 1) < V_HEAD_DIM
    for pair in range(ATT_HEADS // 2):
        even, odd = acc_ref[2 * pair], acc_ref[2 * pair + 1]
        out = jnp.where(low, even / pltpu.roll(even, V_HEAD_DIM, 1), pltpu.roll(odd, V_HEAD_DIM, 1) / odd)
        o_ref[0, :, pair * LANES:(pair + 1) * LANES] = out.astype(BF16)


def _attention_stage(q, k, v):
    b, nh, s, _ = q.shape
    tq = ATT_TILE
    g = ATT_HEADS
    return pl.pallas_call(
        _attention_kernel,
        grid=(b, nh // g, s // tq),
        in_specs=[pl.BlockSpec((1, g, tq, HEAD_PAD), lambda bi, hp, i: (bi, hp, i, 0)),
                  pl.BlockSpec((1, g, s, HEAD_PAD), lambda bi, hp, i: (bi, hp, 0, 0)),
                  pl.BlockSpec((1, g, s, HEAD_PAD), lambda bi, hp, i: (bi, hp, 0, 0))],
        out_specs=pl.BlockSpec((1, tq, g * V_HEAD_DIM), lambda bi, hp, i: (bi, i, hp)),
        out_shape=jax.ShapeDtypeStruct((b, s, BRANCH_WIDTH), BF16),
        scratch_shapes=[pltpu.VMEM((g, tq, LANES), F32), pltpu.VMEM((g, tq, HEAD_PAD), F32),
                        pltpu.VMEM((g, tq, tq), F32)],
        compiler_params=pltpu.CompilerParams(
            dimension_semantics=("arbitrary", "arbitrary", "arbitrary"),
            vmem_limit_bytes=VMEM_LIMIT),
        name="causal_attention",
    )(q, k, v)


def _hgrn_span_matrix():
    c = HG_CHUNK
    x = np.arange(c)[:, None]
    y = np.arange(c)[None, :]
    mats = [(y <= x)]
    for b in HG_LEVELS:
        r = (x // (2 * b)) * (2 * b) + b
        mats.append((y > np.minimum(x, r)) & (y <= np.maximum(x, r)))
    m = np.concatenate(mats, axis=0).astype(np.float32)
    return np.concatenate([m, m], axis=1)


def _block_diag_rows(x):
    a, b = x[:, :HG_DIM], x[:, HG_DIM:]
    zero = jnp.zeros_like(a)
    return jnp.concatenate([jnp.concatenate([a, zero], axis=1),
                            jnp.concatenate([zero, b], axis=1)], axis=0)


def _hgrn_chunk_local(q, hf, v_bf, lb, span, right_rows, pair_masks):
    c = HG_CHUNK
    f = lb + (1.0 - lb) * jax.nn.sigmoid(hf)
    lf = jnp.log2(f)
    kk = 1.0 - f
    lf_hi = lf.astype(BF16)
    lf_lo = (lf - lf_hi.astype(F32)).astype(BF16)
    spans = _dot(span, jnp.concatenate([lf_hi, lf_lo], axis=0))
    yield
    cum = spans[:c]
    last = cum[c - 1:c, :]
    q_dec = (q * jnp.exp2(cum)).astype(BF16)
    k_dec = (kk * jnp.exp2(last - cum)).astype(BF16)
    q_bf = q.astype(BF16)
    k_bf = kk.astype(BF16)
    zs = [(jnp.where(right_rows[lvl], q, kk)
           * jnp.exp2(spans[(lvl + 1) * c:(lvl + 2) * c])).astype(BF16)
          for lvl in range(len(HG_LEVELS))]

    o_intra, incr = [], []
    for pair in range(HG_HEADS // 2):
        psl = slice(2 * pair * HG_DIM, (2 * pair + 2) * HG_DIM)
        att = jnp.where(pair_masks[0], _dot_nt(q_bf[:, psl], _block_diag_rows(k_bf[:, psl])), 0.0)
        yield
        for lvl in range(len(HG_LEVELS)):
            z = zs[lvl][:, psl]
            att = jnp.where(pair_masks[lvl + 1], _dot_nt(z, _block_diag_rows(z)), att)
            yield
        v_pair = v_bf[:, psl]
        o_intra.append(_dot(att.astype(BF16), _block_diag_rows(v_pair)))
        yield
        for hd in (2 * pair, 2 * pair + 1):
            sl = slice(hd * HG_DIM, (hd + 1) * HG_DIM)
            incr.append(_dot(v_bf[:, sl].astype(F32).T.astype(BF16), k_dec[:, sl]))
            yield
    return o_intra, q_dec, incr, jnp.exp2(last)


def _hgrn_kernel(hq_ref, hf_ref, hi_ref, hg_ref, lb_ref, gout_ref, span_ref, o_ref, state_ref):
    @pl.when(pl.program_id(1) == 0)
    def _():
        state_ref[...] = jnp.zeros(state_ref.shape, F32)

    c = HG_CHUNK
    t_i = lax.broadcasted_iota(jnp.int32, (c, 2 * c), 0)
    s_i = lax.broadcasted_iota(jnp.int32, (c, 2 * c), 1) % c
    row = lax.broadcasted_iota(jnp.int32, (c, HG_WIDTH), 0)
    pair_masks = [t_i == s_i] + [((t_i // b) ^ (s_i // b) == 1) & (s_i < t_i) for b in HG_LEVELS]
    right_rows = [(row // b) % 2 == 1 for b in HG_LEVELS]
    span = span_ref[...]
    lb = lb_ref[...]
    n_chunks = HG_BLOCK // c

    local = [None] * n_chunks
    group = 2
    for c0 in range(0, n_chunks, group):
        gens = {}
        for ci in range(c0, c0 + group):
            rows = slice(ci * c, (ci + 1) * c)
            gens[ci] = _hgrn_chunk_local(hq_ref[0, rows, :].astype(F32), hf_ref[0, rows, :],
                                         hi_ref[0, rows, :], lb, span, right_rows, pair_masks)
        while gens:
            for ci in list(gens):
                try:
                    next(gens[ci])
                except StopIteration as done:
                    local[ci] = done.value
                    del gens[ci]

    gout = gout_ref[...]
    for pair in range(HG_HEADS // 2):
        hds = (2 * pair, 2 * pair + 1)
        psl = slice(2 * pair * HG_DIM, (2 * pair + 2) * HG_DIM)
        states = [state_ref[hd] for hd in hds]
        for ci in range(n_chunks):
            rows = slice(ci * c, (ci + 1) * c)
            o_intra, q_dec, incr, decay = local[ci]
            both = jnp.concatenate([st.astype(BF16) for st in states], axis=1)
            o_pair = o_intra[pair] + _dot_nt(q_dec[:, psl], _block_diag_rows(both))
            for n, hd in enumerate(hds):
                sl = slice(hd * HG_DIM, (hd + 1) * HG_DIM)
                states[n] = states[n] * decay[:, sl] + incr[hd]
                o = o_pair[:, n * HG_DIM:(n + 1) * HG_DIM]
                gate = hg_ref[0, rows, sl].astype(F32)
                o_ref[0, rows, sl] = (_rms(o, gout) * (gate * jax.nn.sigmoid(gate))).astype(BF16)
        for n, hd in enumerate(hds):
            state_ref[hd] = states[n]


def _hgrn_stage(hq, hf, hi, hg, lb, gout):
    span = jnp.asarray(_hgrn_span_matrix(), BF16)
    b, s, _ = hq.shape
    tb = HG_BLOCK
    tok = pl.BlockSpec((1, tb, HG_WIDTH), lambda bi, i: (bi, i, 0))
    return pl.pallas_call(
        _hgrn_kernel,
        grid=(b, s // tb),
        in_specs=[tok, tok, tok, tok, _const_spec(lb.shape), _const_spec(gout.shape),
                  _const_spec(span.shape)],
        out_specs=tok,
        out_shape=jax.ShapeDtypeStruct((b, s, HG_WIDTH), BF16),
        scratch_shapes=[pltpu.VMEM((HG_HEADS, HG_DIM, HG_DIM), F32)],
        compiler_params=pltpu.CompilerParams(
            dimension_semantics=("arbitrary", "arbitrary"), vmem_limit_bytes=VMEM_LIMIT),
        name="hgrn2_recurrence",
    )(hq, hf, hi, hg, lb, gout, span)


def _output_kernel(x_ref, attn_ref, rec_ref, gates_ref, p_ref,
                   wb_ref, wout_ref, gffn_ref, wgate_ref, wup_ref, wdown_ref,
                   gpg_ref, wpg_ref, wpp_ref, gpost_ref, o_ref):
    halves = _row_parts(x_ref.shape[0])
    merged = []
    for rows in halves:
        y0 = _dot(attn_ref[rows, :], wb_ref[:BRANCH_WIDTH, :])
        y1 = _dot(rec_ref[rows, :], wb_ref[BRANCH_WIDTH:, :])
        g0 = gates_ref[rows, :D_MODEL].astype(F32)
        g1 = gates_ref[rows, D_MODEL:].astype(F32)
        merged.append((g0 * y0 + g1 * y1).astype(BF16))
    x1 = [x_ref[rows, :] + _dot(m, wout_ref[...]) for rows, m in zip(halves, merged)]
    e = [_rms(_dot(p_ref[rows, :].astype(BF16), wpp_ref[...]), gpost_ref[...]) for rows in halves]
    h2 = [_rms(v, gffn_ref[...]).astype(BF16) for v in x1]
    x2 = list(x1)
    lo = 0
    for width in FFN_SPLITS:
        for n in range(len(halves)):
            a = _dot(h2[n], wgate_ref[:, lo:lo + width])
            u = _dot(h2[n], wup_ref[:, lo:lo + width])
            z = (a * jax.nn.sigmoid(a) * u).astype(BF16)
            x2[n] = x2[n] + _dot(z, wdown_ref[lo:lo + width, :])
        lo += width
    for n, rows in enumerate(halves):
        g = jax.nn.sigmoid(_dot(_rms(x2[n], gpg_ref[...]).astype(BF16), wpg_ref[...]))
        o_ref[rows, :] = x2[n] + g * e[n]


def _output_stage(x2d, attn, rec, gates, p2d, wb, wout, gffn, wgate, wup, wdown,
                  gpg, wpg, wpp, gpost):
    t = x2d.shape[0]
    tm = TOK_TILE

    def tok(width):
        return pl.BlockSpec((tm, width), lambda i: (i, 0))

    consts = (wb, wout, gffn, wgate, wup, wdown, gpg, wpg, wpp, gpost)
    return pl.pallas_call(
        _output_kernel,
        grid=(t // tm,),
        in_specs=[tok(D_MODEL), tok(BRANCH_WIDTH), tok(HG_WIDTH), tok(2 * D_MODEL), tok(PLE_DIM)]
        + [_const_spec(c.shape) for c in consts],
        out_specs=tok(D_MODEL),
        out_shape=jax.ShapeDtypeStruct((t, D_MODEL), F32),
        compiler_params=pltpu.CompilerParams(
            dimension_semantics=("arbitrary",), vmem_limit_bytes=VMEM_LIMIT),
        name="output_stage",
    )(x2d, attn, rec, gates, p2d, *consts)


def _pad_heads(w, heads, width):
    rows = w.shape[0]
    w = w.reshape(rows, heads, width)
    w = jnp.pad(w, ((0, 0), (0, 0), (0, HEAD_PAD - width)))
    return w.reshape(rows, heads * HEAD_PAD)


def _qk_head_layout(nope, rope):
    like = nope if nope is not None else rope

    def z(width):
        return jnp.zeros(like.shape[:-1] + (width,), like.dtype)

    n0 = nope[..., :HEAD_HALF_NOPE] if nope is not None else z(HEAD_HALF_NOPE)
    n1 = nope[..., HEAD_HALF_NOPE:] if nope is not None else z(HEAD_HALF_NOPE)
    x1 = rope[..., :ROPE_HALF] if rope is not None else z(ROPE_HALF)
    x2 = rope[..., ROPE_HALF:] if rope is not None else z(ROPE_HALF)
    return jnp.concatenate([n0, x1, z(ROPE_HALF), n1, x2, z(ROPE_HALF)], axis=-1)


def _row(v):
    return v.reshape(1, -1).astype(F32)


def kernel(x, p, positions, mix_norm_g, w_in, q_a_norm_g, w_uq, kv_a_norm_g, w_ukv, q_norm_g, k_norm_g, hg_lb_logits, hg_out_norm_g, w_branch, w_out, ffn_norm_g, w_ffn_gate, w_ffn_up, w_ffn_down, ple_gate_norm_g, w_ple_gate, w_ple_proj, ple_post_norm_g):
    b, s, d = x.shape
    depth = w_in.shape[0]
    assert d == D_MODEL and s % max(HG_BLOCK, ATT_TILE, TOK_TILE) == 0, (b, s, d)
    assert w_in.shape[1:] == (D_MODEL, W_ALL_COLS - HEAD_PAD + QK_ROPE_DIM), w_in.shape
    lower_bounds = jnp.cumsum(jax.nn.softmax(hg_lb_logits.astype(F32), axis=0), axis=0)
    pos3d = positions.reshape(b * s // TOK_TILE, 1, TOK_TILE)
    x2d = x.reshape(b * s, d)
    q_scale = QK_HEAD_DIM ** -0.5 * math.log2(math.e)

    for layer in range(depth):
        wi = w_in[layer]
        wall = _input_weight_layout(wi.T)
        wq3 = w_uq[layer].reshape(Q_LORA_RANK, MLA_HEADS, QK_HEAD_DIM)
        wuq = _qk_head_layout(wq3[..., :QK_NOPE_DIM], wq3[..., QK_NOPE_DIM:])
        wuq = wuq.reshape(Q_LORA_RANK, MLA_HEADS * HEAD_PAD).astype(BF16)
        wukv = w_ukv[layer].reshape(KV_LORA_RANK, MLA_HEADS, QK_NOPE_DIM + V_HEAD_DIM)
        wk = _qk_head_layout(wukv[..., :QK_NOPE_DIM], None).reshape(KV_LORA_RANK, -1)
        wv = _pad_heads(wukv[:, :, QK_NOPE_DIM:].reshape(KV_LORA_RANK, -1), MLA_HEADS, V_HEAD_DIM)
        wkv = jnp.concatenate([wk, wv], axis=1).astype(BF16)
        gq = q_norm_g[layer].astype(F32) * q_scale
        gk = k_norm_g[layer].astype(F32)

        side = (w_branch[layer].reshape(2 * BRANCH_WIDTH, d), w_out[layer], w_ffn_gate[layer],
                w_ffn_up[layer], w_ffn_down[layer], w_ple_gate[layer], w_ple_proj[layer])
        (q, k, v, hq, hf, hi, hg, gates), side_bf = _input_stage(
            x2d, pos3d, b, s, _row(mix_norm_g[layer]), wall, _row(q_a_norm_g[layer]), wuq,
            _row(kv_a_norm_g[layer]), wkv,
            _row(_qk_head_layout(gq[:QK_NOPE_DIM], gq[QK_NOPE_DIM:])),
            _row(_qk_head_layout(gk[:QK_NOPE_DIM], gk[QK_NOPE_DIM:])), side)
        wb, wout, wgate, wup, wdown, wpg, wpp = side_bf

        attn = _attention_stage(q, k, v)
        rec = _hgrn_stage(hq.reshape(b, s, HG_WIDTH), hf.reshape(b, s, HG_WIDTH),
                          hi.reshape(b, s, HG_WIDTH), hg.reshape(b, s, HG_WIDTH),
                          _row(lower_bounds[layer]), _row(hg_out_norm_g[layer]))

        x2d = _output_stage(
            x2d, attn.reshape(b * s, BRANCH_WIDTH), rec.reshape(b * s, HG_WIDTH), gates,
            p[layer].reshape(b * s, PLE_DIM),
            wb, wout, _row(ffn_norm_g[layer]),
            wgate, wup, wdown, _row(ple_gate_norm_g[layer]), wpg, wpp,
            _row(ple_post_norm_g[layer]))
    return x2d.reshape(b, s, d)
```
